```python
import math
import jax, jax.numpy as jnp
from jax import lax
import numpy as np

D_MODEL = 2048
BATCH = 4
SEQ = 2048
DEPTH = 4
DEC_BATCH = 128
DEC_SEQ = 1
PAST_LEN = 16384
PAGE_SIZE = 128

POOL_WINDOWS = (2, 4, 8, 16)
POOL_GROUPS = 4
POOL_DIM = D_MODEL
POOL_GC = POOL_DIM // POOL_GROUPS
POOL_BUF = 15
D_INNER = 2 * D_MODEL
SSD_HEAD_DIM = 64
SSD_HEADS = D_INNER // SSD_HEAD_DIM
SSD_GROUPS = 8
SSD_HPG = SSD_HEADS // SSD_GROUPS
D_STATE = 128
CONV_WIDTH = 4
CONV_DIM = D_INNER + 2 * SSD_GROUPS * D_STATE
SSD_CHUNK = 128
N_MEM = 256
MEM_HEADS = 4
MEM_HEAD_DIM = 128
MEM_DIM = MEM_HEADS * MEM_HEAD_DIM
D_FF = -(-8 * D_MODEL // (3 * 256)) * 256
IN_DIM = POOL_DIM + D_INNER + CONV_DIM + SSD_HEADS + 2 * D_MODEL
SPLIT_1 = POOL_DIM
SPLIT_2 = SPLIT_1 + D_INNER
SPLIT_3 = SPLIT_2 + CONV_DIM
SPLIT_4 = SPLIT_3 + SSD_HEADS
SPLIT_5 = SPLIT_4 + D_MODEL
ALPHA = (2 * DEPTH) ** 0.25
BETA = (8 * DEPTH) ** -0.25
LN_EPS = 1e-5
RMS_EPS = 1e-5

kernel_name = 'hybrid_pool_ssd_memory_decoder_step'


def layer_norm(x, g, b):
    xf = x.astype(jnp.float32)
    mu = jnp.mean(xf, axis=-1, keepdims=True)
    var = jnp.mean(jnp.square(xf - mu), axis=-1, keepdims=True)
    return ((xf - mu) * lax.rsqrt(var + LN_EPS) * g + b).astype(x.dtype)


def multiscale_pool(u, buf, start_pos):
    L = u.shape[1]
    cat = jnp.concatenate([buf.astype(u.dtype), u], axis=1)
    cs = jnp.cumsum(cat.astype(jnp.float32), axis=1)
    cs0 = jnp.pad(cs, ((0, 0), (1, 0), (0, 0)))
    pos = start_pos + jnp.arange(L)
    outs = []
    for g, w in enumerate(POOL_WINDOWS):
        sl = slice(g * POOL_GC, (g + 1) * POOL_GC)
        hi = cs0[:, POOL_BUF + 1:POOL_BUF + 1 + L, sl]
        lo = cs0[:, POOL_BUF + 1 - w:POOL_BUF + 1 - w + L, sl]
        cnt = jnp.minimum(w, pos + 1).astype(jnp.float32)
        outs.append((hi - lo) / cnt[None, :, None])
    pooled = jnp.concatenate(outs, axis=-1) - u.astype(jnp.float32)
    return pooled, cat[:, -POOL_BUF:]


def causal_conv(xbc, buf, w, bias):
    L = xbc.shape[1]
    cat = jnp.concatenate([buf.astype(xbc.dtype), xbc], axis=1)
    out = bias.astype(jnp.float32)
    for k in range(CONV_WIDTH):
        out = out + cat[:, k:k + L].astype(jnp.float32) * w[k].astype(jnp.float32)
    return jax.nn.silu(out), cat[:, -(CONV_WIDTH - 1):]


def ssd_chunked(xh, dt, a, bm, cm, state0):
    b, L = xh.shape[0], xh.shape[1]
    c = min(SSD_CHUNK, L)
    nc = -(-L // c)
    pad = nc * c - L
    if pad:
        xh = jnp.pad(xh, ((0, 0), (0, pad), (0, 0), (0, 0)))
        dt = jnp.pad(dt, ((0, 0), (0, pad), (0, 0)))
        bm = jnp.pad(bm, ((0, 0), (0, pad), (0, 0), (0, 0)))
        cm = jnp.pad(cm, ((0, 0), (0, pad), (0, 0), (0, 0)))
    Lp = nc * c
    dA = (dt * a).reshape(b, Lp, SSD_GROUPS, SSD_HPG)
    xdt = (xh * dt[..., None]).reshape(b, Lp, SSD_GROUPS, SSD_HPG, SSD_HEAD_DIM)

    def to_chunks(t):
        return jnp.moveaxis(t.reshape((b, nc, c) + t.shape[2:]), 1, 0)

    causal = jnp.tril(jnp.ones((c, c), dtype=bool))[None, :, :, None, None]

    def step(state, inp):
        xdt_k, dA_k, b_k, c_k = inp
        cum = jnp.cumsum(dA_k, axis=1)
        seg = cum[:, :, None] - cum[:, None, :]
        decay = jnp.where(causal, jnp.exp(jnp.where(causal, seg, 0.0)), 0.0)
        cb = jnp.einsum('blgn,bsgn->blsg', c_k, b_k)
        y_diag = jnp.einsum('blsg,blsgh,bsghp->blghp', cb, decay, xdt_k)
        y_off = jnp.einsum('blgn,bghpn->blghp', c_k, state) * jnp.exp(cum)[..., None]
        last = cum[:, -1]
        w_s = jnp.exp(last[:, None] - cum)
        new_state = state * jnp.exp(last)[..., None, None] + jnp.einsum('bsgn,bsgh,bsghp->bghpn', b_k, w_s, xdt_k)
        return new_state, y_diag + y_off

    state_g = state0.reshape(b, SSD_GROUPS, SSD_HPG, SSD_HEAD_DIM, D_STATE)
    state_g, y = lax.scan(step, state_g, (to_chunks(xdt), to_chunks(dA), to_chunks(bm), to_chunks(cm)))
    y = jnp.moveaxis(y, 0, 1).reshape(b, Lp, SSD_HEADS, SSD_HEAD_DIM)[:, :L]
    return y, state_g.reshape(b, SSD_HEADS, SSD_HEAD_DIM, D_STATE)


def token_mixer(h, lp, pool_buf, conv_buf, ssd_state, start_pos):
    b, L, _ = h.shape
    proj = jnp.einsum('bld,dk->blk', h, lp['w_in'])
    u, z, xbc, dt_raw, g_pool, g_ssd = jnp.split(proj, [SPLIT_1, SPLIT_2, SPLIT_3, SPLIT_4, SPLIT_5], axis=-1)
    pooled, new_pool_buf = multiscale_pool(u, pool_buf, start_pos)
    pooled = jnp.einsum('blgc,gcd->blgd', pooled.reshape(b, L, POOL_GROUPS, POOL_GC).astype(h.dtype), lp['w_pool'])
    pooled = pooled.reshape(b, L, POOL_DIM) * lp['pool_scale']
    branch_a = jnp.einsum('blc,cd->bld', pooled.astype(h.dtype), lp['w_pool_br'])
    xbc, new_conv_buf = causal_conv(xbc, conv_buf, lp['conv_w'], lp['conv_b'])
    xs, bm, cm = jnp.split(xbc, [D_INNER, D_INNER + SSD_GROUPS * D_STATE], axis=-1)
    xh = xs.reshape(b, L, SSD_HEADS, SSD_HEAD_DIM)
    dt = jax.nn.softplus(dt_raw.astype(jnp.float32) + lp['dt_bias'].astype(jnp.float32))
    a = -jnp.exp(lp['a_log'].astype(jnp.float32))
    y, new_state = ssd_chunked(xh, dt, a,
                               bm.reshape(b, L, SSD_GROUPS, D_STATE), cm.reshape(b, L, SSD_GROUPS, D_STATE),
                               ssd_state.astype(jnp.float32))
    y = y + lp['d_skip'].astype(jnp.float32)[:, None] * xh
    y = y.reshape(b, L, D_INNER) * jax.nn.silu(z.astype(jnp.float32))
    yg = y.reshape(b, L, SSD_GROUPS, D_INNER // SSD_GROUPS)
    yg = yg * lax.rsqrt(jnp.mean(jnp.square(yg), axis=-1, keepdims=True) + RMS_EPS)
    y = yg.reshape(b, L, D_INNER) * lp['ssd_norm_w']
    branch_b = jnp.einsum('blc,cd->bld', y.astype(h.dtype), lp['w_ssd_br'])
    merged = jax.nn.sigmoid(g_pool) * branch_a + jax.nn.sigmoid(g_ssd) * branch_b
    out = jnp.einsum('bld,de->ble', merged.astype(h.dtype), lp['w_out'])
    return out, new_pool_buf, new_conv_buf, new_state


def memory_attention(h, mk, mv, w_mq, w_mo):
    b, L, _ = h.shape
    q = jnp.einsum('bld,dk->blk', h, w_mq).reshape(b, L, MEM_HEADS, MEM_HEAD_DIM)
    s = jnp.einsum('blhd,bmhd->bhlm', q.astype(jnp.float32), mk.astype(jnp.float32)) / math.sqrt(MEM_HEAD_DIM)
    p = jax.nn.softmax(s, axis=-1)
    o = jnp.einsum('bhlm,bmhd->blhd', p, mv.astype(jnp.float32)).reshape(b, L, MEM_DIM)
    return jnp.einsum('blk,kd->bld', o.astype(h.dtype), w_mo)


def swiglu(h, w_gate, w_up, w_down):
    g = jnp.einsum('bld,df->blf', h, w_gate)
    u = jnp.einsum('bld,df->blf', h, w_up)
    return jnp.einsum('blf,fd->bld', jax.nn.silu(g) * u, w_down)


def decoder_layer(x, mk, mv, pool_buf, conv_buf, ssd_state, start_pos, lp):
    mix, new_pool_buf, new_conv_buf, new_state = token_mixer(x, lp, pool_buf, conv_buf, ssd_state, start_pos)
    x = layer_norm(ALPHA * x + mix, lp['ln_mix_g'], lp['ln_mix_b'])
    ca = memory_attention(x, mk, mv, lp['w_mq'], lp['w_mo'])
    x = layer_norm(ALPHA * x + ca, lp['ln_mem_g'], lp['ln_mem_b'])
    f = swiglu(x, lp['w_ffn_gate'], lp['w_ffn_up'], lp['w_ffn_down'])
    x = layer_norm(ALPHA * x + f, lp['ln_ffn_g'], lp['ln_ffn_b'])
    return x, new_pool_buf, new_conv_buf, new_state


def setup_inputs(seed: int = 0) -> dict:
    key = jax.random.key(seed)
    k = jax.random.split(key, 40)
    f32 = jnp.float32

    def nrm(kk, shape, scale):
        return scale * jax.random.normal(kk, shape, f32)

    dt0 = jnp.exp(jax.random.uniform(k[10], (DEPTH, SSD_HEADS), f32, math.log(1e-3), math.log(1e-1)))
    return {
        'x_prompt': nrm(k[0], (BATCH, SEQ, D_MODEL), 1.0),
        'x_sample': nrm(k[1], (DEC_BATCH, DEC_SEQ, D_MODEL), 1.0),
        'mem_prompt': nrm(k[2], (BATCH, N_MEM, D_MODEL), 1.0),
        'state_ssd': nrm(k[3], (DEPTH, DEC_BATCH, SSD_HEADS, SSD_HEAD_DIM, D_STATE), 0.1),
        'state_conv': nrm(k[4], (DEPTH, DEC_BATCH, CONV_WIDTH - 1, CONV_DIM), 1.0),
        'state_pool': nrm(k[5], (DEPTH, DEC_BATCH, POOL_BUF, POOL_DIM), 1.0),
        'cache_mem_k': nrm(k[6], (DEPTH, DEC_BATCH, N_MEM, MEM_HEADS, MEM_HEAD_DIM), 1.0),
        'cache_mem_v': nrm(k[7], (DEPTH, DEC_BATCH, N_MEM, MEM_HEADS, MEM_HEAD_DIM), 1.0),
        'w_in': nrm(k[8], (DEPTH, D_MODEL, IN_DIM), D_MODEL ** -0.5),
        'w_pool': nrm(k[9], (DEPTH, POOL_GROUPS, POOL_GC, POOL_GC), POOL_GC ** -0.5),
        'pool_scale': 1.0 + nrm(k[11], (DEPTH, POOL_DIM), 0.1),
        'w_pool_br': nrm(k[12], (DEPTH, POOL_DIM, D_MODEL), BETA * POOL_DIM ** -0.5),
        'conv_w': nrm(k[13], (DEPTH, CONV_WIDTH, CONV_DIM), CONV_WIDTH ** -0.5),
        'conv_b': nrm(k[14], (DEPTH, CONV_DIM), 0.02),
        'dt_bias': dt0 + jnp.log(-jnp.expm1(-dt0)),
        'a_log': jnp.log(jax.random.uniform(k[15], (DEPTH, SSD_HEADS), f32, 1.0, 16.0)),
        'd_skip': 1.0 + nrm(k[16], (DEPTH, SSD_HEADS), 0.1),
        'ssd_norm_w': 1.0 + nrm(k[17], (DEPTH, D_INNER), 0.02),
        'w_ssd_br': nrm(k[18], (DEPTH, D_INNER, D_MODEL), BETA * D_INNER ** -0.5),
        'w_out': nrm(k[19], (DEPTH, D_MODEL, D_MODEL), BETA * D_MODEL ** -0.5),
        'ln_mix_g': 1.0 + nrm(k[20], (DEPTH, D_MODEL), 0.02),
        'ln_mix_b': nrm(k[21], (DEPTH, D_MODEL), 0.02),
        'w_mq': nrm(k[22], (DEPTH, D_MODEL, MEM_DIM), D_MODEL ** -0.5),
        'w_mk': nrm(k[23], (DEPTH, D_MODEL, MEM_DIM), D_MODEL ** -0.5),
        'w_mv': nrm(k[24], (DEPTH, D_MODEL, MEM_DIM), BETA * D_MODEL ** -0.5),
        'w_mo': nrm(k[25], (DEPTH, MEM_DIM, D_MODEL), BETA * MEM_DIM ** -0.5),
        'ln_mem_g': 1.0 + nrm(k[26], (DEPTH, D_MODEL), 0.02),
        'ln_mem_b': nrm(k[27], (DEPTH, D_MODEL), 0.02),
        'w_ffn_gate': nrm(k[28], (DEPTH, D_MODEL, D_FF), D_MODEL ** -0.5),
        'w_ffn_up': nrm(k[29], (DEPTH, D_MODEL, D_FF), BETA * D_MODEL ** -0.5),
        'w_ffn_down': nrm(k[30], (DEPTH, D_FF, D_MODEL), BETA * D_FF ** -0.5),
        'ln_ffn_g': 1.0 + nrm(k[31], (DEPTH, D_MODEL), 0.02),
        'ln_ffn_b': nrm(k[32], (DEPTH, D_MODEL), 0.02),
    }


def reference(x_prompt, x_sample, mem_prompt, state_ssd, state_conv, state_pool, cache_mem_k, cache_mem_v,
              w_in, w_pool, pool_scale, w_pool_br, conv_w, conv_b, dt_bias, a_log, d_skip, ssd_norm_w,
              w_ssd_br, w_out, ln_mix_g, ln_mix_b, w_mq, w_mk, w_mv, w_mo, ln_mem_g, ln_mem_b,
              w_ffn_gate, w_ffn_up, w_ffn_down, ln_ffn_g, ln_ffn_b):
    bp = x_prompt.shape[0]
    xp, xs = x_prompt, x_sample
    ssd_p, conv_p, pool_p, mk_list, mv_list = [], [], [], [], []
    ssd_s, conv_s, pool_s = [], [], []
    for l in range(DEPTH):
        lp = {
            'w_in': w_in[l], 'w_pool': w_pool[l], 'pool_scale': pool_scale[l], 'w_pool_br': w_pool_br[l],
            'conv_w': conv_w[l], 'conv_b': conv_b[l], 'dt_bias': dt_bias[l], 'a_log': a_log[l],
            'd_skip': d_skip[l], 'ssd_norm_w': ssd_norm_w[l], 'w_ssd_br': w_ssd_br[l], 'w_out': w_out[l],
            'ln_mix_g': ln_mix_g[l], 'ln_mix_b': ln_mix_b[l], 'w_mq': w_mq[l], 'w_mo': w_mo[l],
            'ln_mem_g': ln_mem_g[l], 'ln_mem_b': ln_mem_b[l], 'w_ffn_gate': w_ffn_gate[l],
            'w_ffn_up': w_ffn_up[l], 'w_ffn_down': w_ffn_down[l], 'ln_ffn_g': ln_ffn_g[l], 'ln_ffn_b': ln_ffn_b[l],
        }
        mk_p = jnp.einsum('bmd,dk->bmk', mem_prompt, w_mk[l]).reshape(bp, N_MEM, MEM_HEADS, MEM_HEAD_DIM)
        mv_p = jnp.einsum('bmd,dk->bmk', mem_prompt, w_mv[l]).reshape(bp, N_MEM, MEM_HEADS, MEM_HEAD_DIM)
        zero_pool = jnp.zeros((bp, POOL_BUF, POOL_DIM), x_prompt.dtype)
        zero_conv = jnp.zeros((bp, CONV_WIDTH - 1, CONV_DIM), x_prompt.dtype)
        zero_ssd = jnp.zeros((bp, SSD_HEADS, SSD_HEAD_DIM, D_STATE), jnp.float32)
        xp, pb, cb, st = decoder_layer(xp, mk_p, mv_p, zero_pool, zero_conv, zero_ssd, 0, lp)
        ssd_p.append(st.astype(x_prompt.dtype))
        conv_p.append(cb)
        pool_p.append(pb)
        mk_list.append(mk_p)
        mv_list.append(mv_p)
        xs, pb, cb, st = decoder_layer(xs, cache_mem_k[l], cache_mem_v[l], state_pool[l], state_conv[l],
                                       state_ssd[l], PAST_LEN, lp)
        ssd_s.append(st.astype(state_ssd.dtype))
        conv_s.append(cb)
        pool_s.append(pb)
    new_ssd_prompt = jnp.stack(ssd_p)
    new_conv_prompt = jnp.stack(conv_p)
    new_pool_prompt = jnp.stack(pool_p)
    new_mem_k_prompt = jnp.stack(mk_list)
    new_mem_v_prompt = jnp.stack(mv_list)
    new_ssd_sample = jnp.stack(ssd_s)
    new_conv_sample = jnp.stack(conv_s)
    new_pool_sample = jnp.stack(pool_s)
    return (xp, xs, new_ssd_prompt, new_conv_prompt, new_pool_prompt, new_mem_k_prompt, new_mem_v_prompt,
            new_ssd_sample, new_conv_sample, new_pool_sample)
```

```python
import functools
import math

import jax
import jax.numpy as jnp
from jax import lax
from jax.experimental import pallas as pl
from jax.experimental.pallas import tpu as pltpu

F32 = jnp.float32
BF16 = jnp.bfloat16

POOL_WINDOWS = (2, 4, 8, 16)
PAST_LEN = 16384
SSD_CHUNK = 128
LN_EPS = 1e-5
RMS_EPS = 1e-5
LANES = 128
VMEM_LIMIT_BYTES = 52 * 1024 * 1024


def _params(*sem):
    return pltpu.CompilerParams(dimension_semantics=sem, vmem_limit_bytes=VMEM_LIMIT_BYTES)


def _silu(x):
    return x * jax.nn.sigmoid(x)


def _softplus(x):
    return jnp.maximum(x, 0.0) + jnp.log(1.0 + jnp.exp(-jnp.abs(x)))


def _layer_norm(x, g, b):
    mu = jnp.mean(x, axis=-1, keepdims=True)
    xc = x - mu
    var = jnp.mean(xc * xc, axis=-1, keepdims=True)
    return xc * lax.rsqrt(var + LN_EPS) * g + b


def _split2(x):
    hi = x.astype(BF16)
    mid = (x - hi.astype(F32)).astype(BF16)
    return hi, mid


def _tile(n, pref):
    t = min(n, pref)
    while n % t:
        t //= 2
    return t


def _mm_kernel(x_ref, w_ref, o_ref, *, act):
    acc = jnp.dot(x_ref[...], w_ref[...], preferred_element_type=F32)
    if act == "silu":
        acc = _silu(acc)
    elif act == "sigmoid":
        acc = jax.nn.sigmoid(acc)
    o_ref[...] = acc.astype(o_ref.dtype)


def _mm(x, w, *, act=None, out_dtype=F32, name):
    m, k = x.shape
    n = w.shape[1]
    tm, tn = _tile(m, 1024), _tile(n, 1024)
    return pl.pallas_call(
        functools.partial(_mm_kernel, act=act),
        grid=(m // tm, n // tn),
        in_specs=[pl.BlockSpec((tm, k), lambda i, j: (i, 0)), pl.BlockSpec((k, tn), lambda i, j: (0, j))],
        out_specs=pl.BlockSpec((tm, tn), lambda i, j: (i, j)),
        out_shape=jax.ShapeDtypeStruct((m, n), out_dtype),
        compiler_params=_params("parallel", "arbitrary"),
        name=name,
    )(x, w)


def _pool_project(pooled_ref, wg_ref, scale_ref, wbr_ref, sg, n_groups, gc):
    for g in range(n_groups):
        cols = slice(g * gc, (g + 1) * gc)
        pg = jnp.dot(pooled_ref[:, cols], wg_ref[g], preferred_element_type=F32) * scale_ref[:, cols]
        pooled_ref[:, cols] = pg.astype(BF16)
    a = jnp.dot(pooled_ref[...], wbr_ref[...], preferred_element_type=F32)
    return a * sg.astype(F32)


def _pool_prompt_kernel(u_ref, wg_ref, scale_ref, wbr_ref, sg_ref, o_ref, ext_ref, pooled_ref, *, tl, gc):
    j = pl.program_id(1)
    halo = 16

    @pl.when(j == 0)
    def _():
        ext_ref[0:halo, :] = jnp.zeros((halo, ext_ref.shape[1]), F32)

    @pl.when(j > 0)
    def _():
        ext_ref[0:halo, :] = ext_ref[tl:tl + halo, :]

    ext_ref[halo:halo + tl, :] = u_ref[0]
    pos = j * tl + lax.broadcasted_iota(jnp.int32, (tl, 1), 0)
    for g, w in enumerate(POOL_WINDOWS):
        cols = slice(g * gc, (g + 1) * gc)
        ug = ext_ref[halo:halo + tl, cols]
        s = ug
        for k in range(1, w):
            s = s + ext_ref[halo - k:halo - k + tl, cols]
        cnt = jnp.minimum(w, pos + 1).astype(F32)
        pooled_ref[:, cols] = (s / cnt - ug).astype(BF16)
    o_ref[0] = _pool_project(pooled_ref, wg_ref, scale_ref, wbr_ref, sg_ref[0], len(POOL_WINDOWS), gc)


def _pool_prompt(u, wg, scale, wbr, sg):
    b, l, d = u.shape
    ng, gc, _ = wg.shape
    tl = _tile(l, 256)
    return pl.pallas_call(
        functools.partial(_pool_prompt_kernel, tl=tl, gc=gc),
        grid=(b, l // tl),
        in_specs=[
            pl.BlockSpec((1, tl, d), lambda i, j: (i, j, 0)),
            pl.BlockSpec((ng, gc, gc), lambda i, j: (0, 0, 0)),
            pl.BlockSpec((1, d), lambda i, j: (0, 0)),
            pl.BlockSpec((d, d), lambda i, j: (0, 0)),
            pl.BlockSpec((1, tl, d), lambda i, j: (i, j, 0)),
        ],
        out_specs=pl.BlockSpec((1, tl, d), lambda i, j: (i, j, 0)),
        out_shape=jax.ShapeDtypeStruct((b, l, d), F32),
        scratch_shapes=[pltpu.VMEM((tl + 16, d), F32), pltpu.VMEM((tl, d), BF16)],
        compiler_params=_params("parallel", "arbitrary"),
        name="pool_prompt",
    )(u, wg, scale, wbr, sg)


def _pool_sample_kernel(u_ref, buf_ref, wg_ref, scale_ref, wbr_ref, sg_ref, o_ref, nbuf_ref, pooled_ref, *, gc, nbuf):
    d = u_ref.shape[1]
    u = u_ref[...]
    nbuf_ref[:, 0:(nbuf - 1) * d] = buf_ref[:, d:nbuf * d]
    nbuf_ref[:, (nbuf - 1) * d:nbuf * d] = u
    for g, w in enumerate(POOL_WINDOWS):
        cols = slice(g * gc, (g + 1) * gc)
        ug = u[:, cols]
        s = ug
        for k in range(1, w):
            r = nbuf - k
            s = s + buf_ref[:, r * d + g * gc:r * d + (g + 1) * gc]
        cnt = float(min(w, PAST_LEN + 1))
        pooled_ref[:, cols] = (s / cnt - ug).astype(BF16)
    o_ref[...] = _pool_project(pooled_ref, wg_ref, scale_ref, wbr_ref, sg_ref[...], len(POOL_WINDOWS), gc)


def _pool_sample(u, buf2d, wg, scale, wbr, sg):
    bs, d = u.shape
    ng, gc, _ = wg.shape
    nbuf = buf2d.shape[1] // d
    tb = _tile(bs, 32)
    return pl.pallas_call(
        functools.partial(_pool_sample_kernel, gc=gc, nbuf=nbuf),
        grid=(bs // tb,),
        in_specs=[
            pl.BlockSpec((tb, d), lambda i: (i, 0)),
            pl.BlockSpec((tb, nbuf * d), lambda i: (i, 0)),
            pl.BlockSpec((ng, gc, gc), lambda i: (0, 0, 0)),
            pl.BlockSpec((1, d), lambda i: (0, 0)),
            pl.BlockSpec((d, d), lambda i: (0, 0)),
            pl.BlockSpec((tb, d), lambda i: (i, 0)),
        ],
        out_specs=[pl.BlockSpec((tb, d), lambda i: (i, 0)), pl.BlockSpec((tb, nbuf * d), lambda i: (i, 0))],
        out_shape=[jax.ShapeDtypeStruct((bs, d), F32), jax.ShapeDtypeStruct(buf2d.shape, F32)],
        scratch_shapes=[pltpu.VMEM((tb, d), BF16)],
        compiler_params=_params("parallel"),
        name="pool_sample",
    )(u, buf2d, wg, scale, wbr, sg)


def _dt_prep_kernel(dtr_ref, bias_ref, alog_ref, dt_ref, cum_ref, cumt_ref):
    dt = _softplus(dtr_ref[...] + bias_ref[...])
    da = dt * (-jnp.exp(alog_ref[...]))
    row = lax.broadcasted_iota(jnp.int32, da.shape, 0)
    cum = da
    k = 1
    while k < da.shape[0]:
        cum = cum + jnp.where(row >= k, pltpu.roll(cum, k, axis=0), 0.0)
        k *= 2
    dt_ref[...] = dt
    cum_ref[...] = cum
    cumt_ref[...] = cum.T


def _dt_prep(dtr, bias, alog):
    t, hp = dtr.shape
    c = SSD_CHUNK
    spec = pl.BlockSpec((c, hp), lambda i: (i, 0))
    vec = pl.BlockSpec((1, hp), lambda i: (0, 0))
    return pl.pallas_call(
        _dt_prep_kernel,
        grid=(t // c,),
        in_specs=[spec, vec, vec],
        out_specs=[spec, spec, spec],
        out_shape=[jax.ShapeDtypeStruct((t, hp), F32)] * 3,
        compiler_params=_params("parallel"),
        name="dt_prep",
    )(dtr, bias, alog)


def _conv_silu(ext_ref, w_ref, b_ref, rows, halo):
    kw = w_ref.shape[0]
    acc = b_ref[...]
    for k in range(kw):
        off = halo - (kw - 1) + k
        acc = acc + ext_ref[off:off + rows, :] * w_ref[k:k + 1, :]
    return _silu(acc)


def _ssd_prompt_kernel(xs_ref, bm_ref, cm_ref, zs_ref, dt_ref, cum_ref, cumt_ref,
                       wx_ref, wb_ref, wc_ref, bx_ref, bb_ref, bc_ref, dskip_ref, nw_ref, e_ref,
                       y_ref, st_ref, stt_ref, extx_ref, extb_ref, extc_ref, *, hpg, p):
    g = pl.program_id(1)
    c = pl.program_id(2)
    nc = pl.num_programs(2)
    rows = xs_ref.shape[1]
    halo = 8

    @pl.when(c == 0)
    def _():
        stt_ref[...] = jnp.zeros(stt_ref.shape, F32)
        for r in (extx_ref, extb_ref, extc_ref):
            r[0:halo, :] = jnp.zeros((halo, r.shape[1]), F32)

    @pl.when(c > 0)
    def _():
        for r in (extx_ref, extb_ref, extc_ref):
            r[0:halo, :] = r[rows:rows + halo, :]

    extx_ref[halo:halo + rows, :] = xs_ref[0]
    extb_ref[halo:halo + rows, :] = bm_ref[0]
    extc_ref[halo:halo + rows, :] = cm_ref[0]
    x = _conv_silu(extx_ref, wx_ref, bx_ref, rows, halo)
    bmat = _conv_silu(extb_ref, wb_ref, bb_ref, rows, halo).astype(BF16)
    cmat = _conv_silu(extc_ref, wc_ref, bc_ref, rows, halo).astype(BF16)

    shift = (LANES - g * hpg) % LANES
    dtg = pltpu.roll(dt_ref[0], shift, axis=1)
    cumg = pltpu.roll(cum_ref[0], shift, axis=1)
    cumtg = cumt_ref[0, pl.ds(pl.multiple_of(g * hpg, hpg), hpg), :]
    last = cumg[rows - 1:rows, :]
    stack = jnp.concatenate([dtg, jnp.exp(cumg), jnp.exp(last - cumg)], axis=0)
    hi, mid = _split2(stack)
    expanded = jnp.dot(jnp.concatenate([hi, mid], axis=1), e_ref[...], preferred_element_type=F32)
    dt_x, ecum_x, ws_x = expanded[0:rows], expanded[rows:2 * rows], expanded[2 * rows:3 * rows]

    xdt = x * dt_x
    xdtw_b = (xdt * ws_x).astype(BF16)
    cb = lax.dot_general(cmat, bmat, (((1,), (1,)), ((), ())), preferred_element_type=F32)
    st_old = stt_ref[...]
    y = jnp.dot(cmat, st_old.astype(BF16), preferred_element_type=F32) * ecum_x
    stt_ref[...] = st_old * ecum_x[rows - 1:rows, :] + lax.dot_general(
        bmat, xdtw_b, (((0,), (0,)), ((), ())), preferred_element_type=F32)

    li = lax.broadcasted_iota(jnp.int32, (rows, rows), 0)
    si = lax.broadcasted_iota(jnp.int32, (rows, rows), 1)
    causal = li >= si
    lane = lax.broadcasted_iota(jnp.int32, (rows, LANES), 1)
    heads_per_tile = LANES // p
    diag = []
    for q in range(hpg // heads_per_tile):
        xq = xdt[:, q * LANES:(q + 1) * LANES]
        acc = None
        for r in range(heads_per_tile):
            h = q * heads_per_tile + r
            seg = cumg[:, h:h + 1] - cumtg[h:h + 1, :]
            dec = jnp.exp(jnp.where(causal, seg, 0.0))
            m = jnp.where(causal, cb * dec, 0.0).astype(BF16)
            xr = jnp.where(lane // p == r, xq, 0.0).astype(BF16)
            t = jnp.dot(m, xr, preferred_element_type=F32)
            acc = t if acc is None else acc + t
        diag.append(acc)
    y = y + jnp.concatenate(diag, axis=1) + dskip_ref[...] * x
    y = y * zs_ref[0].astype(F32)
    y = y * lax.rsqrt(jnp.mean(y * y, axis=-1, keepdims=True) + RMS_EPS) * nw_ref[...]
    y_ref[0] = y.astype(y_ref.dtype)

    @pl.when(c == nc - 1)
    def _():
        st_ref[0] = stt_ref[...].T.reshape(st_ref.shape[1:])


def _ssd_prompt(xbc, zs, dt, cum, cumt, conv_w, conv_b, dskip_x, norm_w, e_mat, *, di, n, p, groups):
    b, l, cd = xbc.shape
    c = SSD_CHUNK
    heads = di // p
    hpg = heads // groups
    gw = hpg * p
    hp = dt.shape[-1]
    xoff, boff, coff = 0, di // n, di // n + groups

    def seg(width, off_blocks):
        return pl.BlockSpec((1, c, width), lambda i, g, k: (i, k, off_blocks + g))

    def wseg(rows, width, off_blocks):
        return pl.BlockSpec((rows, width), lambda i, g, k: (0, off_blocks + g))

    small = pl.BlockSpec((1, c, hp), lambda i, g, k: (i, k, 0))
    kw = conv_w.shape[0]
    return pl.pallas_call(
        functools.partial(_ssd_prompt_kernel, hpg=hpg, p=p),
        grid=(b, groups, l // c),
        in_specs=[
            seg(gw, xoff), seg(n, boff), seg(n, coff), seg(gw, 0), small, small, small,
            wseg(kw, gw, xoff), wseg(kw, n, boff), wseg(kw, n, coff),
            wseg(1, gw, xoff), wseg(1, n, boff), wseg(1, n, coff),
            wseg(1, gw, 0), wseg(1, gw, 0),
            pl.BlockSpec(e_mat.shape, lambda i, g, k: (0, 0)),
        ],
        out_specs=[
            pl.BlockSpec((1, c, gw), lambda i, g, k: (i, k, g)),
            pl.BlockSpec((1, hpg, p, n), lambda i, g, k: (i, g, 0, 0)),
        ],
        out_shape=[jax.ShapeDtypeStruct((b, l, di), BF16), jax.ShapeDtypeStruct((b, heads, p, n), F32)],
        scratch_shapes=[
            pltpu.VMEM((n, gw), F32),
            pltpu.VMEM((c + 8, gw), F32), pltpu.VMEM((c + 8, n), F32), pltpu.VMEM((c + 8, n), F32),
        ],
        compiler_params=_params("parallel", "parallel", "arbitrary"),
        name="ssd_prompt",
    )(xbc, xbc, xbc, zs, dt, cum, cumt, conv_w, conv_w, conv_w, conv_b, conv_b, conv_b, dskip_x, norm_w, e_mat)


def _ssd_sample_pre_kernel(xbc_ref, cbuf_ref, w_ref, b_ref, dtr_ref, bias_ref, alog_ref, e_ref,
                           xs_ref, xdt_ref, dae_ref, bm_ref, cm_ref, ncbuf_ref, *, di, gn):
    cd = xbc_ref.shape[1]
    kw = w_ref.shape[0]
    xbc = xbc_ref[...]
    acc = b_ref[...]
    for k in range(kw - 1):
        acc = acc + cbuf_ref[:, k * cd:(k + 1) * cd] * w_ref[k:k + 1, :]
    acc = _silu(acc + xbc * w_ref[kw - 1:kw, :])
    ncbuf_ref[:, 0:(kw - 2) * cd] = cbuf_ref[:, cd:(kw - 1) * cd]
    ncbuf_ref[:, (kw - 2) * cd:(kw - 1) * cd] = xbc
    xs = acc[:, 0:di]
    dt = _softplus(dtr_ref[...] + bias_ref[...])
    dae = jnp.exp(dt * (-jnp.exp(alog_ref[...])))
    stack = jnp.concatenate([dt, dae], axis=0)
    hi, mid = _split2(stack)
    expanded = jnp.dot(jnp.concatenate([hi, mid], axis=1), e_ref[...], preferred_element_type=F32)
    rows = xbc.shape[0]
    xs_ref[...] = xs
    xdt_ref[...] = xs * expanded[0:rows]
    dae_ref[...] = expanded[rows:2 * rows]
    bm_ref[...] = acc[:, di:di + gn]
    cm_ref[...] = acc[:, di + gn:di + 2 * gn]


def _ssd_sample_pre(xbc, cbuf2d, conv_w, conv_b, dtr, bias, alog, e_mat, *, di, gn):
    bs, cd = xbc.shape
    hp = dtr.shape[1]
    tb = _tile(bs, 32)
    row = lambda w: pl.BlockSpec((tb, w), lambda i: (i, 0))
    full = lambda a: pl.BlockSpec(a.shape, lambda i: (0, 0))
    return pl.pallas_call(
        functools.partial(_ssd_sample_pre_kernel, di=di, gn=gn),
        grid=(bs // tb,),
        in_specs=[row(cd), row(cbuf2d.shape[1]), full(conv_w), full(conv_b), row(hp), full(bias), full(alog), full(e_mat)],
        out_specs=[row(di), row(di), row(di), row(gn), row(gn), row(cbuf2d.shape[1])],
        out_shape=[jax.ShapeDtypeStruct((bs, di), F32)] * 3 + [jax.ShapeDtypeStruct((bs, gn), F32)] * 2
        + [jax.ShapeDtypeStruct(cbuf2d.shape, F32)],
        compiler_params=_params("parallel"),
        name="ssd_sample_pre",
    )(xbc, cbuf2d, conv_w, conv_b, dtr, bias, alog, e_mat)


def _ssd_sample_state_kernel(s_ref, xdt_ref, dae_ref, bm_ref, cm_ref, so_ref, y_ref, *, groups):
    tb, heads, p, n = s_ref.shape
    di = heads * p
    gw = di // groups
    grow = lax.broadcasted_iota(jnp.int32, (groups, di), 0)
    gcol = lax.broadcasted_iota(jnp.int32, (groups, di), 1) // gw
    own = grow == gcol
    r8 = lax.broadcasted_iota(jnp.int32, (8, di), 0)
    r8n = lax.broadcasted_iota(jnp.int32, (8, n), 0)
    for b in range(tb):
        s = s_ref[b].reshape(di, n)
        xrow = xdt_ref[b]
        drow = dae_ref[b]
        d0 = drow.astype(BF16).astype(F32)
        d1 = (drow - d0).astype(BF16).astype(F32)
        d2 = drow - d0 - d1
        xg = jnp.where(own, jnp.broadcast_to(xrow, (groups, di)), 0.0)
        dparts = jnp.where(r8 == 0, d0, jnp.where(r8 == 1, d1, jnp.where(r8 == 2, d2, 0.0)))
        lhs_t = jnp.concatenate([xg, dparts], axis=0).astype(BF16)
        ones = jnp.where(r8n < 3, 1.0, 0.0)
        rhs = jnp.concatenate([
            jnp.concatenate([bm_ref[b], jnp.zeros((groups, n), F32)], axis=1),
            jnp.concatenate([jnp.zeros((8, n), F32), ones], axis=1)], axis=0).astype(BF16)
        ud = lax.dot_general(lhs_t, rhs, (((0,), (0,)), ((), ())), preferred_element_type=F32)
        s_new = s * ud[:, n:2 * n] + ud[:, 0:n]
        so_ref[b] = s_new.reshape(heads, p, n)
        yg = lax.dot_general(cm_ref[b].astype(BF16), s_new.astype(BF16), (((1,), (1,)), ((), ())),
                             preferred_element_type=F32)
        y_ref[b] = jnp.sum(jnp.where(own, yg, 0.0), axis=0, keepdims=True)


def _ssd_sample_state(state, xdt, dae, bm, cm, *, groups):
    bs, heads, p, n = state.shape
    di = heads * p
    tb = _tile(bs, 2)
    xdt3, dae3 = xdt.reshape(bs, 1, di), dae.reshape(bs, 1, di)
    bm3, cm3 = bm.reshape(bs, groups, n), cm.reshape(bs, groups, n)
    sspec = pl.BlockSpec((tb, heads, p, n), lambda i: (i, 0, 0, 0))
    rspec = pl.BlockSpec((tb, 1, di), lambda i: (i, 0, 0))
    gspec = pl.BlockSpec((tb, groups, n), lambda i: (i, 0, 0))
    so, y = pl.pallas_call(
        functools.partial(_ssd_sample_state_kernel, groups=groups),
        grid=(bs // tb,),
        in_specs=[sspec, rspec, rspec, gspec, gspec],
        out_specs=[sspec, rspec],
        out_shape=[jax.ShapeDtypeStruct(state.shape, F32), jax.ShapeDtypeStruct((bs, 1, di), F32)],
        compiler_params=_params("parallel"),
        name="ssd_sample_state",
    )(state, xdt3, dae3, bm3, cm3)
    return so, y.reshape(bs, di)


def _ssd_sample_post_kernel(y_ref, xs_ref, zs_ref, dskip_ref, nw_ref, o_ref, *, groups):
    y = (y_ref[...] + dskip_ref[...] * xs_ref[...]) * zs_ref[...].astype(F32)
    gw = y.shape[1] // groups
    for g in range(groups):
        cols = slice(g * gw, (g + 1) * gw)
        yg = y[:, cols]
        yg = yg * lax.rsqrt(jnp.mean(yg * yg, axis=-1, keepdims=True) + RMS_EPS) * nw_ref[:, cols]
        o_ref[:, cols] = yg.astype(o_ref.dtype)


def _ssd_sample_post(y, xs, zs, dskip_x, norm_w, *, groups):
    return pl.pallas_call(
        functools.partial(_ssd_sample_post_kernel, groups=groups),
        out_shape=jax.ShapeDtypeStruct(y.shape, BF16),
        compiler_params=pltpu.CompilerParams(vmem_limit_bytes=VMEM_LIMIT_BYTES),
        name="ssd_sample_post",
    )(y, xs, zs, dskip_x, norm_w)


def _merge_kernel(y_ref, wb_ref, sg_ref, a_ref, o_ref):
    bb = jnp.dot(y_ref[...], wb_ref[...], preferred_element_type=F32)
    o_ref[...] = (a_ref[...] + sg_ref[...].astype(F32) * bb).astype(o_ref.dtype)


def _merge(y, wb, sg, a):
    m, k = y.shape
    n = wb.shape[1]
    tm, tn = _tile(m, 1024), _tile(n, 512)
    tile = pl.BlockSpec((tm, tn), lambda i, j: (i, j))
    return pl.pallas_call(
        _merge_kernel,
        grid=(m // tm, n // tn),
        in_specs=[pl.BlockSpec((tm, k), lambda i, j: (i, 0)), pl.BlockSpec((k, tn), lambda i, j: (0, j)), tile, tile],
        out_specs=tile,
        out_shape=jax.ShapeDtypeStruct((m, n), BF16),
        compiler_params=_params("parallel", "arbitrary"),
        name="merge",
    )(y, wb, sg, a)


def _proj_ln_kernel(m_ref, w_ref, x_ref, g_ref, b_ref, o_ref, ob_ref, *, alpha):
    f = jnp.dot(m_ref[...], w_ref[...], preferred_element_type=F32)
    h = _layer_norm(alpha * x_ref[...] + f, g_ref[...], b_ref[...])
    o_ref[...] = h
    ob_ref[...] = h.astype(BF16)


def _proj_ln(m_in, w, x, g, b, *, alpha):
    m, k = m_in.shape
    d = w.shape[1]
    tm = _tile(m, 512)
    row = lambda width: pl.BlockSpec((tm, width), lambda i: (i, 0))
    vec = pl.BlockSpec((1, d), lambda i: (0, 0))
    return pl.pallas_call(
        functools.partial(_proj_ln_kernel, alpha=alpha),
        grid=(m // tm,),
        in_specs=[row(k), pl.BlockSpec((k, d), lambda i: (0, 0)), row(d), vec, vec],
        out_specs=[row(d), row(d)],
        out_shape=[jax.ShapeDtypeStruct((m, d), F32), jax.ShapeDtypeStruct((m, d), BF16)],
        compiler_params=_params("parallel"),
        name="proj_ln",
    )(m_in, w, x, g, b)


def _attn_prompt_kernel(hb_ref, h_ref, k_ref, v_ref, wq_ref, wo_ref, g_ref, b_ref, o_ref, ob_ref, *,
                        heads, alpha):
    q = jnp.dot(hb_ref[0], wq_ref[...], preferred_element_type=F32)
    hd = q.shape[1] // heads
    scale = 1.0 / math.sqrt(hd)
    kb = k_ref[0].astype(BF16)
    vb = v_ref[0].astype(BF16)
    outs = []
    for h in range(heads):
        cols = slice(h * hd, (h + 1) * hd)
        s = lax.dot_general(q[:, cols].astype(BF16), kb[:, cols], (((1,), (1,)), ((), ())),
                            preferred_element_type=F32) * scale
        e = jnp.exp(s - jnp.max(s, axis=-1, keepdims=True))
        pr = e / jnp.sum(e, axis=-1, keepdims=True)
        outs.append(jnp.dot(pr.astype(BF16), vb[:, cols], preferred_element_type=F32))
    o = jnp.concatenate(outs, axis=1).astype(BF16)
    ca = jnp.dot(o, wo_ref[...], preferred_element_type=F32)
    hn = _layer_norm(alpha * h_ref[0] + ca, g_ref[...], b_ref[...])
    o_ref[0] = hn
    ob_ref[0] = hn.astype(BF16)


def _attn_prompt(hb, h, mk, mv, wq, wo, g, b, *, heads, alpha):
    bsz, l, d = h.shape
    nm, md = mk.shape[1], mk.shape[2]
    tq = _tile(l, 512)
    row = pl.BlockSpec((1, tq, d), lambda i, j: (i, j, 0))
    mem = pl.BlockSpec((1, nm, md), lambda i, j: (i, 0, 0))
    vec = pl.BlockSpec((1, d), lambda i, j: (0, 0))
    return pl.pallas_call(
        functools.partial(_attn_prompt_kernel, heads=heads, alpha=alpha),
        grid=(bsz, l // tq),
        in_specs=[row, row, mem, mem, pl.BlockSpec((d, md), lambda i, j: (0, 0)),
                  pl.BlockSpec((md, d), lambda i, j: (0, 0)), vec, vec],
        out_specs=[row, row],
        out_shape=[jax.ShapeDtypeStruct((bsz, l, d), F32), jax.ShapeDtypeStruct((bsz, l, d), BF16)],
        compiler_params=_params("parallel", "parallel"),
        name="attn_prompt",
    )(hb, h, mk, mv, wq, wo, g, b)


def _attn_sample_kernel(hb_ref, h_ref, k_ref, v_ref, wq_ref, wo_ref, g_ref, b_ref, o_ref, ob_ref, att_ref, *,
                        heads, alpha):
    q = jnp.dot(hb_ref[...], wq_ref[...], preferred_element_type=F32)
    tb, md = q.shape
    hd = md // heads
    scale = 1.0 / math.sqrt(hd)
    for bi in range(tb):
        prod = k_ref[bi] * q[bi:bi + 1, :]
        vb = v_ref[bi]
        parts = []
        for h in range(heads):
            cols = slice(h * hd, (h + 1) * hd)
            s = jnp.sum(prod[:, cols], axis=-1, keepdims=True) * scale
            e = jnp.exp(s - jnp.max(s, axis=0, keepdims=True))
            pr = e / jnp.sum(e, axis=0, keepdims=True)
            parts.append(jnp.sum(pr * vb[:, cols], axis=0, keepdims=True))
        att_ref[bi:bi + 1, :] = jnp.concatenate(parts, axis=1)
    ca = jnp.dot(att_ref[...].astype(BF16), wo_ref[...], preferred_element_type=F32)
    hn = _layer_norm(alpha * h_ref[...] + ca, g_ref[...], b_ref[...])
    o_ref[...] = hn
    ob_ref[...] = hn.astype(BF16)


def _attn_sample(hb, h, ck, cv, wq, wo, g, b, *, heads, alpha):
    bs, d = h.shape
    nm, md = ck.shape[1], ck.shape[2]
    tb = _tile(bs, 8)
    row = pl.BlockSpec((tb, d), lambda i: (i, 0))
    mem = pl.BlockSpec((tb, nm, md), lambda i: (i, 0, 0))
    vec = pl.BlockSpec((1, d), lambda i: (0, 0))
    return pl.pallas_call(
        functools.partial(_attn_sample_kernel, heads=heads, alpha=alpha),
        grid=(bs // tb,),
        in_specs=[row, row, mem, mem, pl.BlockSpec((d, md), lambda i: (0, 0)),
                  pl.BlockSpec((md, d), lambda i: (0, 0)), vec, vec],
        out_specs=[row, row],
        out_shape=[jax.ShapeDtypeStruct((bs, d), F32), jax.ShapeDtypeStruct((bs, d), BF16)],
        scratch_shapes=[pltpu.VMEM((tb, md), F32)],
        compiler_params=_params("parallel"),
        name="attn_sample",
    )(hb, h, ck, cv, wq, wo, g, b)


def _ffn_kernel(hb_ref, h_ref, wg_ref, wu_ref, wd_ref, g_ref, b_ref, o_ref, ob_ref, acc_ref, *, alpha):
    f = pl.program_id(1)
    x = hb_ref[...]
    gate = jnp.dot(x, wg_ref[...], preferred_element_type=F32)
    up = jnp.dot(x, wu_ref[...], preferred_element_type=F32)
    part = jnp.dot((_silu(gate) * up).astype(BF16), wd_ref[...], preferred_element_type=F32)

    @pl.when(f == 0)
    def _():
        acc_ref[...] = part

    @pl.when(f > 0)
    def _():
        acc_ref[...] += part

    @pl.when(f == pl.num_programs(1) - 1)
    def _():
        hn = _layer_norm(alpha * h_ref[...] + acc_ref[...], g_ref[...], b_ref[...])
        o_ref[...] = hn
        ob_ref[...] = hn.astype(BF16)


def _ffn(hb, h, wg, wu, wd, g, b, *, alpha):
    m, d = h.shape
    ff = wg.shape[1]
    tm, tf = _tile(m, 512), 512
    assert ff % tf == 0
    row = pl.BlockSpec((tm, d), lambda i, f: (i, 0))
    vec = pl.BlockSpec((1, d), lambda i, f: (0, 0))
    return pl.pallas_call(
        functools.partial(_ffn_kernel, alpha=alpha),
        grid=(m // tm, ff // tf),
        in_specs=[row, row, pl.BlockSpec((d, tf), lambda i, f: (0, f)), pl.BlockSpec((d, tf), lambda i, f: (0, f)),
                  pl.BlockSpec((tf, d), lambda i, f: (f, 0)), vec, vec],
        out_specs=[row, row],
        out_shape=[jax.ShapeDtypeStruct((m, d), F32), jax.ShapeDtypeStruct((m, d), BF16)],
        scratch_shapes=[pltpu.VMEM((tm, d), F32)],
        compiler_params=_params("parallel", "arbitrary"),
        name="ffn",
    )(hb, h, wg, wu, wd, g, b)


def _expansion_matrix(n_in, reps, n_pad):
    r = jnp.arange(n_pad)[:, None]
    c = jnp.arange(n_in * reps)[None, :] // reps
    e = ((r == c) & (r < n_in)).astype(BF16)
    return jnp.concatenate([e, e], axis=0)


def kernel(x_prompt, x_sample, mem_prompt, state_ssd, state_conv, state_pool, cache_mem_k, cache_mem_v, w_in, w_pool, pool_scale, w_pool_br, conv_w, conv_b, dt_bias, a_log, d_skip, ssd_norm_w, w_ssd_br, w_out, ln_mix_g, ln_mix_b, w_mq, w_mk, w_mv, w_mo, ln_mem_g, ln_mem_b, w_ffn_gate, w_ffn_up, w_ffn_down, ln_ffn_g, ln_ffn_b):
    bp, seq, d = x_prompt.shape
    bs = x_sample.shape[0]
    depth = w_in.shape[0]
    _, _, heads, p, n = state_ssd.shape
    di = heads * p
    cd = conv_w.shape[2]
    groups = (cd - di) // (2 * n)
    gn = groups * n
    hpg = heads // groups
    n_mem, mem_heads, mem_hd = cache_mem_k.shape[2:]
    md = mem_heads * mem_hd
    nbuf = state_pool.shape[2]
    kw1 = state_conv.shape[2]
    alpha = (2 * depth) ** 0.25
    assert seq % SSD_CHUNK == 0 and heads <= LANES and LANES % p == 0 and hpg % (LANES // p) == 0

    s1, s2, s3, s4 = d, d + di, d + di + cd, d + di + cd + heads
    s5 = s4 + d

    e_group = _expansion_matrix(hpg, p, LANES)
    e_all = _expansion_matrix(heads, p, LANES)
    pad_h = lambda v: jnp.pad(v.astype(F32), (0, LANES - heads)).reshape(1, LANES)

    hp, hs = x_prompt.reshape(bp * seq, d), x_sample.reshape(bs, d)
    hp_b, hs_b = hp.astype(BF16), hs.astype(BF16)
    mem_b = mem_prompt.reshape(bp * n_mem, d).astype(BF16)

    outs = {k: [] for k in ("ssd_p", "conv_p", "pool_p", "mk", "mv", "ssd_s", "conv_s", "pool_s")}
    for l in range(depth):
        wl = w_in[l]
        w_u = wl[:, :s1].astype(BF16)
        w_z = wl[:, s1:s2].astype(BF16)
        w_xbc = wl[:, s2:s3].astype(BF16)
        w_dt = jnp.pad(wl[:, s3:s4], ((0, 0), (0, LANES - heads))).astype(BF16)
        w_gp = wl[:, s4:s5].astype(BF16)
        w_gs = wl[:, s5:].astype(BF16)
        wpool_b = w_pool[l].astype(BF16)
        wpbr_b = w_pool_br[l].astype(BF16)
        wsbr_b = w_ssd_br[l].astype(BF16)
        wout_b = w_out[l].astype(BF16)
        wq_b, wk_b, wv_b, wo_b = (w[l].astype(BF16) for w in (w_mq, w_mk, w_mv, w_mo))
        wg_b, wu_b, wd_b = (w[l].astype(BF16) for w in (w_ffn_gate, w_ffn_up, w_ffn_down))
        scale = pool_scale[l].reshape(1, d)
        cw, cbias = conv_w[l], conv_b[l].reshape(1, cd)
        bias_p, alog_p = pad_h(dt_bias[l]), pad_h(a_log[l])
        dskip_x = jnp.repeat(d_skip[l].astype(F32), p).reshape(1, di)
        nw = ssd_norm_w[l].reshape(1, di)
        vec = lambda v: v[l].reshape(1, d)

        def in_proj(hb):
            u = _mm(hb, w_u, name="in_u")
            zs = _mm(hb, w_z, act="silu", out_dtype=BF16, name="in_z")
            xbc = _mm(hb, w_xbc, name="in_xbc")
            dtr = _mm(hb, w_dt, name="in_dt")
            sgp = _mm(hb, w_gp, act="sigmoid", out_dtype=BF16, name="in_gp")
            sgs = _mm(hb, w_gs, act="sigmoid", out_dtype=BF16, name="in_gs")
            return u, zs, xbc, dtr, sgp, sgs

        def tail(h, y, sgs, a_gated, attn):
            merged = _merge(y, wsbr_b, sgs, a_gated)
            h1, h1b = _proj_ln(merged, wout_b, h, vec(ln_mix_g), vec(ln_mix_b), alpha=alpha)
            h2, h2b = attn(h1, h1b)
            return _ffn(h2b, h2, wg_b, wu_b, wd_b, vec(ln_ffn_g), vec(ln_ffn_b), alpha=alpha)

        u, zs, xbc, dtr, sgp, sgs = in_proj(hp_b)
        u3, xbc3 = u.reshape(bp, seq, d), xbc.reshape(bp, seq, cd)
        a_gated = _pool_prompt(u3, wpool_b, scale, wpbr_b, sgp.reshape(bp, seq, d)).reshape(bp * seq, d)
        dt, cum, cumt = _dt_prep(dtr, bias_p, alog_p)
        r3 = lambda a: a.reshape(bp, seq, a.shape[-1])
        y, st = _ssd_prompt(xbc3, r3(zs), r3(dt), r3(cum), r3(cumt), cw, cbias, dskip_x, nw, e_group,
                            di=di, n=n, p=p, groups=groups)
        mk_p = _mm(mem_b, wk_b, name="mem_k").reshape(bp, n_mem, md)
        mv_p = _mm(mem_b, wv_b, name="mem_v").reshape(bp, n_mem, md)

        def attn_p(h1, h1b):
            o, ob = _attn_prompt(h1b.reshape(bp, seq, d), h1.reshape(bp, seq, d), mk_p, mv_p, wq_b, wo_b,
                                 vec(ln_mem_g), vec(ln_mem_b), heads=mem_heads, alpha=alpha)
            return o.reshape(bp * seq, d), ob.reshape(bp * seq, d)

        hp, hp_b = tail(hp, y.reshape(bp * seq, di), sgs, a_gated, attn_p)
        outs["ssd_p"].append(st)
        outs["conv_p"].append(xbc3[:, seq - kw1:, :])
        outs["pool_p"].append(u3[:, seq - nbuf:, :])
        outs["mk"].append(mk_p.reshape(bp, n_mem, mem_heads, mem_hd))
        outs["mv"].append(mv_p.reshape(bp, n_mem, mem_heads, mem_hd))

        u, zs, xbc, dtr, sgp, sgs = in_proj(hs_b)
        a_gated, npool = _pool_sample(u, state_pool[l].reshape(bs, nbuf * d), wpool_b, scale, wpbr_b, sgp)
        xs, xdt, dae, bm, cm, nconv = _ssd_sample_pre(xbc, state_conv[l].reshape(bs, kw1 * cd), cw, cbias, dtr,
                                                      bias_p, alog_p, e_all, di=di, gn=gn)
        nst, ys = _ssd_sample_state(state_ssd[l], xdt, dae, bm, cm, groups=groups)
        y = _ssd_sample_post(ys, xs, zs, dskip_x, nw, groups=groups)
        ck, cv = cache_mem_k[l].reshape(bs, n_mem, md), cache_mem_v[l].reshape(bs, n_mem, md)

        def attn_s(h1, h1b):
            return _attn_sample(h1b, h1, ck, cv, wq_b, wo_b, vec(ln_mem_g), vec(ln_mem_b),
                                heads=mem_heads, alpha=alpha)

        hs, hs_b = tail(hs, y, sgs, a_gated, attn_s)
        outs["ssd_s"].append(nst)
        outs["conv_s"].append(nconv.reshape(bs, kw1, cd))
        outs["pool_s"].append(npool.reshape(bs, nbuf, d))

    stack = lambda k: jnp.stack(outs[k])
    return (hp.reshape(bp, seq, d), hs.reshape(bs, 1, d), stack("ssd_p"), stack("conv_p"), stack("pool_p"),
            stack("mk"), stack("mv"), stack("ssd_s"), stack("conv_s"), stack("pool_s"))
```

```python
import functools
import math

import jax
import jax.numpy as jnp
from jax import lax
from jax.experimental import pallas as pl
from jax.experimental.pallas import tpu as pltpu

F32 = jnp.float32
BF16 = jnp.bfloat16

POOL_WINDOWS = (2, 4, 8, 16)
PAST_LEN = 16384
SSD_CHUNK = 128
LN_EPS = 1e-5
RMS_EPS = 1e-5
LANES = 128
VMEM_LIMIT_BYTES = 56 * 1024 * 1024


def _params(*sem):
    return pltpu.CompilerParams(dimension_semantics=sem, vmem_limit_bytes=VMEM_LIMIT_BYTES)


def _silu(x):
    return x * jax.nn.sigmoid(x)


def _softplus(x):
    return jnp.maximum(x, 0.0) + jnp.log(1.0 + jnp.exp(-jnp.abs(x)))


def _layer_norm(x, g, b):
    mu = jnp.mean(x, axis=-1, keepdims=True)
    xc = x - mu
    var = jnp.mean(xc * xc, axis=-1, keepdims=True)
    return xc * lax.rsqrt(var + LN_EPS) * g + b


def _split2(x):
    hi = x.astype(BF16)
    mid = (x - hi.astype(F32)).astype(BF16)
    return hi, mid


def _tile(n, pref):
    t = min(n, pref)
    while n % t:
        t //= 2
    return t


def _lspec(layer, block, index_map=None):
    nd = len(block)
    if index_map is None:
        index_map = lambda *g: (0,) * nd
    return pl.BlockSpec((None,) + tuple(block), lambda *g: (layer,) + tuple(index_map(*g)))


def _mm_kernel(x_ref, w_ref, o_ref, *, act):
    acc = jnp.dot(x_ref[...], w_ref[...], preferred_element_type=F32)
    if act == "silu":
        acc = _silu(acc)
    elif act == "sigmoid":
        acc = jax.nn.sigmoid(acc)
    o_ref[...] = acc.astype(o_ref.dtype)


def _mm(x, w, layer, *, col0=0, n=None, act=None, out_dtype=F32, name):
    m, k = x.shape
    n = w.shape[2] if n is None else n
    tm, tn = _tile(m, 1024), _tile(n, 1024)
    assert col0 % tn == 0
    off = col0 // tn
    return pl.pallas_call(
        functools.partial(_mm_kernel, act=act),
        grid=(m // tm, n // tn),
        in_specs=[pl.BlockSpec((tm, k), lambda i, j: (i, 0)), _lspec(layer, (k, tn), lambda i, j: (0, off + j))],
        out_specs=pl.BlockSpec((tm, tn), lambda i, j: (i, j)),
        out_shape=jax.ShapeDtypeStruct((m, n), out_dtype),
        compiler_params=_params("parallel", "arbitrary"),
        name=name,
    )(x, w)


def _pool_project(pooled_ref, wg_ref, scale_ref, wbr_ref, sg, n_groups, gc):
    for g in range(n_groups):
        cols = slice(g * gc, (g + 1) * gc)
        pg = jnp.dot(pooled_ref[:, cols], wg_ref[g], preferred_element_type=F32) * scale_ref[:, cols]
        pooled_ref[:, cols] = pg.astype(BF16)
    a = jnp.dot(pooled_ref[...], wbr_ref[...], preferred_element_type=F32)
    return a * sg.astype(F32)


def _pool_prompt_kernel(u_ref, wg_ref, scale_ref, wbr_ref, sg_ref, o_ref, ext_ref, pooled_ref, *, tl, gc):
    j = pl.program_id(1)
    halo = 16

    @pl.when(j == 0)
    def _():
        ext_ref[0:halo, :] = jnp.zeros((halo, ext_ref.shape[1]), F32)

    @pl.when(j > 0)
    def _():
        ext_ref[0:halo, :] = ext_ref[tl:tl + halo, :]

    ext_ref[halo:halo + tl, :] = u_ref[0]
    pos = j * tl + lax.broadcasted_iota(jnp.int32, (tl, 1), 0)
    for g, w in enumerate(POOL_WINDOWS):
        cols = slice(g * gc, (g + 1) * gc)
        ug = ext_ref[halo:halo + tl, cols]
        s = ug
        for k in range(1, w):
            s = s + ext_ref[halo - k:halo - k + tl, cols]
        cnt = jnp.minimum(w, pos + 1).astype(F32)
        pooled_ref[:, cols] = (s / cnt - ug).astype(BF16)
    o_ref[0] = _pool_project(pooled_ref, wg_ref, scale_ref, wbr_ref, sg_ref[0], len(POOL_WINDOWS), gc)


def _pool_prompt(u, sg, layer, wg, scale, wbr):
    b, l, d = u.shape
    _, ng, gc, _ = wg.shape
    tl = _tile(l, 256)
    tok = pl.BlockSpec((1, tl, d), lambda i, j: (i, j, 0))
    return pl.pallas_call(
        functools.partial(_pool_prompt_kernel, tl=tl, gc=gc),
        grid=(b, l // tl),
        in_specs=[tok, _lspec(layer, (ng, gc, gc)), _lspec(layer, (1, d)), _lspec(layer, (d, d)), tok],
        out_specs=tok,
        out_shape=jax.ShapeDtypeStruct((b, l, d), F32),
        scratch_shapes=[pltpu.VMEM((tl + 16, d), F32), pltpu.VMEM((tl, d), BF16)],
        compiler_params=_params("parallel", "arbitrary"),
        name="pool_prompt",
    )(u, wg, scale, wbr, sg)


def _layer_state_call(kern, *, layer, state, batch_axis, stacked_so_far, grid, tb, other_operands, other_in_specs,
                      other_out_specs, other_out_shapes, scratch_shapes=(), name):
    blk = tuple(tb if a == batch_axis else s for a, s in enumerate(state.shape))[1:]
    spec = _lspec(layer, blk, lambda i: tuple(i if a == batch_axis else 0 for a in range(1, state.ndim)))
    if stacked_so_far is None:
        stacked_so_far = jnp.zeros(state.shape, state.dtype)
    operands = list(other_operands) + [state, stacked_so_far]
    in_specs = list(other_in_specs) + [spec, pl.BlockSpec(memory_space=pl.ANY)]
    return pl.pallas_call(
        kern,
        grid=grid,
        in_specs=in_specs,
        out_specs=list(other_out_specs) + [spec],
        out_shape=list(other_out_shapes) + [jax.ShapeDtypeStruct(state.shape, state.dtype)],
        input_output_aliases={len(operands) - 1: len(other_out_shapes)},
        scratch_shapes=list(scratch_shapes),
        compiler_params=_params("parallel"),
        name=name,
    )(*operands)


def _pool_sample_kernel(u_ref, wg_ref, scale_ref, wbr_ref, sg_ref, buf_ref, stack_hbm_ref,
                        o_ref, nbuf_ref, pooled_ref, *, gc):
    del stack_hbm_ref
    nbuf = buf_ref.shape[0]
    u = u_ref[...]
    for r in range(nbuf - 1):
        nbuf_ref[r] = buf_ref[r + 1]
    nbuf_ref[nbuf - 1] = u
    for g, w in enumerate(POOL_WINDOWS):
        cols = slice(g * gc, (g + 1) * gc)
        ug = u[:, cols]
        s = ug
        for k in range(1, w):
            s = s + buf_ref[nbuf - k, :, cols]
        cnt = float(min(w, PAST_LEN + 1))
        pooled_ref[:, cols] = (s / cnt - ug).astype(BF16)
    o_ref[...] = _pool_project(pooled_ref, wg_ref, scale_ref, wbr_ref, sg_ref[...], len(POOL_WINDOWS), gc)


def _pool_sample(u, sg, pool_state_t, layer, stacked_so_far, wg, scale, wbr):
    bs, d = u.shape
    _, ng, gc, _ = wg.shape
    tb = _tile(bs, 32)
    row = pl.BlockSpec((tb, d), lambda i: (i, 0))
    return _layer_state_call(
        functools.partial(_pool_sample_kernel, gc=gc),
        layer=layer, state=pool_state_t, batch_axis=2, stacked_so_far=stacked_so_far, grid=(bs // tb,), tb=tb,
        other_operands=[u, wg, scale, wbr, sg],
        other_in_specs=[row, _lspec(layer, (ng, gc, gc)), _lspec(layer, (1, d)), _lspec(layer, (d, d)), row],
        other_out_specs=[row], other_out_shapes=[jax.ShapeDtypeStruct((bs, d), F32)],
        scratch_shapes=[pltpu.VMEM((tb, d), BF16)], name="pool_sample")


def _dt_prep_kernel(dtr_ref, bias_ref, alog_ref, dt_ref, cum_ref, cumt_ref):
    dt = _softplus(dtr_ref[...] + bias_ref[...])
    da = dt * (-jnp.exp(alog_ref[...]))
    row = lax.broadcasted_iota(jnp.int32, da.shape, 0)
    cum = da
    k = 1
    while k < da.shape[0]:
        cum = cum + jnp.where(row >= k, pltpu.roll(cum, k, axis=0), 0.0)
        k *= 2
    dt_ref[...] = dt
    cum_ref[...] = cum
    cumt_ref[...] = cum.T


def _dt_prep(dtr, layer, bias, alog):
    t, hp = dtr.shape
    c = SSD_CHUNK
    spec = pl.BlockSpec((c, hp), lambda i: (i, 0))
    return pl.pallas_call(
        _dt_prep_kernel,
        grid=(t // c,),
        in_specs=[spec, _lspec(layer, (1, hp)), _lspec(layer, (1, hp))],
        out_specs=[spec, spec, spec],
        out_shape=[jax.ShapeDtypeStruct((t, hp), F32)] * 3,
        compiler_params=_params("parallel"),
        name="dt_prep",
    )(dtr, bias, alog)


def _conv_silu(x, carry_ref, w_ref, b_ref):
    rows, wd = x.shape
    kw = w_ref.shape[0]
    nt = rows // 8
    last_row = lax.broadcasted_iota(jnp.int32, (nt, 8, wd), 1) == 7
    t = x * w_ref[0:1, :]
    for k in range(1, kw):
        t3 = t.reshape(nt, 8, wd)
        before = jnp.concatenate([carry_ref[k - 1][None], t3[:nt - 1]], axis=0)
        carry_ref[k - 1] = t[rows - 8:rows, :]
        shifted = pltpu.roll(jnp.where(last_row, before, t3), 1, axis=1).reshape(rows, wd)
        t = x * w_ref[k:k + 1, :] + shifted
    return _silu(t + b_ref[...])


def _ssd_prompt_kernel(xs_ref, bm_ref, cm_ref, zs_ref, dt_ref, cum_ref, cumt_ref,
                       wx_ref, wb_ref, wc_ref, bx_ref, bb_ref, bc_ref, dskip_ref, nw_ref, e_ref,
                       y_ref, st_ref, stt_ref, hx_ref, hb_ref, hc_ref, *, hpg, p, gps):
    c = pl.program_id(2)
    nc = pl.num_programs(2)
    rows = xs_ref.shape[1]
    n = bm_ref.shape[2] // gps
    gw = hpg * p

    @pl.when(c == 0)
    def _():
        for r in (stt_ref, hx_ref, hb_ref, hc_ref):
            r[...] = jnp.zeros(r.shape, F32)

    x_all = _conv_silu(xs_ref[0], hx_ref, wx_ref, bx_ref)
    b_all = _conv_silu(bm_ref[0], hb_ref, wb_ref, bb_ref).astype(BF16)
    c_all = _conv_silu(cm_ref[0], hc_ref, wc_ref, bc_ref).astype(BF16)
    causal = lax.broadcasted_iota(jnp.int32, (rows, rows), 0) >= lax.broadcasted_iota(jnp.int32, (rows, rows), 1)
    lane = lax.broadcasted_iota(jnp.int32, (rows, LANES), 1)
    heads_per_tile = LANES // p
    lane_masks = [jnp.where(lane // p == r, 1.0, 0.0).astype(BF16) for r in range(heads_per_tile)]

    for gi in range(gps):
        g = pl.program_id(1) * gps + gi
        gcols = slice(gi * gw, (gi + 1) * gw)
        x = x_all[:, gcols]
        bmat = b_all[:, gi * n:(gi + 1) * n]
        cmat = c_all[:, gi * n:(gi + 1) * n]
        shift = (LANES - g * hpg) % LANES
        dtg = pltpu.roll(dt_ref[0], shift, axis=1)
        cumg = pltpu.roll(cum_ref[0], shift, axis=1)
        cumtg = cumt_ref[0, pl.ds(pl.multiple_of(g * hpg, hpg), hpg), :]
        last = cumg[rows - 1:rows, :]
        stack = jnp.concatenate([dtg, jnp.exp(cumg), jnp.exp(last - cumg)], axis=0)
        hi, mid = _split2(stack)
        expanded = jnp.dot(jnp.concatenate([hi, mid], axis=1), e_ref[...], preferred_element_type=F32)
        dt_x, ecum_x, ws_x = expanded[0:rows], expanded[rows:2 * rows], expanded[2 * rows:3 * rows]

        xdt = x * dt_x
        xdt_b = xdt.astype(BF16)
        xdtw_b = (xdt * ws_x).astype(BF16)
        cb = lax.dot_general(cmat, bmat, (((1,), (1,)), ((), ())), preferred_element_type=F32)
        st_old = stt_ref[:, gcols]
        y = jnp.dot(cmat, st_old.astype(BF16), preferred_element_type=F32) * ecum_x
        stt_ref[:, gcols] = st_old * ecum_x[rows - 1:rows, :] + lax.dot_general(
            bmat, xdtw_b, (((0,), (0,)), ((), ())), preferred_element_type=F32)

        cum2 = cumg * math.log2(math.e)
        cumt2 = cumtg * math.log2(math.e)
        diag = []
        for q in range(hpg // heads_per_tile):
            xq = xdt_b[:, q * LANES:(q + 1) * LANES]
            ms, xr = [], []
            for r in range(heads_per_tile):
                h = q * heads_per_tile + r
                dec = jnp.exp2(cum2[:, h:h + 1] - cumt2[h:h + 1, :])
                ms.append(jnp.where(causal, cb * dec, 0.0).astype(BF16))
                xr.append(xq * lane_masks[r])
            diag.append(jnp.dot(jnp.concatenate(ms, axis=1), jnp.concatenate(xr, axis=0),
                                preferred_element_type=F32))
        y = y + jnp.concatenate(diag, axis=1) + dskip_ref[:, gcols] * x
        y = y * zs_ref[0, :, gcols].astype(F32)
        y = y * lax.rsqrt(jnp.mean(y * y, axis=-1, keepdims=True) + RMS_EPS) * nw_ref[:, gcols]
        y_ref[0, :, gcols] = y.astype(y_ref.dtype)

    @pl.when(c == nc - 1)
    def _():
        st_ref[0] = stt_ref[...].T.reshape(st_ref.shape[1:])


def _ssd_prompt(xbc, zs, dt, cum, cumt, layer, conv_w, conv_b, dskip_x, norm_w, e_mat, *, di, n, p, groups):
    b, l, cd = xbc.shape
    c = SSD_CHUNK
    heads = di // p
    hpg = heads // groups
    gps = 4 if groups % 4 == 0 else 1
    gw = hpg * p * gps
    nn = n * gps
    hp = dt.shape[-1]
    xoff, boff, coff = 0, di // nn, (di + groups * n) // nn

    def seg(width, off_blocks):
        return pl.BlockSpec((1, c, width), lambda i, g, k: (i, k, off_blocks + g))

    def wseg(rows, width, off_blocks):
        return _lspec(layer, (rows, width), lambda i, g, k: (0, off_blocks + g))

    small = pl.BlockSpec((1, c, hp), lambda i, g, k: (i, k, 0))
    kw = conv_w.shape[1]
    return pl.pallas_call(
        functools.partial(_ssd_prompt_kernel, hpg=hpg, p=p, gps=gps),
        grid=(b, groups // gps, l // c),
        in_specs=[
            seg(gw, xoff), seg(nn, boff), seg(nn, coff), seg(gw, 0), small, small, small,
            wseg(kw, gw, xoff), wseg(kw, nn, boff), wseg(kw, nn, coff),
            wseg(1, gw, xoff), wseg(1, nn, boff), wseg(1, nn, coff),
            wseg(1, gw, 0), wseg(1, gw, 0),
            pl.BlockSpec(e_mat.shape, lambda i, g, k: (0, 0)),
        ],
        out_specs=[
            pl.BlockSpec((1, c, gw), lambda i, g, k: (i, k, g)),
            pl.BlockSpec((1, hpg * gps, p, n), lambda i, g, k: (i, g, 0, 0)),
        ],
        out_shape=[jax.ShapeDtypeStruct((b, l, di), BF16), jax.ShapeDtypeStruct((b, heads, p, n), F32)],
        scratch_shapes=[
            pltpu.VMEM((n, gw), F32),
            pltpu.VMEM((kw - 1, 8, gw), F32), pltpu.VMEM((kw - 1, 8, nn), F32), pltpu.VMEM((kw - 1, 8, nn), F32),
        ],
        compiler_params=_params("parallel", "parallel", "arbitrary"),
        name="ssd_prompt",
    )(xbc, xbc, xbc, zs, dt, cum, cumt, conv_w, conv_w, conv_w, conv_b, conv_b, conv_b, dskip_x, norm_w, e_mat)


def _ssd_sample_pre_kernel(xbc_ref, w_ref, b_ref, dtr_ref, bias_ref, alog_ref, e_ref, cbuf_ref, stack_hbm_ref,
                           xs_ref, xdt_ref, dae_ref, bm_ref, cm_ref, ncbuf_ref, *, di, gn):
    del stack_hbm_ref
    kw = w_ref.shape[0]
    xbc = xbc_ref[...]
    acc = b_ref[...]
    for k in range(kw - 1):
        acc = acc + cbuf_ref[k] * w_ref[k:k + 1, :]
    acc = _silu(acc + xbc * w_ref[kw - 1:kw, :])
    for k in range(kw - 2):
        ncbuf_ref[k] = cbuf_ref[k + 1]
    ncbuf_ref[kw - 2] = xbc
    xs = acc[:, 0:di]
    dt = _softplus(dtr_ref[...] + bias_ref[...])
    dae = jnp.exp(dt * (-jnp.exp(alog_ref[...])))
    stack = jnp.concatenate([dt, dae], axis=0)
    hi, mid = _split2(stack)
    expanded = jnp.dot(jnp.concatenate([hi, mid], axis=1), e_ref[...], preferred_element_type=F32)
    rows = xbc.shape[0]
    xs_ref[...] = xs
    xdt_ref[...] = xs * expanded[0:rows]
    dae_ref[...] = expanded[rows:2 * rows]
    bm_ref[...] = acc[:, di:di + gn]
    cm_ref[...] = acc[:, di + gn:di + 2 * gn]


def _ssd_sample_pre(xbc, dtr, conv_state_t, layer, stacked_so_far, conv_w, conv_b, bias, alog, e_mat, *, di, gn):
    bs, cd = xbc.shape
    hp = dtr.shape[1]
    kw = conv_w.shape[1]
    tb = _tile(bs, 32)
    row = lambda w: pl.BlockSpec((tb, w), lambda i: (i, 0))
    return _layer_state_call(
        functools.partial(_ssd_sample_pre_kernel, di=di, gn=gn),
        layer=layer, state=conv_state_t, batch_axis=2, stacked_so_far=stacked_so_far, grid=(bs // tb,), tb=tb,
        other_operands=[xbc, conv_w, conv_b, dtr, bias, alog, e_mat],
        other_in_specs=[row(cd), _lspec(layer, (kw, cd)), _lspec(layer, (1, cd)), row(hp), _lspec(layer, (1, hp)),
                        _lspec(layer, (1, hp)), pl.BlockSpec(e_mat.shape, lambda i: (0, 0))],
        other_out_specs=[row(di), row(di), row(di), row(gn), row(gn)],
        other_out_shapes=[jax.ShapeDtypeStruct((bs, di), F32)] * 3 + [jax.ShapeDtypeStruct((bs, gn), F32)] * 2,
        name="ssd_sample_pre")


def _ssd_sample_state_kernel(xdt_ref, dae_ref, bm_ref, cm_ref, s_ref, stack_hbm_ref, y_ref, so_ref, *, groups):
    del stack_hbm_ref
    tb, heads, p, n = s_ref.shape
    di = heads * p
    gw = di // groups
    grow = lax.broadcasted_iota(jnp.int32, (groups, di), 0)
    gcol = lax.broadcasted_iota(jnp.int32, (groups, di), 1) // gw
    own = grow == gcol
    r8 = lax.broadcasted_iota(jnp.int32, (8, di), 0)
    r8n = lax.broadcasted_iota(jnp.int32, (8, n), 0)
    for b in range(tb):
        s = s_ref[b].reshape(di, n)
        xrow = xdt_ref[b]
        drow = dae_ref[b]
        d0 = drow.astype(BF16).astype(F32)
        d1 = (drow - d0).astype(BF16).astype(F32)
        d2 = drow - d0 - d1
        xg = jnp.where(own, jnp.broadcast_to(xrow, (groups, di)), 0.0)
        dparts = jnp.where(r8 == 0, d0, jnp.where(r8 == 1, d1, jnp.where(r8 == 2, d2, 0.0)))
        lhs_t = jnp.concatenate([xg, dparts], axis=0).astype(BF16)
        ones = jnp.where(r8n < 3, 1.0, 0.0)
        rhs = jnp.concatenate([
            jnp.concatenate([bm_ref[b], jnp.zeros((groups, n), F32)], axis=1),
            jnp.concatenate([jnp.zeros((8, n), F32), ones], axis=1)], axis=0).astype(BF16)
        ud = lax.dot_general(lhs_t, rhs, (((0,), (0,)), ((), ())), preferred_element_type=F32)
        s_new = s * ud[:, n:2 * n] + ud[:, 0:n]
        so_ref[b] = s_new.reshape(heads, p, n)
        yg = lax.dot_general(cm_ref[b].astype(BF16), s_new.astype(BF16), (((1,), (1,)), ((), ())),
                             preferred_element_type=F32)
        y_ref[b] = jnp.sum(jnp.where(own, yg, 0.0), axis=0, keepdims=True)


def _ssd_sample_state(ssd_state, layer, stacked_so_far, xdt, dae, bm, cm, *, groups):
    _, bs, heads, p, n = ssd_state.shape
    di = heads * p
    tb = _tile(bs, 2)
    xdt3, dae3 = xdt.reshape(bs, 1, di), dae.reshape(bs, 1, di)
    bm3, cm3 = bm.reshape(bs, groups, n), cm.reshape(bs, groups, n)
    rspec = pl.BlockSpec((tb, 1, di), lambda i: (i, 0, 0))
    gspec = pl.BlockSpec((tb, groups, n), lambda i: (i, 0, 0))
    y, stacked = _layer_state_call(
        functools.partial(_ssd_sample_state_kernel, groups=groups),
        layer=layer, state=ssd_state, batch_axis=1, stacked_so_far=stacked_so_far, grid=(bs // tb,), tb=tb,
        other_operands=[xdt3, dae3, bm3, cm3], other_in_specs=[rspec, rspec, gspec, gspec],
        other_out_specs=[rspec], other_out_shapes=[jax.ShapeDtypeStruct((bs, 1, di), F32)],
        name="ssd_sample_state")
    return stacked, y.reshape(bs, di)


def _ssd_sample_post_kernel(y_ref, xs_ref, zs_ref, dskip_ref, nw_ref, o_ref, *, groups):
    y = (y_ref[...] + dskip_ref[...] * xs_ref[...]) * zs_ref[...].astype(F32)
    gw = y.shape[1] // groups
    for g in range(groups):
        cols = slice(g * gw, (g + 1) * gw)
        yg = y[:, cols]
        yg = yg * lax.rsqrt(jnp.mean(yg * yg, axis=-1, keepdims=True) + RMS_EPS) * nw_ref[:, cols]
        o_ref[:, cols] = yg.astype(o_ref.dtype)


def _ssd_sample_post(y, xs, zs, layer, dskip_x, norm_w, *, groups):
    bs, di = y.shape
    full = pl.BlockSpec((bs, di), lambda i: (0, 0))
    return pl.pallas_call(
        functools.partial(_ssd_sample_post_kernel, groups=groups),
        grid=(1,),
        in_specs=[full, full, full, _lspec(layer, (1, di)), _lspec(layer, (1, di))],
        out_specs=full,
        out_shape=jax.ShapeDtypeStruct(y.shape, BF16),
        compiler_params=_params("arbitrary"),
        name="ssd_sample_post",
    )(y, xs, zs, dskip_x, norm_w)


def _merge_kernel(y_ref, wb_ref, sg_ref, a_ref, o_ref):
    bb = jnp.dot(y_ref[...], wb_ref[...], preferred_element_type=F32)
    o_ref[...] = (a_ref[...] + sg_ref[...].astype(F32) * bb).astype(o_ref.dtype)


def _merge(y, sg, a, layer, wb):
    m, k = y.shape
    n = wb.shape[2]
    tm, tn = _tile(m, 1024), _tile(n, 512)
    tile = pl.BlockSpec((tm, tn), lambda i, j: (i, j))
    return pl.pallas_call(
        _merge_kernel,
        grid=(m // tm, n // tn),
        in_specs=[pl.BlockSpec((tm, k), lambda i, j: (i, 0)), _lspec(layer, (k, tn), lambda i, j: (0, j)), tile, tile],
        out_specs=tile,
        out_shape=jax.ShapeDtypeStruct((m, n), BF16),
        compiler_params=_params("parallel", "arbitrary"),
        name="merge",
    )(y, wb, sg, a)


def _proj_ln_kernel(m_ref, w_ref, x_ref, g_ref, b_ref, o_ref, ob_ref, *, alpha):
    f = jnp.dot(m_ref[...], w_ref[...], preferred_element_type=F32)
    h = _layer_norm(alpha * x_ref[...] + f, g_ref[...], b_ref[...])
    o_ref[...] = h
    ob_ref[...] = h.astype(BF16)


def _proj_ln(m_in, x, layer, w, g, b, *, alpha):
    m, k = m_in.shape
    d = w.shape[2]
    tm = _tile(m, 512)
    row = lambda width: pl.BlockSpec((tm, width), lambda i: (i, 0))
    return pl.pallas_call(
        functools.partial(_proj_ln_kernel, alpha=alpha),
        grid=(m // tm,),
        in_specs=[row(k), _lspec(layer, (k, d)), row(d), _lspec(layer, (1, d)), _lspec(layer, (1, d))],
        out_specs=[row(d), row(d)],
        out_shape=[jax.ShapeDtypeStruct((m, d), F32), jax.ShapeDtypeStruct((m, d), BF16)],
        compiler_params=_params("parallel"),
        name="proj_ln",
    )(m_in, w, x, g, b)


def _attn_prompt_kernel(hb_ref, h_ref, k_ref, v_ref, wq_ref, wo_ref, g_ref, b_ref, o_ref, ob_ref, *,
                        heads, alpha):
    q = jnp.dot(hb_ref[0], wq_ref[...], preferred_element_type=F32)
    hd = q.shape[1] // heads
    scale = 1.0 / math.sqrt(hd)
    kb = k_ref[0].astype(BF16)
    vb = v_ref[0].astype(BF16)
    outs = []
    for h in range(heads):
        cols = slice(h * hd, (h + 1) * hd)
        s = lax.dot_general(q[:, cols].astype(BF16), kb[:, cols], (((1,), (1,)), ((), ())),
                            preferred_element_type=F32) * scale
        e = jnp.exp(s - jnp.max(s, axis=-1, keepdims=True))
        pr = e / jnp.sum(e, axis=-1, keepdims=True)
        outs.append(jnp.dot(pr.astype(BF16), vb[:, cols], preferred_element_type=F32))
    o = jnp.concatenate(outs, axis=1).astype(BF16)
    ca = jnp.dot(o, wo_ref[...], preferred_element_type=F32)
    hn = _layer_norm(alpha * h_ref[0] + ca, g_ref[...], b_ref[...])
    o_ref[0] = hn
    ob_ref[0] = hn.astype(BF16)


def _attn_prompt(hb, h, mk, mv, layer, wq, wo, g, b, *, heads, alpha):
    bsz, l, d = h.shape
    nm, md = mk.shape[1], mk.shape[2]
    tq = _tile(l, 512)
    row = pl.BlockSpec((1, tq, d), lambda i, j: (i, j, 0))
    mem = pl.BlockSpec((1, nm, md), lambda i, j: (i, 0, 0))
    return pl.pallas_call(
        functools.partial(_attn_prompt_kernel, heads=heads, alpha=alpha),
        grid=(bsz, l // tq),
        in_specs=[row, row, mem, mem, _lspec(layer, (d, md)), _lspec(layer, (md, d)), _lspec(layer, (1, d)),
                  _lspec(layer, (1, d))],
        out_specs=[row, row],
        out_shape=[jax.ShapeDtypeStruct((bsz, l, d), F32), jax.ShapeDtypeStruct((bsz, l, d), BF16)],
        compiler_params=_params("parallel", "parallel"),
        name="attn_prompt",
    )(hb, h, mk, mv, wq, wo, g, b)


def _attn_sample_kernel(hb_ref, h_ref, k_ref, v_ref, wq_ref, wo_ref, g_ref, b_ref, o_ref, ob_ref, att_ref, *,
                        heads, alpha):
    tb = hb_ref.shape[0]
    rows, hd = k_ref.shape[1], k_ref.shape[2]
    reps = 8 // heads
    scale = 1.0 / math.sqrt(hd)
    hb = hb_ref[...]
    qs = [jnp.dot(hb, wq_ref[:, h * hd:(h + 1) * hd], preferred_element_type=F32) for h in range(heads)]

    def over_copies(x, op):
        out = x
        for r in range(1, reps):
            out = op(out, pltpu.roll(x, r * heads, axis=0))
        return out

    for bi in range(tb):
        q8 = jnp.concatenate([q[bi:bi + 1, :] for q in qs] * reps, axis=0)
        k3 = k_ref[bi].reshape(rows // 8, 8, hd)
        v3 = v_ref[bi].reshape(rows // 8, 8, hd)
        s = jnp.sum(k3 * q8[None], axis=-1, keepdims=True) * scale
        m = over_copies(jnp.max(s, axis=0), jnp.maximum)
        e = jnp.exp(s - m[None])
        den = over_copies(jnp.sum(e, axis=0), jnp.add)
        o8 = over_copies(jnp.sum((e / den[None]) * v3, axis=0), jnp.add)
        att_ref[bi] = o8[0:heads, :]
    ca = None
    for h in range(heads):
        t = jnp.dot(att_ref[:, h, :].astype(BF16), wo_ref[h * hd:(h + 1) * hd, :], preferred_element_type=F32)
        ca = t if ca is None else ca + t
    hn = _layer_norm(alpha * h_ref[...] + ca, g_ref[...], b_ref[...])
    o_ref[...] = hn
    ob_ref[...] = hn.astype(BF16)


def _attn_sample(hb, h, cache_k, cache_v, layer, wq, wo, g, b, *, alpha):
    bs, d = h.shape
    depth, _, nm, heads, hd = cache_k.shape
    assert 8 % heads == 0
    md = heads * hd
    ck = cache_k.reshape(depth, bs, nm * heads, hd)
    cv = cache_v.reshape(depth, bs, nm * heads, hd)
    tb = _tile(bs, 8)
    row = pl.BlockSpec((tb, d), lambda i: (i, 0))
    mem = _lspec(layer, (tb, nm * heads, hd), lambda i: (i, 0, 0))
    return pl.pallas_call(
        functools.partial(_attn_sample_kernel, heads=heads, alpha=alpha),
        grid=(bs // tb,),
        in_specs=[row, row, mem, mem, _lspec(layer, (d, md)), _lspec(layer, (md, d)), _lspec(layer, (1, d)),
                  _lspec(layer, (1, d))],
        out_specs=[row, row],
        out_shape=[jax.ShapeDtypeStruct((bs, d), F32), jax.ShapeDtypeStruct((bs, d), BF16)],
        scratch_shapes=[pltpu.VMEM((tb, heads, hd), F32)],
        compiler_params=_params("parallel"),
        name="attn_sample",
    )(hb, h, ck, cv, wq, wo, g, b)


def _ffn_kernel(hb_ref, h_ref, wg_ref, wu_ref, wd_ref, g_ref, b_ref, o_ref, ob_ref, *, alpha):
    f = pl.program_id(1)
    tm, d = o_ref.shape

    @pl.when(f == 0)
    def _():
        o_ref[...] = jnp.zeros(o_ref.shape, F32)

    x = hb_ref[...]
    gate = jnp.dot(x, wg_ref[...], preferred_element_type=F32)
    up = jnp.dot(x, wu_ref[...], preferred_element_type=F32)
    act = (_silu(gate) * up).astype(BF16)
    for c in range(2):
        cols = slice(c * d // 2, (c + 1) * d // 2)
        o_ref[:, cols] += jnp.dot(act, wd_ref[:, cols], preferred_element_type=F32)

    @pl.when(f == pl.num_programs(1) - 1)
    def _():
        for r in range(4):
            rows = slice(r * tm // 4, (r + 1) * tm // 4)
            hn = _layer_norm(alpha * h_ref[rows, :] + o_ref[rows, :], g_ref[...], b_ref[...])
            o_ref[rows, :] = hn
            ob_ref[rows, :] = hn.astype(BF16)


def _ffn(hb, h, layer, wg, wu, wd, g, b, *, alpha):
    m, d = h.shape
    ff = wg.shape[2]
    tm, tf = _tile(m, 1024), 512
    assert ff % tf == 0
    row_once = pl.BlockSpec((tm, d), lambda i, f: (i, 0), pipeline_mode=pl.Buffered(1))
    return pl.pallas_call(
        functools.partial(_ffn_kernel, alpha=alpha),
        grid=(m // tm, ff // tf),
        in_specs=[row_once, row_once, _lspec(layer, (d, tf), lambda i, f: (0, f)),
                  _lspec(layer, (d, tf), lambda i, f: (0, f)), _lspec(layer, (tf, d), lambda i, f: (f, 0)),
                  _lspec(layer, (1, d)), _lspec(layer, (1, d))],
        out_specs=[row_once, row_once],
        out_shape=[jax.ShapeDtypeStruct((m, d), F32), jax.ShapeDtypeStruct((m, d), BF16)],
        compiler_params=_params("parallel", "arbitrary"),
        name="ffn",
    )(hb, h, wg, wu, wd, g, b)


def _expansion_matrix(n_in, reps, n_pad):
    r = jnp.arange(n_pad)[:, None]
    c = jnp.arange(n_in * reps)[None, :] // reps
    e = ((r == c) & (r < n_in)).astype(BF16)
    return jnp.concatenate([e, e], axis=0)


def kernel(x_prompt, x_sample, mem_prompt, state_ssd, state_conv, state_pool, cache_mem_k, cache_mem_v, w_in, w_pool, pool_scale, w_pool_br, conv_w, conv_b, dt_bias, a_log, d_skip, ssd_norm_w, w_ssd_br, w_out, ln_mix_g, ln_mix_b, w_mq, w_mk, w_mv, w_mo, ln_mem_g, ln_mem_b, w_ffn_gate, w_ffn_up, w_ffn_down, ln_ffn_g, ln_ffn_b):
    bp, seq, d = x_prompt.shape
    bs = x_sample.shape[0]
    depth = w_in.shape[0]
    _, _, heads, p, n = state_ssd.shape
    di = heads * p
    cd = conv_w.shape[2]
    groups = (cd - di) // (2 * n)
    gn = groups * n
    hpg = heads // groups
    n_mem, mem_heads, mem_hd = cache_mem_k.shape[2:]
    md = mem_heads * mem_hd
    nbuf = state_pool.shape[2]
    kw1 = state_conv.shape[2]
    alpha = (2 * depth) ** 0.25
    assert seq % SSD_CHUNK == 0 and heads <= LANES and LANES % p == 0 and hpg % (LANES // p) == 0

    s3 = d + di + cd
    s4 = s3 + heads

    w_main = w_in[:, :, :s3].astype(BF16)
    w_dt = jnp.pad(w_in[:, :, s3:s4], ((0, 0), (0, 0), (0, LANES - heads))).astype(BF16)
    w_gates = w_in[:, :, s4:].astype(BF16)
    bf = lambda w: w.astype(BF16)
    wpool_b, wpbr_b, wsbr_b, wout_b = bf(w_pool), bf(w_pool_br), bf(w_ssd_br), bf(w_out)
    wq_b, wk_b, wv_b, wo_b = bf(w_mq), bf(w_mk), bf(w_mv), bf(w_mo)
    wg_b, wu_b, wd_b = bf(w_ffn_gate), bf(w_ffn_up), bf(w_ffn_down)
    vec = lambda v: v.astype(F32).reshape(depth, 1, v.shape[-1])
    pad_h = lambda v: jnp.pad(v.astype(F32), ((0, 0), (0, LANES - heads))).reshape(depth, 1, LANES)
    scale, cbias, nw = vec(pool_scale), vec(conv_b), vec(ssd_norm_w)
    bias_p, alog_p = pad_h(dt_bias), pad_h(a_log)
    dskip_x = vec(jnp.repeat(d_skip, p, axis=1))
    mix_g, mix_b, mem_g, mem_b, ffn_g, ffn_b = (vec(v) for v in (ln_mix_g, ln_mix_b, ln_mem_g, ln_mem_b,
                                                                   ln_ffn_g, ln_ffn_b))
    e_group = _expansion_matrix(hpg, p, LANES)
    e_all = _expansion_matrix(heads, p, LANES)

    pool_state_t = jnp.transpose(state_pool, (0, 2, 1, 3))
    conv_state_t = jnp.transpose(state_conv, (0, 2, 1, 3))

    hp, hs = x_prompt.reshape(bp * seq, d), x_sample.reshape(bs, d)
    hp_b, hs_b = hp.astype(BF16), hs.astype(BF16)
    memory_bf = mem_prompt.reshape(bp * n_mem, d).astype(BF16)

    outs = {k: [] for k in ("ssd_p", "conv_p", "pool_p", "mk", "mv")}
    new_ssd_s = new_conv_s = new_pool_s = None
    for l in range(depth):

        def in_proj(hb):
            u = _mm(hb, w_main, l, col0=0, n=d, name="in_u")
            zs = _mm(hb, w_main, l, col0=d, n=di, act="silu", out_dtype=BF16, name="in_z")
            xbc = _mm(hb, w_main, l, col0=d + di, n=cd, name="in_xbc")
            dtr = _mm(hb, w_dt, l, name="in_dt")
            sgp = _mm(hb, w_gates, l, col0=0, n=d, act="sigmoid", out_dtype=BF16, name="in_gp")
            sgs = _mm(hb, w_gates, l, col0=d, n=d, act="sigmoid", out_dtype=BF16, name="in_gs")
            return u, zs, xbc, dtr, sgp, sgs

        def tail(h, y, sgs, a_gated, attn):
            merged = _merge(y, sgs, a_gated, l, wsbr_b)
            h1, h1b = _proj_ln(merged, h, l, wout_b, mix_g, mix_b, alpha=alpha)
            h2, h2b = attn(h1, h1b)
            return _ffn(h2b, h2, l, wg_b, wu_b, wd_b, ffn_g, ffn_b, alpha=alpha)

        u, zs, xbc, dtr, sgp, sgs = in_proj(hp_b)
        u3, xbc3 = u.reshape(bp, seq, d), xbc.reshape(bp, seq, cd)
        a_gated = _pool_prompt(u3, sgp.reshape(bp, seq, d), l, wpool_b, scale, wpbr_b).reshape(bp * seq, d)
        dt, cum, cumt = _dt_prep(dtr, l, bias_p, alog_p)
        r3 = lambda a: a.reshape(bp, seq, a.shape[-1])
        y, st = _ssd_prompt(xbc3, r3(zs), r3(dt), r3(cum), r3(cumt), l, conv_w, cbias, dskip_x, nw, e_group,
                            di=di, n=n, p=p, groups=groups)
        mk_p = _mm(memory_bf, wk_b, l, name="mem_k").reshape(bp, n_mem, md)
        mv_p = _mm(memory_bf, wv_b, l, name="mem_v").reshape(bp, n_mem, md)

        def attn_p(h1, h1b):
            o, ob = _attn_prompt(h1b.reshape(bp, seq, d), h1.reshape(bp, seq, d), mk_p, mv_p, l, wq_b, wo_b,
                                 mem_g, mem_b, heads=mem_heads, alpha=alpha)
            return o.reshape(bp * seq, d), ob.reshape(bp * seq, d)

        hp, hp_b = tail(hp, y.reshape(bp * seq, di), sgs, a_gated, attn_p)
        outs["ssd_p"].append(st)
        outs["conv_p"].append(xbc3[:, seq - kw1:, :])
        outs["pool_p"].append(u3[:, seq - nbuf:, :])
        outs["mk"].append(mk_p.reshape(bp, n_mem, mem_heads, mem_hd))
        outs["mv"].append(mv_p.reshape(bp, n_mem, mem_heads, mem_hd))

        u, zs, xbc, dtr, sgp, sgs = in_proj(hs_b)
        a_gated, new_pool_s = _pool_sample(u, sgp, pool_state_t, l, new_pool_s, wpool_b, scale, wpbr_b)
        xs, xdt, dae, bm, cm, new_conv_s = _ssd_sample_pre(xbc, dtr, conv_state_t, l, new_conv_s, conv_w, cbias,
                                                           bias_p, alog_p, e_all, di=di, gn=gn)
        new_ssd_s, ys = _ssd_sample_state(state_ssd, l, new_ssd_s, xdt, dae, bm, cm, groups=groups)
        y = _ssd_sample_post(ys, xs, zs, l, dskip_x, nw, groups=groups)

        def attn_s(h1, h1b):
            return _attn_sample(h1b, h1, cache_mem_k, cache_mem_v, l, wq_b, wo_b, mem_g, mem_b, alpha=alpha)

        hs, hs_b = tail(hs, y, sgs, a_gated, attn_s)

    stack = lambda k: jnp.stack(outs[k])
    return (hp.reshape(bp, seq, d), hs.reshape(bs, 1, d), stack("ssd_p"), stack("conv_p"), stack("pool_p"),
            stack("mk"), stack("mv"), new_ssd_s, jnp.transpose(new_conv_s, (0, 2, 1, 3)),
            jnp.transpose(new_pool_s, (0, 2, 1, 3)))
```

```python
import functools
import math

import jax
import jax.numpy as jnp
from jax import lax
from jax.experimental import pallas as pl
from jax.experimental.pallas import tpu as pltpu

F32 = jnp.float32
BF16 = jnp.bfloat16

POOL_WINDOWS = (2, 4, 8, 16)
PAST_LEN = 16384
SSD_CHUNK = 128
LN_EPS = 1e-5
RMS_EPS = 1e-5
LANES = 128
VMEM_LIMIT_BYTES = 56 * 1024 * 1024


def _params(*sem):
    return pltpu.CompilerParams(dimension_semantics=sem, vmem_limit_bytes=VMEM_LIMIT_BYTES)


def _silu(x):
    return x * jax.nn.sigmoid(x)


def _softplus(x):
    return jnp.maximum(x, 0.0) + jnp.log(1.0 + jnp.exp(-jnp.abs(x)))


def _layer_norm(x, g, b):
    mu = jnp.mean(x, axis=-1, keepdims=True)
    xc = x - mu
    var = jnp.mean(xc * xc, axis=-1, keepdims=True)
    return xc * lax.rsqrt(var + LN_EPS) * g + b


def _split2(x):
    hi = x.astype(BF16)
    mid = (x - hi.astype(F32)).astype(BF16)
    return hi, mid


def _tile(n, pref):
    t = min(n, pref)
    while n % t:
        t //= 2
    return t


def _lspec(layer, block, index_map=None):
    nd = len(block)
    if index_map is None:
        index_map = lambda *g: (0,) * nd
    return pl.BlockSpec((None,) + tuple(block), lambda *g: (layer,) + tuple(index_map(*g)))


def _mm_kernel(x_ref, w_ref, o_ref, *, act, w_is_nk):
    contract_w = 1 if w_is_nk else 0
    acc = lax.dot_general(x_ref[...], w_ref[...], (((1,), (contract_w,)), ((), ())), preferred_element_type=F32)
    if act == "silu":
        acc = _silu(acc)
    elif act == "sigmoid":
        acc = jax.nn.sigmoid(acc)
    o_ref[...] = acc.astype(o_ref.dtype)


def _mm(x, w, layer, *, col0=0, n=None, act=None, out_dtype=F32, w_is_nk=False, name):
    m, k = x.shape
    n = w.shape[1 if w_is_nk else 2] if n is None else n
    tm, tn = _tile(m, 1024), _tile(n, 1024)
    assert col0 % tn == 0
    off = col0 // tn
    w_spec = (_lspec(layer, (tn, k), lambda i, j: (off + j, 0)) if w_is_nk
              else _lspec(layer, (k, tn), lambda i, j: (0, off + j)))
    return pl.pallas_call(
        functools.partial(_mm_kernel, act=act, w_is_nk=w_is_nk),
        grid=(m // tm, n // tn),
        in_specs=[pl.BlockSpec((tm, k), lambda i, j: (i, 0)), w_spec],
        out_specs=pl.BlockSpec((tm, tn), lambda i, j: (i, j)),
        out_shape=jax.ShapeDtypeStruct((m, n), out_dtype),
        compiler_params=_params("parallel", "arbitrary"),
        name=name,
    )(x, w)


def _pool_project(pooled_ref, wg_ref, scale_ref, wbr_ref, sg, n_groups, gc):
    for g in range(n_groups):
        cols = slice(g * gc, (g + 1) * gc)
        pg = jnp.dot(pooled_ref[:, cols], wg_ref[g], preferred_element_type=F32) * scale_ref[:, cols]
        pooled_ref[:, cols] = pg.astype(BF16)
    a = jnp.dot(pooled_ref[...], wbr_ref[...], preferred_element_type=F32)
    return a * sg.astype(F32)


def _pool_prompt_kernel(u_ref, wg_ref, scale_ref, wbr_ref, sg_ref, o_ref, carry_ref, pooled_ref, *, tl, gc):
    j = pl.program_id(1)

    @pl.when(j == 0)
    def _():
        carry_ref[...] = jnp.zeros(carry_ref.shape, F32)

    nt = tl // 8
    sub = lax.broadcasted_iota(jnp.int32, (nt, 8, gc), 1)
    pos = j * tl + lax.broadcasted_iota(jnp.int32, (tl, 1), 0)
    for g, w in enumerate(POOL_WINDOWS):
        cols = slice(g * gc, (g + 1) * gc)
        ug = u_ref[0, :, cols]
        acc, s, level = ug, 1, 0
        while s < w:
            a3 = acc.reshape(nt, 8, gc)
            before = jnp.concatenate([carry_ref[level, :, cols][None], a3[:nt - 1]], axis=0)
            carry_ref[level, :, cols] = acc[tl - 8:tl, :]
            shifted = before if s == 8 else pltpu.roll(jnp.where(sub >= 8 - s, before, a3), s, axis=1)
            acc = acc + shifted.reshape(tl, gc)
            s, level = 2 * s, level + 1
        cnt = jnp.minimum(w, pos + 1).astype(F32)
        pooled_ref[:, cols] = (acc / cnt - ug).astype(BF16)
    o_ref[0] = _pool_project(pooled_ref, wg_ref, scale_ref, wbr_ref, sg_ref[0], len(POOL_WINDOWS), gc)


def _pool_prompt(u, sg, layer, wg, scale, wbr):
    b, l, d = u.shape
    _, ng, gc, _ = wg.shape
    tl = _tile(l, 256)
    assert all(w in (1, 2, 4, 8, 16) for w in POOL_WINDOWS) and tl % 8 == 0
    levels = max(POOL_WINDOWS).bit_length() - 1
    tok = pl.BlockSpec((1, tl, d), lambda i, j: (i, j, 0))
    return pl.pallas_call(
        functools.partial(_pool_prompt_kernel, tl=tl, gc=gc),
        grid=(b, l // tl),
        in_specs=[tok, _lspec(layer, (ng, gc, gc)), _lspec(layer, (1, d)), _lspec(layer, (d, d)), tok],
        out_specs=tok,
        out_shape=jax.ShapeDtypeStruct((b, l, d), F32),
        scratch_shapes=[pltpu.VMEM((levels, 8, d), F32), pltpu.VMEM((tl, d), BF16)],
        compiler_params=_params("parallel", "arbitrary"),
        name="pool_prompt",
    )(u, wg, scale, wbr, sg)


def _layer_state_call(kern, *, layer, state, batch_axis, stacked_so_far, grid, tb, other_operands, other_in_specs,
                      other_out_specs, other_out_shapes, scratch_shapes=(), name):
    blk = tuple(tb if a == batch_axis else s for a, s in enumerate(state.shape))[1:]
    spec = _lspec(layer, blk, lambda i: tuple(i if a == batch_axis else 0 for a in range(1, state.ndim)))
    if stacked_so_far is None:
        stacked_so_far = jnp.zeros(state.shape, state.dtype)
    operands = list(other_operands) + [state, stacked_so_far]
    in_specs = list(other_in_specs) + [spec, pl.BlockSpec(memory_space=pl.ANY)]
    return pl.pallas_call(
        kern,
        grid=grid,
        in_specs=in_specs,
        out_specs=list(other_out_specs) + [spec],
        out_shape=list(other_out_shapes) + [jax.ShapeDtypeStruct(state.shape, state.dtype)],
        input_output_aliases={len(operands) - 1: len(other_out_shapes)},
        scratch_shapes=list(scratch_shapes),
        compiler_params=_params("parallel"),
        name=name,
    )(*operands)


def _pool_sample_kernel(u_ref, wg_ref, scale_ref, wbr_ref, sg_ref, buf_ref, stack_hbm_ref,
                        o_ref, nbuf_ref, pooled_ref, *, gc):
    del stack_hbm_ref
    nbuf = buf_ref.shape[0]
    u = u_ref[...]
    for r in range(nbuf - 1):
        nbuf_ref[r] = buf_ref[r + 1]
    nbuf_ref[nbuf - 1] = u
    for g, w in enumerate(POOL_WINDOWS):
        cols = slice(g * gc, (g + 1) * gc)
        ug = u[:, cols]
        s = ug
        for k in range(1, w):
            s = s + buf_ref[nbuf - k, :, cols]
        cnt = float(min(w, PAST_LEN + 1))
        pooled_ref[:, cols] = (s / cnt - ug).astype(BF16)
    o_ref[...] = _pool_project(pooled_ref, wg_ref, scale_ref, wbr_ref, sg_ref[...], len(POOL_WINDOWS), gc)


def _pool_sample(u, sg, pool_state_t, layer, stacked_so_far, wg, scale, wbr):
    bs, d = u.shape
    _, ng, gc, _ = wg.shape
    tb = _tile(bs, 32)
    row = pl.BlockSpec((tb, d), lambda i: (i, 0))
    return _layer_state_call(
        functools.partial(_pool_sample_kernel, gc=gc),
        layer=layer, state=pool_state_t, batch_axis=2, stacked_so_far=stacked_so_far, grid=(bs // tb,), tb=tb,
        other_operands=[u, wg, scale, wbr, sg],
        other_in_specs=[row, _lspec(layer, (ng, gc, gc)), _lspec(layer, (1, d)), _lspec(layer, (d, d)), row],
        other_out_specs=[row], other_out_shapes=[jax.ShapeDtypeStruct((bs, d), F32)],
        scratch_shapes=[pltpu.VMEM((tb, d), BF16)], name="pool_sample")


def _dt_prep_kernel(dtr_ref, bias_ref, alog_ref, dt_ref, cum_ref, cumt_ref):
    c = SSD_CHUNK
    dt_ref[...] = _softplus(dtr_ref[...] + bias_ref[...])
    a = -jnp.exp(alog_ref[...])
    row = lax.broadcasted_iota(jnp.int32, (c, dtr_ref.shape[1]), 0)
    for ch in range(dtr_ref.shape[0] // c):
        rows = slice(ch * c, (ch + 1) * c)
        cum = dt_ref[rows, :] * a
        k = 1
        while k < c:
            cum = cum + jnp.where(row >= k, pltpu.roll(cum, k, axis=0), 0.0)
            k *= 2
        cum_ref[rows, :] = cum
        cumt_ref[rows, :] = cum.T


def _dt_prep(dtr, layer, bias, alog):
    t, hp = dtr.shape
    c = SSD_CHUNK * _tile(t // SSD_CHUNK, 8)
    spec = pl.BlockSpec((c, hp), lambda i: (i, 0))
    return pl.pallas_call(
        _dt_prep_kernel,
        grid=(t // c,),
        in_specs=[spec, _lspec(layer, (1, hp)), _lspec(layer, (1, hp))],
        out_specs=[spec, spec, spec],
        out_shape=[jax.ShapeDtypeStruct((t, hp), F32)] * 3,
        compiler_params=_params("parallel"),
        name="dt_prep",
    )(dtr, bias, alog)


def _conv_silu(x, carry_ref, w_ref, b_ref):
    rows, wd = x.shape
    kw = w_ref.shape[0]
    nt = rows // 8
    last_row = lax.broadcasted_iota(jnp.int32, (nt, 8, wd), 1) == 7
    t = x * w_ref[0:1, :]
    for k in range(1, kw):
        t3 = t.reshape(nt, 8, wd)
        before = jnp.concatenate([carry_ref[k - 1][None], t3[:nt - 1]], axis=0)
        carry_ref[k - 1] = t[rows - 8:rows, :]
        shifted = pltpu.roll(jnp.where(last_row, before, t3), 1, axis=1).reshape(rows, wd)
        t = x * w_ref[k:k + 1, :] + shifted
    return _silu(t + b_ref[...])


def _ssd_prompt_kernel(xs_ref, bm_ref, cm_ref, zs_ref, dt_ref, cum_ref, cumt_ref,
                       wx_ref, wb_ref, wc_ref, bx_ref, bb_ref, bc_ref, dskip_ref, nw_ref, e_ref,
                       y_ref, st_ref, stt_ref, hx_ref, hb_ref, hc_ref, *, hpg, p, gps):
    c = pl.program_id(2)
    nc = pl.num_programs(2)
    rows = xs_ref.shape[1]
    n = bm_ref.shape[2] // gps
    gw = hpg * p

    @pl.when(c == 0)
    def _():
        for r in (stt_ref, hx_ref, hb_ref, hc_ref):
            r[...] = jnp.zeros(r.shape, F32)

    x_all = _conv_silu(xs_ref[0], hx_ref, wx_ref, bx_ref)
    b_all = _conv_silu(bm_ref[0], hb_ref, wb_ref, bb_ref).astype(BF16)
    c_all = _conv_silu(cm_ref[0], hc_ref, wc_ref, bc_ref).astype(BF16)
    causal = lax.broadcasted_iota(jnp.int32, (rows, rows), 0) >= lax.broadcasted_iota(jnp.int32, (rows, rows), 1)
    lane = lax.broadcasted_iota(jnp.int32, (rows, LANES), 1)
    heads_per_tile = LANES // p
    lane_masks = [jnp.where(lane // p == r, 1.0, 0.0).astype(BF16) for r in range(heads_per_tile)]

    for gi in range(gps):
        g = pl.program_id(1) * gps + gi
        gcols = slice(gi * gw, (gi + 1) * gw)
        x = x_all[:, gcols]
        bmat = b_all[:, gi * n:(gi + 1) * n]
        cmat = c_all[:, gi * n:(gi + 1) * n]
        shift = (LANES - g * hpg) % LANES
        dtg = pltpu.roll(dt_ref[0], shift, axis=1)
        cumg = pltpu.roll(cum_ref[0], shift, axis=1)
        cumtg = cumt_ref[0, pl.ds(pl.multiple_of(g * hpg, hpg), hpg), :]
        last = cumg[rows - 1:rows, :]
        stack = jnp.concatenate([dtg, jnp.exp(cumg), jnp.exp(last - cumg)], axis=0)
        hi, mid = _split2(stack)
        expanded = jnp.dot(jnp.concatenate([hi, mid], axis=1), e_ref[...], preferred_element_type=F32)
        dt_x, ecum_x, ws_x = expanded[0:rows], expanded[rows:2 * rows], expanded[2 * rows:3 * rows]

        xdt = x * dt_x
        xdt_b = xdt.astype(BF16)
        xdtw_b = (xdt * ws_x).astype(BF16)
        cb = lax.dot_general(cmat, bmat, (((1,), (1,)), ((), ())), preferred_element_type=F32)
        st_old = stt_ref[:, gcols]
        y = jnp.dot(cmat, st_old.astype(BF16), preferred_element_type=F32) * ecum_x
        stt_ref[:, gcols] = st_old * ecum_x[rows - 1:rows, :] + lax.dot_general(
            bmat, xdtw_b, (((0,), (0,)), ((), ())), preferred_element_type=F32)

        cum2 = cumg * math.log2(math.e)
        cumt2 = cumtg * math.log2(math.e)
        diag = []
        for q in range(hpg // heads_per_tile):
            xq = xdt_b[:, q * LANES:(q + 1) * LANES]
            ms, xr = [], []
            for r in range(heads_per_tile):
                h = q * heads_per_tile + r
                dec = jnp.exp2(cum2[:, h:h + 1] - cumt2[h:h + 1, :])
                ms.append(jnp.where(causal, cb * dec, 0.0).astype(BF16))
                xr.append(xq * lane_masks[r])
            diag.append(jnp.dot(jnp.concatenate(ms, axis=1), jnp.concatenate(xr, axis=0),
                                preferred_element_type=F32))
        y = y + jnp.concatenate(diag, axis=1) + dskip_ref[:, gcols] * x
        y = y * zs_ref[0, :, gcols].astype(F32)
        y = y * lax.rsqrt(jnp.mean(y * y, axis=-1, keepdims=True) + RMS_EPS) * nw_ref[:, gcols]
        y_ref[0, :, gcols] = y.astype(y_ref.dtype)

    @pl.when(c == nc - 1)
    def _():
        st_ref[0] = stt_ref[...].T.reshape(st_ref.shape[1:])


def _ssd_prompt(xbc, zs, dt, cum, cumt, layer, conv_w, conv_b, dskip_x, norm_w, e_mat, *, di, n, p, groups):
    b, l, cd = xbc.shape
    c = SSD_CHUNK
    heads = di // p
    hpg = heads // groups
    gps = 4 if groups % 4 == 0 else 1
    gw = hpg * p * gps
    nn = n * gps
    hp = dt.shape[-1]
    xoff, boff, coff = 0, di // nn, (di + groups * n) // nn

    def seg(width, off_blocks):
        return pl.BlockSpec((1, c, width), lambda i, g, k: (i, k, off_blocks + g))

    def wseg(rows, width, off_blocks):
        return _lspec(layer, (rows, width), lambda i, g, k: (0, off_blocks + g))

    small = pl.BlockSpec((1, c, hp), lambda i, g, k: (i, k, 0))
    kw = conv_w.shape[1]
    return pl.pallas_call(
        functools.partial(_ssd_prompt_kernel, hpg=hpg, p=p, gps=gps),
        grid=(b, groups // gps, l // c),
        in_specs=[
            seg(gw, xoff), seg(nn, boff), seg(nn, coff), seg(gw, 0), small, small, small,
            wseg(kw, gw, xoff), wseg(kw, nn, boff), wseg(kw, nn, coff),
            wseg(1, gw, xoff), wseg(1, nn, boff), wseg(1, nn, coff),
            wseg(1, gw, 0), wseg(1, gw, 0),
            pl.BlockSpec(e_mat.shape, lambda i, g, k: (0, 0)),
        ],
        out_specs=[
            pl.BlockSpec((1, c, gw), lambda i, g, k: (i, k, g)),
            pl.BlockSpec((1, hpg * gps, p, n), lambda i, g, k: (i, g, 0, 0)),
        ],
        out_shape=[jax.ShapeDtypeStruct((b, l, di), BF16), jax.ShapeDtypeStruct((b, heads, p, n), F32)],
        scratch_shapes=[
            pltpu.VMEM((n, gw), F32),
            pltpu.VMEM((kw - 1, 8, gw), F32), pltpu.VMEM((kw - 1, 8, nn), F32), pltpu.VMEM((kw - 1, 8, nn), F32),
        ],
        compiler_params=_params("parallel", "parallel", "arbitrary"),
        name="ssd_prompt",
    )(xbc, xbc, xbc, zs, dt, cum, cumt, conv_w, conv_w, conv_w, conv_b, conv_b, conv_b, dskip_x, norm_w, e_mat)


def _ssd_sample_pre_kernel(xbc_ref, w_ref, b_ref, dtr_ref, bias_ref, alog_ref, e_ref, cbuf_ref, stack_hbm_ref,
                           xs_ref, xdt_ref, dae_ref, bm_ref, cm_ref, ncbuf_ref, *, di, gn):
    del stack_hbm_ref
    kw = w_ref.shape[0]
    xbc = xbc_ref[...]
    acc = b_ref[...]
    for k in range(kw - 1):
        acc = acc + cbuf_ref[k] * w_ref[k:k + 1, :]
    acc = _silu(acc + xbc * w_ref[kw - 1:kw, :])
    for k in range(kw - 2):
        ncbuf_ref[k] = cbuf_ref[k + 1]
    ncbuf_ref[kw - 2] = xbc
    xs = acc[:, 0:di]
    dt = _softplus(dtr_ref[...] + bias_ref[...])
    dae_ref[...] = jnp.exp(dt * (-jnp.exp(alog_ref[...])))
    hi, mid = _split2(dt)
    dt_x = jnp.dot(jnp.concatenate([hi, mid], axis=1), e_ref[...], preferred_element_type=F32)
    xs_ref[...] = xs
    xdt_ref[...] = xs * dt_x
    bm_ref[...] = acc[:, di:di + gn]
    cm_ref[...] = acc[:, di + gn:di + 2 * gn]


def _ssd_sample_pre(xbc, dtr, conv_state_t, layer, stacked_so_far, conv_w, conv_b, bias, alog, e_mat, *, di, gn):
    bs, cd = xbc.shape
    hp = dtr.shape[1]
    kw = conv_w.shape[1]
    tb = _tile(bs, 32)
    row = lambda w: pl.BlockSpec((tb, w), lambda i: (i, 0))
    return _layer_state_call(
        functools.partial(_ssd_sample_pre_kernel, di=di, gn=gn),
        layer=layer, state=conv_state_t, batch_axis=2, stacked_so_far=stacked_so_far, grid=(bs // tb,), tb=tb,
        other_operands=[xbc, conv_w, conv_b, dtr, bias, alog, e_mat],
        other_in_specs=[row(cd), _lspec(layer, (kw, cd)), _lspec(layer, (1, cd)), row(hp), _lspec(layer, (1, hp)),
                        _lspec(layer, (1, hp)), pl.BlockSpec(e_mat.shape, lambda i: (0, 0))],
        other_out_specs=[row(di), row(di), row(hp), row(gn), row(gn)],
        other_out_shapes=[jax.ShapeDtypeStruct((bs, di), F32)] * 2 + [jax.ShapeDtypeStruct((bs, hp), F32)]
        + [jax.ShapeDtypeStruct((bs, gn), F32)] * 2,
        name="ssd_sample_pre")


def _ssd_sample_state_kernel(dae_ref, xdt_ref, bm_ref, cm_ref, s_ref, stack_hbm_ref, y_ref, so_ref, *, groups):
    del stack_hbm_ref
    tb, heads, p, n = s_ref.shape
    di = heads * p
    gw = di // groups
    grow = lax.broadcasted_iota(jnp.int32, (groups, di), 0)
    gcol = lax.broadcasted_iota(jnp.int32, (groups, di), 1) // gw
    own = grow == gcol
    fill = (-groups) % 16
    for b in range(tb):
        seq = pl.program_id(0) * tb + b
        xg = jnp.where(own, jnp.broadcast_to(xdt_ref[b], (groups, di)), 0.0)
        bmat = bm_ref[b]
        if fill:
            xg = jnp.concatenate([xg, jnp.zeros((fill, di), F32)], axis=0)
            bmat = jnp.concatenate([bmat, jnp.zeros((fill, n), F32)], axis=0)
        upd = lax.dot_general(xg.astype(BF16), bmat.astype(BF16), (((0,), (0,)), ((), ())),
                              preferred_element_type=F32)
        for h in range(heads):
            so_ref[b, h] = s_ref[b, h] * dae_ref[seq, h] + upd[h * p:(h + 1) * p, :]
        s_new = so_ref[b].reshape(di, n).astype(BF16)
        yg = lax.dot_general(cm_ref[b].astype(BF16), s_new, (((1,), (1,)), ((), ())),
                             preferred_element_type=F32)
        y_ref[b] = jnp.sum(jnp.where(own, yg, 0.0), axis=0, keepdims=True)


def _ssd_sample_state(ssd_state, layer, stacked_so_far, xdt, dae, bm, cm, *, groups):
    _, bs, heads, p, n = ssd_state.shape
    di = heads * p
    tb = _tile(bs, 2)
    xdt3 = xdt.reshape(bs, 1, di)
    bm3, cm3 = bm.reshape(bs, groups, n), cm.reshape(bs, groups, n)
    rspec = pl.BlockSpec((tb, 1, di), lambda i: (i, 0, 0))
    gspec = pl.BlockSpec((tb, groups, n), lambda i: (i, 0, 0))
    y, stacked = _layer_state_call(
        functools.partial(_ssd_sample_state_kernel, groups=groups),
        layer=layer, state=ssd_state, batch_axis=1, stacked_so_far=stacked_so_far, grid=(bs // tb,), tb=tb,
        other_operands=[dae, xdt3, bm3, cm3],
        other_in_specs=[pl.BlockSpec(memory_space=pltpu.SMEM), rspec, gspec, gspec],
        other_out_specs=[rspec], other_out_shapes=[jax.ShapeDtypeStruct((bs, 1, di), F32)],
        name="ssd_sample_state")
    return stacked, y.reshape(bs, di)


def _ssd_sample_post_kernel(y_ref, xs_ref, zs_ref, dskip_ref, nw_ref, o_ref, *, groups):
    y = (y_ref[...] + dskip_ref[...] * xs_ref[...]) * zs_ref[...].astype(F32)
    gw = y.shape[1] // groups
    for g in range(groups):
        cols = slice(g * gw, (g + 1) * gw)
        yg = y[:, cols]
        yg = yg * lax.rsqrt(jnp.mean(yg * yg, axis=-1, keepdims=True) + RMS_EPS) * nw_ref[:, cols]
        o_ref[:, cols] = yg.astype(o_ref.dtype)


def _ssd_sample_post(y, xs, zs, layer, dskip_x, norm_w, *, groups):
    bs, di = y.shape
    full = pl.BlockSpec((bs, di), lambda i: (0, 0))
    return pl.pallas_call(
        functools.partial(_ssd_sample_post_kernel, groups=groups),
        grid=(1,),
        in_specs=[full, full, full, _lspec(layer, (1, di)), _lspec(layer, (1, di))],
        out_specs=full,
        out_shape=jax.ShapeDtypeStruct(y.shape, BF16),
        compiler_params=_params("arbitrary"),
        name="ssd_sample_post",
    )(y, xs, zs, dskip_x, norm_w)


def _merge_kernel(y_ref, wb_ref, sg_ref, a_ref, o_ref):
    bb = jnp.dot(y_ref[...], wb_ref[...], preferred_element_type=F32)
    o_ref[...] = (a_ref[...] + sg_ref[...].astype(F32) * bb).astype(o_ref.dtype)


def _merge(y, sg, a, layer, wb):
    m, k = y.shape
    n = wb.shape[2]
    tm, tn = _tile(m, 1024), _tile(n, 512)
    tile = pl.BlockSpec((tm, tn), lambda i, j: (i, j))
    return pl.pallas_call(
        _merge_kernel,
        grid=(m // tm, n // tn),
        in_specs=[pl.BlockSpec((tm, k), lambda i, j: (i, 0)), _lspec(layer, (k, tn), lambda i, j: (0, j)), tile, tile],
        out_specs=tile,
        out_shape=jax.ShapeDtypeStruct((m, n), BF16),
        compiler_params=_params("parallel", "arbitrary"),
        name="merge",
    )(y, wb, sg, a)


def _proj_ln_kernel(m_ref, w_ref, x_ref, g_ref, b_ref, o_ref, ob_ref, *, alpha):
    f = jnp.dot(m_ref[...], w_ref[...], preferred_element_type=F32)
    h = _layer_norm(alpha * x_ref[...] + f, g_ref[...], b_ref[...])
    o_ref[...] = h
    ob_ref[...] = h.astype(BF16)


def _proj_ln(m_in, x, layer, w, g, b, *, alpha):
    m, k = m_in.shape
    d = w.shape[2]
    tm = _tile(m, 512)
    row = lambda width: pl.BlockSpec((tm, width), lambda i: (i, 0))
    return pl.pallas_call(
        functools.partial(_proj_ln_kernel, alpha=alpha),
        grid=(m // tm,),
        in_specs=[row(k), _lspec(layer, (k, d)), row(d), _lspec(layer, (1, d)), _lspec(layer, (1, d))],
        out_specs=[row(d), row(d)],
        out_shape=[jax.ShapeDtypeStruct((m, d), F32), jax.ShapeDtypeStruct((m, d), BF16)],
        compiler_params=_params("parallel"),
        name="proj_ln",
    )(m_in, w, x, g, b)


def _attn_prompt_kernel(hb_ref, h_ref, k_ref, v_ref, wq_ref, wo_ref, g_ref, b_ref, o_ref, ob_ref, *,
                        heads, alpha):
    q = jnp.dot(hb_ref[0], wq_ref[...], preferred_element_type=F32)
    hd = q.shape[1] // heads
    scale = 1.0 / math.sqrt(hd)
    kb = k_ref[0].astype(BF16)
    vb = v_ref[0].astype(BF16)
    outs = []
    for h in range(heads):
        cols = slice(h * hd, (h + 1) * hd)
        s = lax.dot_general(q[:, cols].astype(BF16), kb[:, cols], (((1,), (1,)), ((), ())),
                            preferred_element_type=F32) * scale
        e = jnp.exp(s - jnp.max(s, axis=-1, keepdims=True))
        pr = e / jnp.sum(e, axis=-1, keepdims=True)
        outs.append(jnp.dot(pr.astype(BF16), vb[:, cols], preferred_element_type=F32))
    o = jnp.concatenate(outs, axis=1).astype(BF16)
    ca = jnp.dot(o, wo_ref[...], preferred_element_type=F32)
    hn = _layer_norm(alpha * h_ref[0] + ca, g_ref[...], b_ref[...])
    o_ref[0] = hn
    ob_ref[0] = hn.astype(BF16)


def _attn_prompt(hb, h, mk, mv, layer, wq, wo, g, b, *, heads, alpha):
    bsz, l, d = h.shape
    nm, md = mk.shape[1], mk.shape[2]
    tq = _tile(l, 512)
    row = pl.BlockSpec((1, tq, d), lambda i, j: (i, j, 0))
    mem = pl.BlockSpec((1, nm, md), lambda i, j: (i, 0, 0))
    return pl.pallas_call(
        functools.partial(_attn_prompt_kernel, heads=heads, alpha=alpha),
        grid=(bsz, l // tq),
        in_specs=[row, row, mem, mem, _lspec(layer, (d, md)), _lspec(layer, (md, d)), _lspec(layer, (1, d)),
                  _lspec(layer, (1, d))],
        out_specs=[row, row],
        out_shape=[jax.ShapeDtypeStruct((bsz, l, d), F32), jax.ShapeDtypeStruct((bsz, l, d), BF16)],
        compiler_params=_params("parallel", "parallel"),
        name="attn_prompt",
    )(hb, h, mk, mv, wq, wo, g, b)


def _attn_sample_kernel(hb_ref, h_ref, k_ref, v_ref, wq_ref, wo_ref, g_ref, b_ref, o_ref, ob_ref, att_ref, *,
                        heads, alpha):
    tb = hb_ref.shape[0]
    rows, hd = k_ref.shape[1], k_ref.shape[2]
    reps = 8 // heads
    scale = 1.0 / math.sqrt(hd)
    hb = hb_ref[...]
    qs = [jnp.dot(hb, wq_ref[:, h * hd:(h + 1) * hd], preferred_element_type=F32) for h in range(heads)]

    def over_copies(x, op):
        out = x
        for r in range(1, reps):
            out = op(out, pltpu.roll(x, r * heads, axis=0))
        return out

    for bi in range(tb):
        q8 = jnp.concatenate([q[bi:bi + 1, :] for q in qs] * reps, axis=0)
        k3 = k_ref[bi].reshape(rows // 8, 8, hd)
        v3 = v_ref[bi].reshape(rows // 8, 8, hd)
        s = jnp.sum(k3 * q8[None], axis=-1, keepdims=True) * scale
        m = over_copies(jnp.max(s, axis=0), jnp.maximum)
        e = jnp.exp(s - m[None])
        den = over_copies(jnp.sum(e, axis=0), jnp.add)
        o8 = over_copies(jnp.sum((e / den[None]) * v3, axis=0), jnp.add)
        att_ref[bi] = o8[0:heads, :]
    ca = None
    for h in range(heads):
        t = jnp.dot(att_ref[:, h, :].astype(BF16), wo_ref[h * hd:(h + 1) * hd, :], preferred_element_type=F32)
        ca = t if ca is None else ca + t
    hn = _layer_norm(alpha * h_ref[...] + ca, g_ref[...], b_ref[...])
    o_ref[...] = hn
    ob_ref[...] = hn.astype(BF16)


def _attn_sample(hb, h, cache_k, cache_v, layer, wq, wo, g, b, *, alpha):
    bs, d = h.shape
    depth, _, nm, heads, hd = cache_k.shape
    assert 8 % heads == 0
    md = heads * hd
    ck = cache_k.reshape(depth, bs, nm * heads, hd)
    cv = cache_v.reshape(depth, bs, nm * heads, hd)
    tb = _tile(bs, 8)
    row = pl.BlockSpec((tb, d), lambda i: (i, 0))
    mem = _lspec(layer, (tb, nm * heads, hd), lambda i: (i, 0, 0))
    return pl.pallas_call(
        functools.partial(_attn_sample_kernel, heads=heads, alpha=alpha),
        grid=(bs // tb,),
        in_specs=[row, row, mem, mem, _lspec(layer, (d, md)), _lspec(layer, (md, d)), _lspec(layer, (1, d)),
                  _lspec(layer, (1, d))],
        out_specs=[row, row],
        out_shape=[jax.ShapeDtypeStruct((bs, d), F32), jax.ShapeDtypeStruct((bs, d), BF16)],
        scratch_shapes=[pltpu.VMEM((tb, heads, hd), F32)],
        compiler_params=_params("parallel"),
        name="attn_sample",
    )(hb, h, ck, cv, wq, wo, g, b)


def _ffn_kernel(hb_ref, h_ref, wg_ref, wu_ref, wd_ref, g_ref, b_ref, o_ref, ob_ref, *, alpha):
    f = pl.program_id(1)
    tm, d = o_ref.shape

    @pl.when(f == 0)
    def _():
        o_ref[...] = jnp.zeros(o_ref.shape, F32)

    x = hb_ref[...]
    gate = jnp.dot(x, wg_ref[...], preferred_element_type=F32)
    up = jnp.dot(x, wu_ref[...], preferred_element_type=F32)
    act = (_silu(gate) * up).astype(BF16)
    for c in range(2):
        cols = slice(c * d // 2, (c + 1) * d // 2)
        o_ref[:, cols] += jnp.dot(act, wd_ref[:, cols], preferred_element_type=F32)

    @pl.when(f == pl.num_programs(1) - 1)
    def _():
        for r in range(4):
            rows = slice(r * tm // 4, (r + 1) * tm // 4)
            hn = _layer_norm(alpha * h_ref[rows, :] + o_ref[rows, :], g_ref[...], b_ref[...])
            o_ref[rows, :] = hn
            ob_ref[rows, :] = hn.astype(BF16)


def _ffn(hb, h, layer, wg, wu, wd, g, b, *, alpha):
    m, d = h.shape
    ff = wg.shape[2]
    tm, tf = _tile(m, 1024), 512
    assert ff % tf == 0
    row = pl.BlockSpec((tm, d), lambda i, f: (i, 0))
    row_once = pl.BlockSpec((tm, d), lambda i, f: (i, 0), pipeline_mode=pl.Buffered(1))
    return pl.pallas_call(
        functools.partial(_ffn_kernel, alpha=alpha),
        grid=(m // tm, ff // tf),
        in_specs=[row, row_once, _lspec(layer, (d, tf), lambda i, f: (0, f)),
                  _lspec(layer, (d, tf), lambda i, f: (0, f)), _lspec(layer, (tf, d), lambda i, f: (f, 0)),
                  _lspec(layer, (1, d)), _lspec(layer, (1, d))],
        out_specs=[row_once, row],
        out_shape=[jax.ShapeDtypeStruct((m, d), F32), jax.ShapeDtypeStruct((m, d), BF16)],
        compiler_params=_params("parallel", "arbitrary"),
        name="ffn",
    )(hb, h, wg, wu, wd, g, b)


def _expansion_matrix(n_in, reps, n_pad):
    r = jnp.arange(n_pad)[:, None]
    c = jnp.arange(n_in * reps)[None, :] // reps
    e = ((r == c) & (r < n_in)).astype(BF16)
    return jnp.concatenate([e, e], axis=0)


def kernel(x_prompt, x_sample, mem_prompt, state_ssd, state_conv, state_pool, cache_mem_k, cache_mem_v, w_in, w_pool, pool_scale, w_pool_br, conv_w, conv_b, dt_bias, a_log, d_skip, ssd_norm_w, w_ssd_br, w_out, ln_mix_g, ln_mix_b, w_mq, w_mk, w_mv, w_mo, ln_mem_g, ln_mem_b, w_ffn_gate, w_ffn_up, w_ffn_down, ln_ffn_g, ln_ffn_b):
    bp, seq, d = x_prompt.shape
    bs = x_sample.shape[0]
    depth = w_in.shape[0]
    _, _, heads, p, n = state_ssd.shape
    di = heads * p
    cd = conv_w.shape[2]
    groups = (cd - di) // (2 * n)
    gn = groups * n
    hpg = heads // groups
    n_mem, mem_heads, mem_hd = cache_mem_k.shape[2:]
    md = mem_heads * mem_hd
    nbuf = state_pool.shape[2]
    kw1 = state_conv.shape[2]
    alpha = (2 * depth) ** 0.25
    assert seq % SSD_CHUNK == 0 and heads <= LANES and LANES % p == 0 and hpg % (LANES // p) == 0

    s3 = d + di + cd
    s4 = s3 + heads

    w_in_t = jnp.transpose(w_in, (0, 2, 1)).astype(BF16)
    w_main = w_in_t
    w_dt = jnp.pad(w_in_t[:, s3:s4, :], ((0, 0), (0, LANES - heads), (0, 0)))
    w_gates = w_in_t[:, s4:, :]
    bf = lambda w: w.astype(BF16)
    wpool_b, wpbr_b, wsbr_b, wout_b = bf(w_pool), bf(w_pool_br), bf(w_ssd_br), bf(w_out)
    wq_b, wk_b, wv_b, wo_b = bf(w_mq), bf(w_mk), bf(w_mv), bf(w_mo)
    wg_b, wu_b, wd_b = bf(w_ffn_gate), bf(w_ffn_up), bf(w_ffn_down)
    vec = lambda v: v.astype(F32).reshape(depth, 1, v.shape[-1])
    pad_h = lambda v: jnp.pad(v.astype(F32), ((0, 0), (0, LANES - heads))).reshape(depth, 1, LANES)
    scale, cbias, nw = vec(pool_scale), vec(conv_b), vec(ssd_norm_w)
    bias_p, alog_p = pad_h(dt_bias), pad_h(a_log)
    dskip_x = vec(jnp.repeat(d_skip, p, axis=1))
    mix_g, mix_b, mem_g, mem_b, ffn_g, ffn_b = (vec(v) for v in (ln_mix_g, ln_mix_b, ln_mem_g, ln_mem_b,
                                                                   ln_ffn_g, ln_ffn_b))
    e_group = _expansion_matrix(hpg, p, LANES)
    e_all = _expansion_matrix(heads, p, LANES)

    pool_state_t = jnp.transpose(state_pool, (0, 2, 1, 3))
    conv_state_t = jnp.transpose(state_conv, (0, 2, 1, 3))

    hp, hs = x_prompt.reshape(bp * seq, d), x_sample.reshape(bs, d)
    hp_b, hs_b = hp.astype(BF16), hs.astype(BF16)
    memory_bf = mem_prompt.reshape(bp * n_mem, d).astype(BF16)

    outs = {k: [] for k in ("ssd_p", "conv_p", "pool_p", "mk", "mv")}
    new_ssd_s = new_conv_s = new_pool_s = None
    for l in range(depth):

        def in_proj(hb):
            mm = functools.partial(_mm, hb, layer=l, w_is_nk=True)
            u = mm(w_main, col0=0, n=d, name="in_u")
            zs = mm(w_main, col0=d, n=di, act="silu", out_dtype=BF16, name="in_z")
            xbc = mm(w_main, col0=d + di, n=cd, name="in_xbc")
            dtr = mm(w_dt, name="in_dt")
            sgp = mm(w_gates, col0=0, n=d, act="sigmoid", out_dtype=BF16, name="in_gp")
            sgs = mm(w_gates, col0=d, n=d, act="sigmoid", out_dtype=BF16, name="in_gs")
            return u, zs, xbc, dtr, sgp, sgs

        def tail(h, y, sgs, a_gated, attn):
            merged = _merge(y, sgs, a_gated, l, wsbr_b)
            h1, h1b = _proj_ln(merged, h, l, wout_b, mix_g, mix_b, alpha=alpha)
            h2, h2b = attn(h1, h1b)
            return _ffn(h2b, h2, l, wg_b, wu_b, wd_b, ffn_g, ffn_b, alpha=alpha)

        u, zs, xbc, dtr, sgp, sgs = in_proj(hp_b)
        u3, xbc3 = u.reshape(bp, seq, d), xbc.reshape(bp, seq, cd)
        a_gated = _pool_prompt(u3, sgp.reshape(bp, seq, d), l, wpool_b, scale, wpbr_b).reshape(bp * seq, d)
        dt, cum, cumt = _dt_prep(dtr, l, bias_p, alog_p)
        r3 = lambda a: a.reshape(bp, seq, a.shape[-1])
        y, st = _ssd_prompt(xbc3, r3(zs), r3(dt), r3(cum), r3(cumt), l, conv_w, cbias, dskip_x, nw, e_group,
                            di=di, n=n, p=p, groups=groups)
        mk_p = _mm(memory_bf, wk_b, l, name="mem_k").reshape(bp, n_mem, md)
        mv_p = _mm(memory_bf, wv_b, l, name="mem_v").reshape(bp, n_mem, md)

        def attn_p(h1, h1b):
            o, ob = _attn_prompt(h1b.reshape(bp, seq, d), h1.reshape(bp, seq, d), mk_p, mv_p, l, wq_b, wo_b,
                                 mem_g, mem_b, heads=mem_heads, alpha=alpha)
            return o.reshape(bp * seq, d), ob.reshape(bp * seq, d)

        hp, hp_b = tail(hp, y.reshape(bp * seq, di), sgs, a_gated, attn_p)
        outs["ssd_p"].append(st)
        outs["conv_p"].append(xbc3[:, seq - kw1:, :])
        outs["pool_p"].append(u3[:, seq - nbuf:, :])
        outs["mk"].append(mk_p.reshape(bp, n_mem, mem_heads, mem_hd))
        outs["mv"].append(mv_p.reshape(bp, n_mem, mem_heads, mem_hd))

        u, zs, xbc, dtr, sgp, sgs = in_proj(hs_b)
        a_gated, new_pool_s = _pool_sample(u, sgp, pool_state_t, l, new_pool_s, wpool_b, scale, wpbr_b)
        xs, xdt, dae, bm, cm, new_conv_s = _ssd_sample_pre(xbc, dtr, conv_state_t, l, new_conv_s, conv_w, cbias,
                                                           bias_p, alog_p, e_all, di=di, gn=gn)
        new_ssd_s, ys = _ssd_sample_state(state_ssd, l, new_ssd_s, xdt, dae, bm, cm, groups=groups)
        y = _ssd_sample_post(ys, xs, zs, l, dskip_x, nw, groups=groups)

        def attn_s(h1, h1b):
            return _attn_sample(h1b, h1, cache_mem_k, cache_mem_v, l, wq_b, wo_b, mem_g, mem_b, alpha=alpha)

        hs, hs_b = tail(hs, y, sgs, a_gated, attn_s)

    stack = lambda k: jnp.stack(outs[k])
    return (hp.reshape(bp, seq, d), hs.reshape(bs, 1, d), stack("ssd_p"), stack("conv_p"), stack("pool_p"),
            stack("mk"), stack("mv"), new_ssd_s, jnp.transpose(new_conv_s, (0, 2, 1, 3)),
            jnp.transpose(new_pool_s, (0, 2, 1, 3)))
```

```python
import functools
import math

import jax
import jax.numpy as jnp
from jax import lax
from jax.experimental import pallas as pl
from jax.experimental.pallas import tpu as pltpu

F32 = jnp.float32
BF16 = jnp.bfloat16

POOL_WINDOWS = (2, 4, 8, 16)
PAST_LEN = 16384
SSD_CHUNK = 128
LN_EPS = 1e-5
RMS_EPS = 1e-5
LANES = 128
VMEM_LIMIT_BYTES = 56 * 1024 * 1024


def _params(*sem):
    return pltpu.CompilerParams(dimension_semantics=sem, vmem_limit_bytes=VMEM_LIMIT_BYTES)


def _silu(x):
    return x * jax.nn.sigmoid(x)


def _softplus(x):
    return jnp.maximum(x, 0.0) + jnp.log(1.0 + jnp.exp(-jnp.abs(x)))


def _layer_norm(x, g, b):
    mu = jnp.mean(x, axis=-1, keepdims=True)
    xc = x - mu
    var = jnp.mean(xc * xc, axis=-1, keepdims=True)
    return xc * lax.rsqrt(var + LN_EPS) * g + b


def _split2(x):
    hi = x.astype(BF16)
    mid = (x - hi.astype(F32)).astype(BF16)
    return hi, mid


def _tile(n, pref):
    t = min(n, pref)
    while n % t:
        t //= 2
    return t


def _lspec(layer, block, index_map=None, **kwargs):
    nd = len(block)
    if index_map is None:
        index_map = lambda *g: (0,) * nd
    return pl.BlockSpec((None,) + tuple(block), lambda *g: (layer,) + tuple(index_map(*g)), **kwargs)


def _mm_kernel(x_ref, w_ref, o_ref, *, act, w_is_nk):
    if w_is_nk:
        acc = lax.dot_general(x_ref[...], w_ref[0].astype(BF16), (((1,), (1,)), ((), ())),
                              preferred_element_type=F32)
    else:
        acc = jnp.dot(x_ref[...], w_ref[...], preferred_element_type=F32)
    if act == "silu":
        acc = _silu(acc)
    elif act == "sigmoid":
        acc = jax.nn.sigmoid(acc)
    o_ref[...] = acc.astype(o_ref.dtype)


def _mm(x, w, layer, *, col0=0, n=None, act=None, out_dtype=F32, w_is_nk=False, name):
    m, k = x.shape
    n = w.shape[1 if w_is_nk else 2] if n is None else n
    if w_is_nk:
        tm, tn = _tile(m, 2048), _tile(n, 512)
        assert col0 % 16 == 0
        w_spec = pl.BlockSpec((pl.Element(1), pl.Element(tn), pl.Element(k)),
                              lambda i, j: (layer, pl.multiple_of(col0 + j * tn, 16), 0))
    else:
        tm, tn = _tile(m, 1024), _tile(n, 1024)
        assert col0 % tn == 0
        w_spec = _lspec(layer, (k, tn), lambda i, j: (0, col0 // tn + j))
    return pl.pallas_call(
        functools.partial(_mm_kernel, act=act, w_is_nk=w_is_nk),
        grid=(m // tm, n // tn),
        in_specs=[pl.BlockSpec((tm, k), lambda i, j: (i, 0)), w_spec],
        out_specs=pl.BlockSpec((tm, tn), lambda i, j: (i, j)),
        out_shape=jax.ShapeDtypeStruct((m, n), out_dtype),
        compiler_params=_params("parallel", "arbitrary"),
        name=name,
    )(x, w)


def _pool_project(pooled_ref, wg_ref, scale_ref, wbr_ref, sg, n_groups, gc):
    for g in range(n_groups):
        cols = slice(g * gc, (g + 1) * gc)
        pg = jnp.dot(pooled_ref[:, cols], wg_ref[g], preferred_element_type=F32) * scale_ref[:, cols]
        pooled_ref[:, cols] = pg.astype(BF16)
    a = jnp.dot(pooled_ref[...], wbr_ref[...], preferred_element_type=F32)
    return a * sg.astype(F32)


def _pool_prompt_kernel(u_ref, wg_ref, scale_ref, wbr_ref, sg_ref, o_ref, carry_ref, pooled_ref, *, tl, gc):
    j = pl.program_id(1)

    @pl.when(j == 0)
    def _():
        carry_ref[...] = jnp.zeros(carry_ref.shape, F32)

    nt = tl // 8
    sub = lax.broadcasted_iota(jnp.int32, (nt, 8, gc), 1)
    pos = j * tl + lax.broadcasted_iota(jnp.int32, (tl, 1), 0)
    for g, w in enumerate(POOL_WINDOWS):
        cols = slice(g * gc, (g + 1) * gc)
        ug = u_ref[0, :, cols]
        acc, s, level = ug, 1, 0
        while s < w:
            a3 = acc.reshape(nt, 8, gc)
            before = jnp.concatenate([carry_ref[level, :, cols][None], a3[:nt - 1]], axis=0)
            carry_ref[level, :, cols] = acc[tl - 8:tl, :]
            shifted = before if s == 8 else pltpu.roll(jnp.where(sub >= 8 - s, before, a3), s, axis=1)
            acc = acc + shifted.reshape(tl, gc)
            s, level = 2 * s, level + 1
        cnt = jnp.minimum(w, pos + 1).astype(F32)
        pooled_ref[:, cols] = (acc / cnt - ug).astype(BF16)
    o_ref[0] = _pool_project(pooled_ref, wg_ref, scale_ref, wbr_ref, sg_ref[0], len(POOL_WINDOWS), gc)


def _pool_prompt(u, sg, layer, wg, scale, wbr):
    b, l, d = u.shape
    _, ng, gc, _ = wg.shape
    tl = _tile(l, 256)
    assert all(w in (1, 2, 4, 8, 16) for w in POOL_WINDOWS) and tl % 8 == 0
    levels = max(POOL_WINDOWS).bit_length() - 1
    tok = pl.BlockSpec((1, tl, d), lambda i, j: (i, j, 0))
    return pl.pallas_call(
        functools.partial(_pool_prompt_kernel, tl=tl, gc=gc),
        grid=(b, l // tl),
        in_specs=[tok, _lspec(layer, (ng, gc, gc)), _lspec(layer, (1, d)), _lspec(layer, (d, d)), tok],
        out_specs=tok,
        out_shape=jax.ShapeDtypeStruct((b, l, d), F32),
        scratch_shapes=[pltpu.VMEM((levels, 8, d), F32), pltpu.VMEM((tl, d), BF16)],
        compiler_params=_params("parallel", "arbitrary"),
        name="pool_prompt",
    )(u, wg, scale, wbr, sg)


def _layer_state_call(kern, *, layer, state, batch_axis, stacked_so_far, grid, tb, other_operands, other_in_specs,
                      other_out_specs, other_out_shapes, scratch_shapes=(), name):
    blk = tuple(tb if a == batch_axis else s for a, s in enumerate(state.shape))[1:]
    spec = _lspec(layer, blk, lambda i: tuple(i if a == batch_axis else 0 for a in range(1, state.ndim)))
    if stacked_so_far is None:
        stacked_so_far = jnp.zeros(state.shape, state.dtype)
    operands = list(other_operands) + [state, stacked_so_far]
    in_specs = list(other_in_specs) + [spec, pl.BlockSpec(memory_space=pl.ANY)]
    return pl.pallas_call(
        kern,
        grid=grid,
        in_specs=in_specs,
        out_specs=list(other_out_specs) + [spec],
        out_shape=list(other_out_shapes) + [jax.ShapeDtypeStruct(state.shape, state.dtype)],
        input_output_aliases={len(operands) - 1: len(other_out_shapes)},
        scratch_shapes=list(scratch_shapes),
        compiler_params=_params("parallel"),
        name=name,
    )(*operands)


def _pool_sample_kernel(u_ref, wg_ref, scale_ref, wbr_ref, sg_ref, buf_ref, stack_hbm_ref,
                        o_ref, nbuf_ref, pooled_ref, *, gc):
    del stack_hbm_ref
    nbuf = buf_ref.shape[0]
    u = u_ref[...]
    for r in range(nbuf - 1):
        nbuf_ref[r] = buf_ref[r + 1]
    nbuf_ref[nbuf - 1] = u
    for g, w in enumerate(POOL_WINDOWS):
        cols = slice(g * gc, (g + 1) * gc)
        ug = u[:, cols]
        s = ug
        for k in range(1, w):
            s = s + buf_ref[nbuf - k, :, cols]
        cnt = float(min(w, PAST_LEN + 1))
        pooled_ref[:, cols] = (s / cnt - ug).astype(BF16)
    o_ref[...] = _pool_project(pooled_ref, wg_ref, scale_ref, wbr_ref, sg_ref[...], len(POOL_WINDOWS), gc)


def _pool_sample(u, sg, pool_state_t, layer, stacked_so_far, wg, scale, wbr):
    bs, d = u.shape
    _, ng, gc, _ = wg.shape
    tb = _tile(bs, 32)
    row = pl.BlockSpec((tb, d), lambda i: (i, 0))
    return _layer_state_call(
        functools.partial(_pool_sample_kernel, gc=gc),
        layer=layer, state=pool_state_t, batch_axis=2, stacked_so_far=stacked_so_far, grid=(bs // tb,), tb=tb,
        other_operands=[u, wg, scale, wbr, sg],
        other_in_specs=[row, _lspec(layer, (ng, gc, gc)), _lspec(layer, (1, d)), _lspec(layer, (d, d)), row],
        other_out_specs=[row], other_out_shapes=[jax.ShapeDtypeStruct((bs, d), F32)],
        scratch_shapes=[pltpu.VMEM((tb, d), BF16)], name="pool_sample")


def _dt_prep_kernel(dtr_ref, bias_ref, alog_ref, dt_ref, cum_ref, cumt_ref):
    c = SSD_CHUNK
    dt_ref[...] = _softplus(dtr_ref[...] + bias_ref[...])
    a = -jnp.exp(alog_ref[...])
    row = lax.broadcasted_iota(jnp.int32, (c, dtr_ref.shape[1]), 0)
    for ch in range(dtr_ref.shape[0] // c):
        rows = slice(ch * c, (ch + 1) * c)
        cum = dt_ref[rows, :] * a
        k = 1
        while k < c:
            cum = cum + jnp.where(row >= k, pltpu.roll(cum, k, axis=0), 0.0)
            k *= 2
        cum_ref[rows, :] = cum
        cumt_ref[rows, :] = cum.T


def _dt_prep(dtr, layer, bias, alog):
    t, hp = dtr.shape
    c = SSD_CHUNK * _tile(t // SSD_CHUNK, 8)
    spec = pl.BlockSpec((c, hp), lambda i: (i, 0))
    return pl.pallas_call(
        _dt_prep_kernel,
        grid=(t // c,),
        in_specs=[spec, _lspec(layer, (1, hp)), _lspec(layer, (1, hp))],
        out_specs=[spec, spec, spec],
        out_shape=[jax.ShapeDtypeStruct((t, hp), F32)] * 3,
        compiler_params=_params("parallel"),
        name="dt_prep",
    )(dtr, bias, alog)


def _conv_silu(x, carry_ref, w_ref, b_ref):
    rows, wd = x.shape
    kw = w_ref.shape[0]
    nt = rows // 8
    last_row = lax.broadcasted_iota(jnp.int32, (nt, 8, wd), 1) == 7
    t = x * w_ref[0:1, :]
    for k in range(1, kw):
        t3 = t.reshape(nt, 8, wd)
        before = jnp.concatenate([carry_ref[k - 1][None], t3[:nt - 1]], axis=0)
        carry_ref[k - 1] = t[rows - 8:rows, :]
        shifted = pltpu.roll(jnp.where(last_row, before, t3), 1, axis=1).reshape(rows, wd)
        t = x * w_ref[k:k + 1, :] + shifted
    return _silu(t + b_ref[...])


def _ssd_prompt_kernel(xs_ref, bm_ref, cm_ref, zs_ref, dt_ref, cum_ref, cumt_ref,
                       wx_ref, wb_ref, wc_ref, bx_ref, bb_ref, bc_ref, dskip_ref, nw_ref, e_ref,
                       y_ref, st_ref, stt_ref, hx_ref, hb_ref, hc_ref, *, hpg, p, gps):
    c = pl.program_id(2)
    nc = pl.num_programs(2)
    rows = xs_ref.shape[1]
    n = bm_ref.shape[2] // gps
    gw = hpg * p

    @pl.when(c == 0)
    def _():
        for r in (stt_ref, hx_ref, hb_ref, hc_ref):
            r[...] = jnp.zeros(r.shape, F32)

    x_all = _conv_silu(xs_ref[0], hx_ref, wx_ref, bx_ref)
    b_all = _conv_silu(bm_ref[0], hb_ref, wb_ref, bb_ref).astype(BF16)
    c_all = _conv_silu(cm_ref[0], hc_ref, wc_ref, bc_ref).astype(BF16)
    causal = lax.broadcasted_iota(jnp.int32, (rows, rows), 0) >= lax.broadcasted_iota(jnp.int32, (rows, rows), 1)
    lane = lax.broadcasted_iota(jnp.int32, (rows, LANES), 1)
    heads_per_tile = LANES // p
    lane_masks = [jnp.where(lane // p == r, 1.0, 0.0).astype(BF16) for r in range(heads_per_tile)]

    for gi in range(gps):
        g = pl.program_id(1) * gps + gi
        gcols = slice(gi * gw, (gi + 1) * gw)
        x = x_all[:, gcols]
        bmat = b_all[:, gi * n:(gi + 1) * n]
        cmat = c_all[:, gi * n:(gi + 1) * n]
        shift = (LANES - g * hpg) % LANES
        dtg = pltpu.roll(dt_ref[0], shift, axis=1)
        cumg = pltpu.roll(cum_ref[0], shift, axis=1)
        cumtg = cumt_ref[0, pl.ds(pl.multiple_of(g * hpg, hpg), hpg), :]
        last = cumg[rows - 1:rows, :]
        stack = jnp.concatenate([dtg, jnp.exp(cumg), jnp.exp(last - cumg)], axis=0)
        hi, mid = _split2(stack)
        expanded = jnp.dot(jnp.concatenate([hi, mid], axis=1), e_ref[...], preferred_element_type=F32)
        dt_x, ecum_x, ws_x = expanded[0:rows], expanded[rows:2 * rows], expanded[2 * rows:3 * rows]

        xdt = x * dt_x
        xdt_b = xdt.astype(BF16)
        xdtw_b = (xdt * ws_x).astype(BF16)
        cb = lax.dot_general(cmat, bmat, (((1,), (1,)), ((), ())), preferred_element_type=F32)
        st_old = stt_ref[:, gcols]
        y = jnp.dot(cmat, st_old.astype(BF16), preferred_element_type=F32) * ecum_x
        stt_ref[:, gcols] = st_old * ecum_x[rows - 1:rows, :] + lax.dot_general(
            bmat, xdtw_b, (((0,), (0,)), ((), ())), preferred_element_type=F32)

        cum2 = cumg * math.log2(math.e)
        cumt2 = cumtg * math.log2(math.e)
        diag = []
        for q in range(hpg // heads_per_tile):
            xq = xdt_b[:, q * LANES:(q + 1) * LANES]
            ms, xr = [], []
            for r in range(heads_per_tile):
                h = q * heads_per_tile + r
                dec = jnp.exp2(cum2[:, h:h + 1] - cumt2[h:h + 1, :])
                ms.append(jnp.where(causal, cb * dec, 0.0).astype(BF16))
                xr.append(xq * lane_masks[r])
            diag.append(jnp.dot(jnp.concatenate(ms, axis=1), jnp.concatenate(xr, axis=0),
                                preferred_element_type=F32))
        y = y + jnp.concatenate(diag, axis=1) + dskip_ref[:, gcols] * x
        y = y * zs_ref[0, :, gcols].astype(F32)
        y = y * lax.rsqrt(jnp.mean(y * y, axis=-1, keepdims=True) + RMS_EPS) * nw_ref[:, gcols]
        y_ref[0, :, gcols] = y.astype(y_ref.dtype)

    @pl.when(c == nc - 1)
    def _():
        st_ref[0] = stt_ref[...].T.reshape(st_ref.shape[1:])


def _ssd_prompt(xbc, zs, dt, cum, cumt, layer, conv_w, conv_b, dskip_x, norm_w, e_mat, *, di, n, p, groups):
    b, l, cd = xbc.shape
    c = SSD_CHUNK
    heads = di // p
    hpg = heads // groups
    gps = 4 if groups % 4 == 0 else 1
    gw = hpg * p * gps
    nn = n * gps
    hp = dt.shape[-1]
    xoff, boff, coff = 0, di // nn, (di + groups * n) // nn

    def seg(width, off_blocks):
        return pl.BlockSpec((1, c, width), lambda i, g, k: (i, k, off_blocks + g))

    def wseg(rows, width, off_blocks):
        return _lspec(layer, (rows, width), lambda i, g, k: (0, off_blocks + g))

    small = pl.BlockSpec((1, c, hp), lambda i, g, k: (i, k, 0))
    kw = conv_w.shape[1]
    return pl.pallas_call(
        functools.partial(_ssd_prompt_kernel, hpg=hpg, p=p, gps=gps),
        grid=(b, groups // gps, l // c),
        in_specs=[
            seg(gw, xoff), seg(nn, boff), seg(nn, coff), seg(gw, 0), small, small, small,
            wseg(kw, gw, xoff), wseg(kw, nn, boff), wseg(kw, nn, coff),
            wseg(1, gw, xoff), wseg(1, nn, boff), wseg(1, nn, coff),
            wseg(1, gw, 0), wseg(1, gw, 0),
            pl.BlockSpec(e_mat.shape, lambda i, g, k: (0, 0)),
        ],
        out_specs=[
            pl.BlockSpec((1, c, gw), lambda i, g, k: (i, k, g)),
            pl.BlockSpec((1, hpg * gps, p, n), lambda i, g, k: (i, g, 0, 0)),
        ],
        out_shape=[jax.ShapeDtypeStruct((b, l, di), BF16), jax.ShapeDtypeStruct((b, heads, p, n), F32)],
        scratch_shapes=[
            pltpu.VMEM((n, gw), F32),
            pltpu.VMEM((kw - 1, 8, gw), F32), pltpu.VMEM((kw - 1, 8, nn), F32), pltpu.VMEM((kw - 1, 8, nn), F32),
        ],
        compiler_params=_params("parallel", "parallel", "arbitrary"),
        name="ssd_prompt",
    )(xbc, xbc, xbc, zs, dt, cum, cumt, conv_w, conv_w, conv_w, conv_b, conv_b, conv_b, dskip_x, norm_w, e_mat)


def _ssd_sample_pre_kernel(xbc_ref, w_ref, b_ref, dtr_ref, bias_ref, alog_ref, e_ref, cbuf_ref, stack_hbm_ref,
                           xs_ref, xdt_ref, dae_ref, bm_ref, cm_ref, ncbuf_ref, *, di, gn):
    del stack_hbm_ref
    kw = w_ref.shape[0]
    xbc = xbc_ref[...]
    acc = b_ref[...]
    for k in range(kw - 1):
        acc = acc + cbuf_ref[k] * w_ref[k:k + 1, :]
    acc = _silu(acc + xbc * w_ref[kw - 1:kw, :])
    for k in range(kw - 2):
        ncbuf_ref[k] = cbuf_ref[k + 1]
    ncbuf_ref[kw - 2] = xbc
    xs = acc[:, 0:di]
    dt = _softplus(dtr_ref[...] + bias_ref[...])
    dae_ref[...] = jnp.exp(dt * (-jnp.exp(alog_ref[...])))
    hi, mid = _split2(dt)
    dt_x = jnp.dot(jnp.concatenate([hi, mid], axis=1), e_ref[...], preferred_element_type=F32)
    xs_ref[...] = xs
    xdt_ref[...] = xs * dt_x
    bm_ref[...] = acc[:, di:di + gn]
    cm_ref[...] = acc[:, di + gn:di + 2 * gn]


def _ssd_sample_pre(xbc, dtr, conv_state_t, layer, stacked_so_far, conv_w, conv_b, bias, alog, e_mat, *, di, gn):
    bs, cd = xbc.shape
    hp = dtr.shape[1]
    kw = conv_w.shape[1]
    tb = _tile(bs, 32)
    row = lambda w: pl.BlockSpec((tb, w), lambda i: (i, 0))
    return _layer_state_call(
        functools.partial(_ssd_sample_pre_kernel, di=di, gn=gn),
        layer=layer, state=conv_state_t, batch_axis=2, stacked_so_far=stacked_so_far, grid=(bs // tb,), tb=tb,
        other_operands=[xbc, conv_w, conv_b, dtr, bias, alog, e_mat],
        other_in_specs=[row(cd), _lspec(layer, (kw, cd)), _lspec(layer, (1, cd)), row(hp), _lspec(layer, (1, hp)),
                        _lspec(layer, (1, hp)), pl.BlockSpec(e_mat.shape, lambda i: (0, 0))],
        other_out_specs=[row(di), row(di), row(hp), row(gn), row(gn)],
        other_out_shapes=[jax.ShapeDtypeStruct((bs, di), F32)] * 2 + [jax.ShapeDtypeStruct((bs, hp), F32)]
        + [jax.ShapeDtypeStruct((bs, gn), F32)] * 2,
        name="ssd_sample_pre")


def _ssd_sample_state_kernel(dae_ref, xdt_ref, bm_ref, cm_ref, s_ref, stack_hbm_ref, y_ref, so_ref, *, groups):
    del stack_hbm_ref
    tb, heads, p, n = s_ref.shape
    di = heads * p
    gw = di // groups
    grow = lax.broadcasted_iota(jnp.int32, (groups, di), 0)
    gcol = lax.broadcasted_iota(jnp.int32, (groups, di), 1) // gw
    own = grow == gcol
    fill = (-groups) % 16
    for b in range(tb):
        seq = pl.program_id(0) * tb + b
        xg = jnp.where(own, jnp.broadcast_to(xdt_ref[b], (groups, di)), 0.0)
        bmat = bm_ref[b]
        if fill:
            xg = jnp.concatenate([xg, jnp.zeros((fill, di), F32)], axis=0)
            bmat = jnp.concatenate([bmat, jnp.zeros((fill, n), F32)], axis=0)
        upd = lax.dot_general(xg.astype(BF16), bmat.astype(BF16), (((0,), (0,)), ((), ())),
                              preferred_element_type=F32)
        for h in range(heads):
            so_ref[b, h] = s_ref[b, h] * dae_ref[seq, h] + upd[h * p:(h + 1) * p, :]
        s_new = so_ref[b].reshape(di, n).astype(BF16)
        yg = lax.dot_general(cm_ref[b].astype(BF16), s_new, (((1,), (1,)), ((), ())),
                             preferred_element_type=F32)
        y_ref[b] = jnp.sum(jnp.where(own, yg, 0.0), axis=0, keepdims=True)


def _ssd_sample_state(ssd_state, layer, stacked_so_far, xdt, dae, bm, cm, *, groups):
    _, bs, heads, p, n = ssd_state.shape
    di = heads * p
    tb = _tile(bs, 2)
    xdt3 = xdt.reshape(bs, 1, di)
    bm3, cm3 = bm.reshape(bs, groups, n), cm.reshape(bs, groups, n)
    rspec = pl.BlockSpec((tb, 1, di), lambda i: (i, 0, 0))
    gspec = pl.BlockSpec((tb, groups, n), lambda i: (i, 0, 0))
    y, stacked = _layer_state_call(
        functools.partial(_ssd_sample_state_kernel, groups=groups),
        layer=layer, state=ssd_state, batch_axis=1, stacked_so_far=stacked_so_far, grid=(bs // tb,), tb=tb,
        other_operands=[dae, xdt3, bm3, cm3],
        other_in_specs=[pl.BlockSpec(memory_space=pltpu.SMEM), rspec, gspec, gspec],
        other_out_specs=[rspec], other_out_shapes=[jax.ShapeDtypeStruct((bs, 1, di), F32)],
        name="ssd_sample_state")
    return stacked, y.reshape(bs, di)


def _ssd_sample_post_kernel(y_ref, xs_ref, zs_ref, dskip_ref, nw_ref, o_ref, *, groups):
    y = (y_ref[...] + dskip_ref[...] * xs_ref[...]) * zs_ref[...].astype(F32)
    gw = y.shape[1] // groups
    for g in range(groups):
        cols = slice(g * gw, (g + 1) * gw)
        yg = y[:, cols]
        yg = yg * lax.rsqrt(jnp.mean(yg * yg, axis=-1, keepdims=True) + RMS_EPS) * nw_ref[:, cols]
        o_ref[:, cols] = yg.astype(o_ref.dtype)


def _ssd_sample_post(y, xs, zs, layer, dskip_x, norm_w, *, groups):
    bs, di = y.shape
    full = pl.BlockSpec((bs, di), lambda i: (0, 0))
    return pl.pallas_call(
        functools.partial(_ssd_sample_post_kernel, groups=groups),
        grid=(1,),
        in_specs=[full, full, full, _lspec(layer, (1, di)), _lspec(layer, (1, di))],
        out_specs=full,
        out_shape=jax.ShapeDtypeStruct(y.shape, BF16),
        compiler_params=_params("arbitrary"),
        name="ssd_sample_post",
    )(y, xs, zs, dskip_x, norm_w)


def _merge_proj_ln_kernel(y_ref, sg_ref, a_ref, x_ref, wb_ref, wo_ref, g_ref, b_ref, o_ref, ob_ref, *, alpha):
    tm = y_ref.shape[0]
    halves = 2 if tm % 16 == 0 else 1
    for r in range(halves):
        rows = slice(r * tm // halves, (r + 1) * tm // halves)
        bb = jnp.dot(y_ref[rows, :], wb_ref[...], preferred_element_type=F32)
        merged = (a_ref[rows, :] + sg_ref[rows, :].astype(F32) * bb).astype(BF16)
        f = jnp.dot(merged, wo_ref[...], preferred_element_type=F32)
        h = _layer_norm(alpha * x_ref[rows, :] + f, g_ref[...], b_ref[...])
        o_ref[rows, :] = h
        ob_ref[rows, :] = h.astype(BF16)


def _merge_proj_ln(y, sg, a, x, layer, wb, wo, g, b, *, alpha):
    m, k = y.shape
    d = wo.shape[2]
    tm = _tile(m, 256)
    row = lambda width: pl.BlockSpec((tm, width), lambda i: (i, 0))
    once = pl.Buffered(1)
    return pl.pallas_call(
        functools.partial(_merge_proj_ln_kernel, alpha=alpha),
        grid=(m // tm,),
        in_specs=[row(k), row(d), row(d), row(d), _lspec(layer, (k, d), pipeline_mode=once),
                  _lspec(layer, (d, d), pipeline_mode=once), _lspec(layer, (1, d)), _lspec(layer, (1, d))],
        out_specs=[row(d), row(d)],
        out_shape=[jax.ShapeDtypeStruct((m, d), F32), jax.ShapeDtypeStruct((m, d), BF16)],
        compiler_params=_params("parallel"),
        name="merge_proj_ln",
    )(y, sg, a, x, wb, wo, g, b)


def _attn_prompt_kernel(hb_ref, h_ref, k_ref, v_ref, wq_ref, wo_ref, g_ref, b_ref, o_ref, ob_ref, *,
                        heads, alpha):
    q = jnp.dot(hb_ref[0], wq_ref[...], preferred_element_type=F32)
    hd = q.shape[1] // heads
    scale = 1.0 / math.sqrt(hd)
    kb = k_ref[0].astype(BF16)
    vb = v_ref[0].astype(BF16)
    outs = []
    for h in range(heads):
        cols = slice(h * hd, (h + 1) * hd)
        s = lax.dot_general(q[:, cols].astype(BF16), kb[:, cols], (((1,), (1,)), ((), ())),
                            preferred_element_type=F32) * scale
        e = jnp.exp(s - jnp.max(s, axis=-1, keepdims=True))
        pr = e / jnp.sum(e, axis=-1, keepdims=True)
        outs.append(jnp.dot(pr.astype(BF16), vb[:, cols], preferred_element_type=F32))
    o = jnp.concatenate(outs, axis=1).astype(BF16)
    ca = jnp.dot(o, wo_ref[...], preferred_element_type=F32)
    hn = _layer_norm(alpha * h_ref[0] + ca, g_ref[...], b_ref[...])
    o_ref[0] = hn
    ob_ref[0] = hn.astype(BF16)


def _attn_prompt(hb, h, mk, mv, layer, wq, wo, g, b, *, heads, alpha):
    bsz, l, d = h.shape
    nm, md = mk.shape[1], mk.shape[2]
    tq = _tile(l, 512)
    row = pl.BlockSpec((1, tq, d), lambda i, j: (i, j, 0))
    mem = pl.BlockSpec((1, nm, md), lambda i, j: (i, 0, 0))
    return pl.pallas_call(
        functools.partial(_attn_prompt_kernel, heads=heads, alpha=alpha),
        grid=(bsz, l // tq),
        in_specs=[row, row, mem, mem, _lspec(layer, (d, md)), _lspec(layer, (md, d)), _lspec(layer, (1, d)),
                  _lspec(layer, (1, d))],
        out_specs=[row, row],
        out_shape=[jax.ShapeDtypeStruct((bsz, l, d), F32), jax.ShapeDtypeStruct((bsz, l, d), BF16)],
        compiler_params=_params("parallel", "parallel"),
        name="attn_prompt",
    )(hb, h, mk, mv, wq, wo, g, b)


def _attn_sample_kernel(hb_ref, h_ref, k_ref, v_ref, wq_ref, wo_ref, g_ref, b_ref, o_ref, ob_ref, att_ref, *,
                        heads, alpha):
    tb = hb_ref.shape[0]
    rows, hd = k_ref.shape[1], k_ref.shape[2]
    reps = 8 // heads
    scale = 1.0 / math.sqrt(hd)
    hb = hb_ref[...]
    qs = [jnp.dot(hb, wq_ref[:, h * hd:(h + 1) * hd], preferred_element_type=F32) for h in range(heads)]

    def over_copies(x, op):
        out = x
        for r in range(1, reps):
            out = op(out, pltpu.roll(x, r * heads, axis=0))
        return out

    for bi in range(tb):
        q8 = jnp.concatenate([q[bi:bi + 1, :] for q in qs] * reps, axis=0)
        k3 = k_ref[bi].reshape(rows // 8, 8, hd)
        v3 = v_ref[bi].reshape(rows // 8, 8, hd)
        s = jnp.sum(k3 * q8[None], axis=-1, keepdims=True) * scale
        m = over_copies(jnp.max(s, axis=0), jnp.maximum)
        e = jnp.exp(s - m[None])
        den = over_copies(jnp.sum(e, axis=0), jnp.add)
        o8 = over_copies(jnp.sum((e / den[None]) * v3, axis=0), jnp.add)
        att_ref[bi] = o8[0:heads, :]
    ca = None
    for h in range(heads):
        t = jnp.dot(att_ref[:, h, :].astype(BF16), wo_ref[h * hd:(h + 1) * hd, :], preferred_element_type=F32)
        ca = t if ca is None else ca + t
    hn = _layer_norm(alpha * h_ref[...] + ca, g_ref[...], b_ref[...])
    o_ref[...] = hn
    ob_ref[...] = hn.astype(BF16)


def _attn_sample(hb, h, cache_k, cache_v, layer, wq, wo, g, b, *, alpha):
    bs, d = h.shape
    depth, _, nm, heads, hd = cache_k.shape
    assert 8 % heads == 0
    md = heads * hd
    ck = cache_k.reshape(depth, bs, nm * heads, hd)
    cv = cache_v.reshape(depth, bs, nm * heads, hd)
    tb = _tile(bs, 8)
    row = pl.BlockSpec((tb, d), lambda i: (i, 0))
    mem = _lspec(layer, (tb, nm * heads, hd), lambda i: (i, 0, 0))
    return pl.pallas_call(
        functools.partial(_attn_sample_kernel, heads=heads, alpha=alpha),
        grid=(bs // tb,),
        in_specs=[row, row, mem, mem, _lspec(layer, (d, md)), _lspec(layer, (md, d)), _lspec(layer, (1, d)),
                  _lspec(layer, (1, d))],
        out_specs=[row, row],
        out_shape=[jax.ShapeDtypeStruct((bs, d), F32), jax.ShapeDtypeStruct((bs, d), BF16)],
        scratch_shapes=[pltpu.VMEM((tb, heads, hd), F32)],
        compiler_params=_params("parallel"),
        name="attn_sample",
    )(hb, h, ck, cv, wq, wo, g, b)


def _ffn_kernel(hb_ref, h_ref, wg_ref, wu_ref, wd_ref, g_ref, b_ref, o_ref, ob_ref, *, alpha):
    f = pl.program_id(1)
    tm, d = o_ref.shape

    @pl.when(f == 0)
    def _():
        o_ref[...] = jnp.zeros(o_ref.shape, F32)

    x = hb_ref[...]
    gate = jnp.dot(x, wg_ref[...], preferred_element_type=F32)
    up = jnp.dot(x, wu_ref[...], preferred_element_type=F32)
    act = (_silu(gate) * up).astype(BF16)
    for c in range(2):
        cols = slice(c * d // 2, (c + 1) * d // 2)
        o_ref[:, cols] += jnp.dot(act, wd_ref[:, cols], preferred_element_type=F32)

    @pl.when(f == pl.num_programs(1) - 1)
    def _():
        for r in range(4):
            rows = slice(r * tm // 4, (r + 1) * tm // 4)
            hn = _layer_norm(alpha * h_ref[rows, :] + o_ref[rows, :], g_ref[...], b_ref[...])
            o_ref[rows, :] = hn
            ob_ref[rows, :] = hn.astype(BF16)


def _ffn(hb, h, layer, wg, wu, wd, g, b, *, alpha):
    m, d = h.shape
    ff = wg.shape[2]
    tm, tf = _tile(m, 1024), 512
    assert ff % tf == 0
    row = pl.BlockSpec((tm, d), lambda i, f: (i, 0))
    row_once = pl.BlockSpec((tm, d), lambda i, f: (i, 0), pipeline_mode=pl.Buffered(1))
    return pl.pallas_call(
        functools.partial(_ffn_kernel, alpha=alpha),
        grid=(m // tm, ff // tf),
        in_specs=[row, row_once, _lspec(layer, (d, tf), lambda i, f: (0, f)),
                  _lspec(layer, (d, tf), lambda i, f: (0, f)), _lspec(layer, (tf, d), lambda i, f: (f, 0)),
                  _lspec(layer, (1, d)), _lspec(layer, (1, d))],
        out_specs=[row_once, row],
        out_shape=[jax.ShapeDtypeStruct((m, d), F32), jax.ShapeDtypeStruct((m, d), BF16)],
        compiler_params=_params("parallel", "arbitrary"),
        name="ffn",
    )(hb, h, wg, wu, wd, g, b)


def _expansion_matrix(n_in, reps, n_pad):
    r = jnp.arange(n_pad)[:, None]
    c = jnp.arange(n_in * reps)[None, :] // reps
    e = ((r == c) & (r < n_in)).astype(BF16)
    return jnp.concatenate([e, e], axis=0)


def kernel(x_prompt, x_sample, mem_prompt, state_ssd, state_conv, state_pool, cache_mem_k, cache_mem_v, w_in, w_pool, pool_scale, w_pool_br, conv_w, conv_b, dt_bias, a_log, d_skip, ssd_norm_w, w_ssd_br, w_out, ln_mix_g, ln_mix_b, w_mq, w_mk, w_mv, w_mo, ln_mem_g, ln_mem_b, w_ffn_gate, w_ffn_up, w_ffn_down, ln_ffn_g, ln_ffn_b):
    bp, seq, d = x_prompt.shape
    bs = x_sample.shape[0]
    depth = w_in.shape[0]
    _, _, heads, p, n = state_ssd.shape
    di = heads * p
    cd = conv_w.shape[2]
    groups = (cd - di) // (2 * n)
    gn = groups * n
    hpg = heads // groups
    n_mem, mem_heads, mem_hd = cache_mem_k.shape[2:]
    md = mem_heads * mem_hd
    nbuf = state_pool.shape[2]
    kw1 = state_conv.shape[2]
    alpha = (2 * depth) ** 0.25
    assert seq % SSD_CHUNK == 0 and heads <= LANES and LANES % p == 0 and hpg % (LANES // p) == 0

    s3 = d + di + cd
    s4 = s3 + heads

    w_in_t = jnp.transpose(w_in, (0, 2, 1))
    w_dt = jnp.pad(w_in_t[:, s3:s4, :], ((0, 0), (0, LANES - heads), (0, 0)))
    bf = lambda w: w.astype(BF16)
    wpool_b, wpbr_b, wsbr_b, wout_b = bf(w_pool), bf(w_pool_br), bf(w_ssd_br), bf(w_out)
    wq_b, wk_b, wv_b, wo_b = bf(w_mq), bf(w_mk), bf(w_mv), bf(w_mo)
    wg_b, wu_b, wd_b = bf(w_ffn_gate), bf(w_ffn_up), bf(w_ffn_down)
    vec = lambda v: v.astype(F32).reshape(depth, 1, v.shape[-1])
    pad_h = lambda v: jnp.pad(v.astype(F32), ((0, 0), (0, LANES - heads))).reshape(depth, 1, LANES)
    scale, cbias, nw = vec(pool_scale), vec(conv_b), vec(ssd_norm_w)
    bias_p, alog_p = pad_h(dt_bias), pad_h(a_log)
    dskip_x = vec(jnp.repeat(d_skip, p, axis=1))
    mix_g, mix_b, mem_g, mem_b, ffn_g, ffn_b = (vec(v) for v in (ln_mix_g, ln_mix_b, ln_mem_g, ln_mem_b,
                                                                   ln_ffn_g, ln_ffn_b))
    e_group = _expansion_matrix(hpg, p, LANES)
    e_all = _expansion_matrix(heads, p, LANES)

    pool_state_t = jnp.transpose(state_pool, (0, 2, 1, 3))
    conv_state_t = jnp.transpose(state_conv, (0, 2, 1, 3))

    hp, hs = x_prompt.reshape(bp * seq, d), x_sample.reshape(bs, d)
    hp_b, hs_b = hp.astype(BF16), hs.astype(BF16)
    memory_bf = mem_prompt.reshape(bp * n_mem, d).astype(BF16)

    outs = {k: [] for k in ("ssd_p", "conv_p", "pool_p", "mk", "mv")}
    new_ssd_s = new_conv_s = new_pool_s = None
    for l in range(depth):

        def in_proj(hb):
            mm = functools.partial(_mm, hb, layer=l, w_is_nk=True)
            u = mm(w_in_t, col0=0, n=d, name="in_u")
            zs = mm(w_in_t, col0=d, n=di, act="silu", out_dtype=BF16, name="in_z")
            xbc = mm(w_in_t, col0=d + di, n=cd, name="in_xbc")
            dtr = mm(w_dt, name="in_dt")
            sgp = mm(w_in_t, col0=s4, n=d, act="sigmoid", out_dtype=BF16, name="in_gp")
            sgs = mm(w_in_t, col0=s4 + d, n=d, act="sigmoid", out_dtype=BF16, name="in_gs")
            return u, zs, xbc, dtr, sgp, sgs

        def tail(h, y, sgs, a_gated, attn):
            h1, h1b = _merge_proj_ln(y, sgs, a_gated, h, l, wsbr_b, wout_b, mix_g, mix_b, alpha=alpha)
            h2, h2b = attn(h1, h1b)
            return _ffn(h2b, h2, l, wg_b, wu_b, wd_b, ffn_g, ffn_b, alpha=alpha)

        u, zs, xbc, dtr, sgp, sgs = in_proj(hp_b)
        u3, xbc3 = u.reshape(bp, seq, d), xbc.reshape(bp, seq, cd)
        a_gated = _pool_prompt(u3, sgp.reshape(bp, seq, d), l, wpool_b, scale, wpbr_b).reshape(bp * seq, d)
        dt, cum, cumt = _dt_prep(dtr, l, bias_p, alog_p)
        r3 = lambda a: a.reshape(bp, seq, a.shape[-1])
        y, st = _ssd_prompt(xbc3, r3(zs), r3(dt), r3(cum), r3(cumt), l, conv_w, cbias, dskip_x, nw, e_group,
                            di=di, n=n, p=p, groups=groups)
        mk_p = _mm(memory_bf, wk_b, l, name="mem_k").reshape(bp, n_mem, md)
        mv_p = _mm(memory_bf, wv_b, l, name="mem_v").reshape(bp, n_mem, md)

        def attn_p(h1, h1b):
            o, ob = _attn_prompt(h1b.reshape(bp, seq, d), h1.reshape(bp, seq, d), mk_p, mv_p, l, wq_b, wo_b,
                                 mem_g, mem_b, heads=mem_heads, alpha=alpha)
            return o.reshape(bp * seq, d), ob.reshape(bp * seq, d)

        hp, hp_b = tail(hp, y.reshape(bp * seq, di), sgs, a_gated, attn_p)
        outs["ssd_p"].append(st)
        outs["conv_p"].append(xbc3[:, seq - kw1:, :])
        outs["pool_p"].append(u3[:, seq - nbuf:, :])
        outs["mk"].append(mk_p.reshape(bp, n_mem, mem_heads, mem_hd))
        outs["mv"].append(mv_p.reshape(bp, n_mem, mem_heads, mem_hd))

        u, zs, xbc, dtr, sgp, sgs = in_proj(hs_b)
        a_gated, new_pool_s = _pool_sample(u, sgp, pool_state_t, l, new_pool_s, wpool_b, scale, wpbr_b)
        xs, xdt, dae, bm, cm, new_conv_s = _ssd_sample_pre(xbc, dtr, conv_state_t, l, new_conv_s, conv_w, cbias,
                                                           bias_p, alog_p, e_all, di=di, gn=gn)
        new_ssd_s, ys = _ssd_sample_state(state_ssd, l, new_ssd_s, xdt, dae, bm, cm, groups=groups)
        y = _ssd_sample_post(ys, xs, zs, l, dskip_x, nw, groups=groups)

        def attn_s(h1, h1b):
            return _attn_sample(h1b, h1, cache_mem_k, cache_mem_v, l, wq_b, wo_b, mem_g, mem_b, alpha=alpha)

        hs, hs_b = tail(hs, y, sgs, a_gated, attn_s)

    stack = lambda k: jnp.stack(outs[k])
    return (hp.reshape(bp, seq, d), hs.reshape(bs, 1, d), stack("ssd_p"), stack("conv_p"), stack("pool_p"),
            stack("mk"), stack("mv"), new_ssd_s, jnp.transpose(new_conv_s, (0, 2, 1, 3)),
            jnp.transpose(new_pool_s, (0, 2, 1, 3)))
```

```python
import functools
import math

import jax
import jax.numpy as jnp
from jax import lax
from jax.experimental import pallas as pl
from jax.experimental.pallas import tpu as pltpu

F32 = jnp.float32
BF16 = jnp.bfloat16

POOL_WINDOWS = (2, 4, 8, 16)
PAST_LEN = 16384
SSD_CHUNK = 128
LN_EPS = 1e-5
RMS_EPS = 1e-5
LANES = 128
VMEM_LIMIT_BYTES = 56 * 1024 * 1024


def _params(*sem):
    return pltpu.CompilerParams(dimension_semantics=sem, vmem_limit_bytes=VMEM_LIMIT_BYTES)


def _silu(x):
    return x * jax.nn.sigmoid(x)


def _softplus(x):
    return jnp.maximum(x, 0.0) + jnp.log(1.0 + jnp.exp(-jnp.abs(x)))


def _layer_norm(x, g, b):
    mu = jnp.mean(x, axis=-1, keepdims=True)
    xc = x - mu
    var = jnp.mean(xc * xc, axis=-1, keepdims=True)
    return xc * lax.rsqrt(var + LN_EPS) * g + b


def _split2(x):
    hi = x.astype(BF16)
    mid = (x - hi.astype(F32)).astype(BF16)
    return hi, mid


def _tile(n, pref):
    t = min(n, pref)
    while n % t:
        t //= 2
    return t


def _lspec(layer, block, index_map=None, **kwargs):
    nd = len(block)
    if index_map is None:
        index_map = lambda *g: (0,) * nd
    return pl.BlockSpec((None,) + tuple(block), lambda *g: (layer,) + tuple(index_map(*g)), **kwargs)


def _mm_kernel(x_ref, w_ref, o_ref, *, act, w_is_nk):
    if w_is_nk:
        acc = lax.dot_general(x_ref[...], w_ref[0].astype(BF16), (((1,), (1,)), ((), ())),
                              preferred_element_type=F32)
    else:
        acc = jnp.dot(x_ref[...], w_ref[...], preferred_element_type=F32)
    if act == "silu":
        acc = _silu(acc)
    elif act == "sigmoid":
        acc = jax.nn.sigmoid(acc)
    o_ref[...] = acc.astype(o_ref.dtype)


def _mm(x, w, layer, *, col0=0, n=None, act=None, out_dtype=F32, w_is_nk=False, name):
    m, k = x.shape
    n = w.shape[1 if w_is_nk else 2] if n is None else n
    if w_is_nk:
        tm, tn = _tile(m, 2048), _tile(n, 512)
        assert col0 % 16 == 0
        w_spec = pl.BlockSpec((pl.Element(1), pl.Element(tn), pl.Element(k)),
                              lambda i, j: (layer, pl.multiple_of(col0 + j * tn, 16), 0))
    else:
        tm, tn = _tile(m, 1024), _tile(n, 1024)
        assert col0 % tn == 0
        w_spec = _lspec(layer, (k, tn), lambda i, j: (0, col0 // tn + j))
    return pl.pallas_call(
        functools.partial(_mm_kernel, act=act, w_is_nk=w_is_nk),
        grid=(m // tm, n // tn),
        in_specs=[pl.BlockSpec((tm, k), lambda i, j: (i, 0)), w_spec],
        out_specs=pl.BlockSpec((tm, tn), lambda i, j: (i, j)),
        out_shape=jax.ShapeDtypeStruct((m, n), out_dtype),
        compiler_params=_params("parallel", "arbitrary"),
        name=name,
    )(x, w)


def _in_proj_main_kernel(x_ref, w_ref, u_ref, z_ref, xbc_ref, *, nu, nz):
    j = pl.program_id(1)
    acc = lax.dot_general(x_ref[...], w_ref[0].astype(BF16), (((1,), (1,)), ((), ())), preferred_element_type=F32)

    @pl.when(j < nu)
    def _():
        u_ref[...] = acc

    @pl.when((j >= nu) & (j < nu + nz))
    def _():
        z_ref[...] = _silu(acc).astype(z_ref.dtype)

    @pl.when(j >= nu + nz)
    def _():
        xbc_ref[...] = acc


def _in_proj_main(x, w_t, layer, *, d, di, cd):
    m, k = x.shape
    tm, tn = _tile(m, 2048), 512
    assert d % tn == 0 and di % tn == 0 and cd % tn == 0
    nu, nz, nx = d // tn, di // tn, cd // tn
    clamp = lambda j, lo, cnt: jnp.clip(j - lo, 0, cnt - 1)
    return pl.pallas_call(
        functools.partial(_in_proj_main_kernel, nu=nu, nz=nz),
        grid=(m // tm, nu + nz + nx),
        in_specs=[pl.BlockSpec((tm, k), lambda i, j: (i, 0)),
                  pl.BlockSpec((pl.Element(1), pl.Element(tn), pl.Element(k)),
                               lambda i, j: (layer, pl.multiple_of(j * tn, 16), 0))],
        out_specs=[pl.BlockSpec((tm, tn), lambda i, j: (i, clamp(j, 0, nu))),
                   pl.BlockSpec((tm, tn), lambda i, j: (i, clamp(j, nu, nz))),
                   pl.BlockSpec((tm, tn), lambda i, j: (i, clamp(j, nu + nz, nx)))],
        out_shape=[jax.ShapeDtypeStruct((m, d), F32), jax.ShapeDtypeStruct((m, di), BF16),
                   jax.ShapeDtypeStruct((m, cd), F32)],
        compiler_params=_params("parallel", "arbitrary"),
        name="in_main",
    )(x, w_t)


def _pool_project(pooled_ref, wg_ref, scale_ref, wbr_ref, sg, n_groups, gc):
    for g in range(n_groups):
        cols = slice(g * gc, (g + 1) * gc)
        pg = jnp.dot(pooled_ref[:, cols], wg_ref[g], preferred_element_type=F32) * scale_ref[:, cols]
        pooled_ref[:, cols] = pg.astype(BF16)
    a = jnp.dot(pooled_ref[...], wbr_ref[...], preferred_element_type=F32)
    return a * sg.astype(F32)


def _pool_prompt_kernel(u_ref, wg_ref, scale_ref, wbr_ref, sg_ref, o_ref, carry_ref, pooled_ref, *, tl, gc):
    j = pl.program_id(1)

    @pl.when(j == 0)
    def _():
        carry_ref[...] = jnp.zeros(carry_ref.shape, F32)

    nt = tl // 8
    sub = lax.broadcasted_iota(jnp.int32, (nt, 8, gc), 1)
    pos = j * tl + lax.broadcasted_iota(jnp.int32, (tl, 1), 0)
    for g, w in enumerate(POOL_WINDOWS):
        cols = slice(g * gc, (g + 1) * gc)
        ug = u_ref[0, :, cols]
        acc, s, level = ug, 1, 0
        while s < w:
            a3 = acc.reshape(nt, 8, gc)
            before = jnp.concatenate([carry_ref[level, :, cols][None], a3[:nt - 1]], axis=0)
            carry_ref[level, :, cols] = acc[tl - 8:tl, :]
            shifted = before if s == 8 else pltpu.roll(jnp.where(sub >= 8 - s, before, a3), s, axis=1)
            acc = acc + shifted.reshape(tl, gc)
            s, level = 2 * s, level + 1
        cnt = jnp.minimum(w, pos + 1).astype(F32)
        pooled_ref[:, cols] = (acc / cnt - ug).astype(BF16)
    o_ref[0] = _pool_project(pooled_ref, wg_ref, scale_ref, wbr_ref, sg_ref[0], len(POOL_WINDOWS), gc)


def _pool_prompt(u, sg, layer, wg, scale, wbr):
    b, l, d = u.shape
    _, ng, gc, _ = wg.shape
    tl = _tile(l, 256)
    assert all(w in (1, 2, 4, 8, 16) for w in POOL_WINDOWS) and tl % 8 == 0
    levels = max(POOL_WINDOWS).bit_length() - 1
    tok = pl.BlockSpec((1, tl, d), lambda i, j: (i, j, 0))
    return pl.pallas_call(
        functools.partial(_pool_prompt_kernel, tl=tl, gc=gc),
        grid=(b, l // tl),
        in_specs=[tok, _lspec(layer, (ng, gc, gc)), _lspec(layer, (1, d)), _lspec(layer, (d, d)), tok],
        out_specs=tok,
        out_shape=jax.ShapeDtypeStruct((b, l, d), F32),
        scratch_shapes=[pltpu.VMEM((levels, 8, d), F32), pltpu.VMEM((tl, d), BF16)],
        compiler_params=_params("parallel", "arbitrary"),
        name="pool_prompt",
    )(u, wg, scale, wbr, sg)


def _layer_state_call(kern, *, layer, state, batch_axis, stacked_so_far, grid, tb, other_operands, other_in_specs,
                      other_out_specs, other_out_shapes, scratch_shapes=(), name):
    blk = tuple(tb if a == batch_axis else s for a, s in enumerate(state.shape))[1:]
    spec = _lspec(layer, blk, lambda i: tuple(i if a == batch_axis else 0 for a in range(1, state.ndim)))
    if stacked_so_far is None:
        stacked_so_far = jnp.zeros(state.shape, state.dtype)
    operands = list(other_operands) + [state, stacked_so_far]
    in_specs = list(other_in_specs) + [spec, pl.BlockSpec(memory_space=pl.ANY)]
    return pl.pallas_call(
        kern,
        grid=grid,
        in_specs=in_specs,
        out_specs=list(other_out_specs) + [spec],
        out_shape=list(other_out_shapes) + [jax.ShapeDtypeStruct(state.shape, state.dtype)],
        input_output_aliases={len(operands) - 1: len(other_out_shapes)},
        scratch_shapes=list(scratch_shapes),
        compiler_params=_params("parallel"),
        name=name,
    )(*operands)


def _pool_sample_kernel(u_ref, wg_ref, scale_ref, wbr_ref, sg_ref, buf_ref, stack_hbm_ref,
                        o_ref, nbuf_ref, pooled_ref, *, gc):
    del stack_hbm_ref
    nbuf = buf_ref.shape[0]
    u = u_ref[...]
    for r in range(nbuf - 1):
        nbuf_ref[r] = buf_ref[r + 1]
    nbuf_ref[nbuf - 1] = u
    for g, w in enumerate(POOL_WINDOWS):
        cols = slice(g * gc, (g + 1) * gc)
        ug = u[:, cols]
        s = ug
        for k in range(1, w):
            s = s + buf_ref[nbuf - k, :, cols]
        cnt = float(min(w, PAST_LEN + 1))
        pooled_ref[:, cols] = (s / cnt - ug).astype(BF16)
    o_ref[...] = _pool_project(pooled_ref, wg_ref, scale_ref, wbr_ref, sg_ref[...], len(POOL_WINDOWS), gc)


def _pool_sample(u, sg, pool_state_t, layer, stacked_so_far, wg, scale, wbr):
    bs, d = u.shape
    _, ng, gc, _ = wg.shape
    tb = _tile(bs, 32)
    row = pl.BlockSpec((tb, d), lambda i: (i, 0))
    return _layer_state_call(
        functools.partial(_pool_sample_kernel, gc=gc),
        layer=layer, state=pool_state_t, batch_axis=2, stacked_so_far=stacked_so_far, grid=(bs // tb,), tb=tb,
        other_operands=[u, wg, scale, wbr, sg],
        other_in_specs=[row, _lspec(layer, (ng, gc, gc)), _lspec(layer, (1, d)), _lspec(layer, (d, d)), row],
        other_out_specs=[row], other_out_shapes=[jax.ShapeDtypeStruct((bs, d), F32)],
        scratch_shapes=[pltpu.VMEM((tb, d), BF16)], name="pool_sample")


def _dt_prep_kernel(dtr_ref, bias_ref, alog_ref, dt_ref, cum_ref, cumt_ref):
    c = SSD_CHUNK
    dt_ref[...] = _softplus(dtr_ref[...] + bias_ref[...])
    a = -jnp.exp(alog_ref[...])
    row = lax.broadcasted_iota(jnp.int32, (c, dtr_ref.shape[1]), 0)
    for ch in range(dtr_ref.shape[0] // c):
        rows = slice(ch * c, (ch + 1) * c)
        cum = dt_ref[rows, :] * a
        k = 1
        while k < c:
            cum = cum + jnp.where(row >= k, pltpu.roll(cum, k, axis=0), 0.0)
            k *= 2
        cum_ref[rows, :] = cum
        cumt_ref[rows, :] = cum.T


def _dt_prep(dtr, layer, bias, alog):
    t, hp = dtr.shape
    c = SSD_CHUNK * _tile(t // SSD_CHUNK, 8)
    spec = pl.BlockSpec((c, hp), lambda i: (i, 0))
    return pl.pallas_call(
        _dt_prep_kernel,
        grid=(t // c,),
        in_specs=[spec, _lspec(layer, (1, hp)), _lspec(layer, (1, hp))],
        out_specs=[spec, spec, spec],
        out_shape=[jax.ShapeDtypeStruct((t, hp), F32)] * 3,
        compiler_params=_params("parallel"),
        name="dt_prep",
    )(dtr, bias, alog)


def _conv_silu(x, carry_ref, w_ref, b_ref):
    rows, wd = x.shape
    kw = w_ref.shape[0]
    nt = rows // 8
    last_row = lax.broadcasted_iota(jnp.int32, (nt, 8, wd), 1) == 7
    t = x * w_ref[0:1, :]
    for k in range(1, kw):
        t3 = t.reshape(nt, 8, wd)
        before = jnp.concatenate([carry_ref[k - 1][None], t3[:nt - 1]], axis=0)
        carry_ref[k - 1] = t[rows - 8:rows, :]
        shifted = pltpu.roll(jnp.where(last_row, before, t3), 1, axis=1).reshape(rows, wd)
        t = x * w_ref[k:k + 1, :] + shifted
    return _silu(t + b_ref[...])


def _ssd_prompt_kernel(xs_ref, bm_ref, cm_ref, zs_ref, dt_ref, cum_ref, cumt_ref,
                       wx_ref, wb_ref, wc_ref, bx_ref, bb_ref, bc_ref, dskip_ref, nw_ref, e_ref,
                       y_ref, st_ref, *rest, hpg, p, gps, zero_fill):
    stt_ref, hx_ref, hb_ref, hc_ref = rest[-4:]
    c = pl.program_id(2)
    nc = pl.num_programs(2)
    rows = xs_ref.shape[1]
    n = bm_ref.shape[2] // gps
    gw = hpg * p
    if zero_fill:
        rest[0][...] = jnp.zeros(rest[0].shape, F32)

    @pl.when(c == 0)
    def _():
        for r in (stt_ref, hx_ref, hb_ref, hc_ref):
            r[...] = jnp.zeros(r.shape, F32)

    x_all = _conv_silu(xs_ref[0], hx_ref, wx_ref, bx_ref)
    b_all = _conv_silu(bm_ref[0], hb_ref, wb_ref, bb_ref).astype(BF16)
    c_all = _conv_silu(cm_ref[0], hc_ref, wc_ref, bc_ref).astype(BF16)
    causal = lax.broadcasted_iota(jnp.int32, (rows, rows), 0) >= lax.broadcasted_iota(jnp.int32, (rows, rows), 1)
    lane = lax.broadcasted_iota(jnp.int32, (rows, LANES), 1)
    heads_per_tile = LANES // p
    lane_masks = [jnp.where(lane // p == r, 1.0, 0.0).astype(BF16) for r in range(heads_per_tile)]

    for gi in range(gps):
        g = pl.program_id(1) * gps + gi
        gcols = slice(gi * gw, (gi + 1) * gw)
        x = x_all[:, gcols]
        bmat = b_all[:, gi * n:(gi + 1) * n]
        cmat = c_all[:, gi * n:(gi + 1) * n]
        shift = (LANES - g * hpg) % LANES
        dtg = pltpu.roll(dt_ref[0], shift, axis=1)
        cumg = pltpu.roll(cum_ref[0], shift, axis=1)
        cumtg = cumt_ref[0, pl.ds(pl.multiple_of(g * hpg, hpg), hpg), :]
        last = cumg[rows - 1:rows, :]
        stack = jnp.concatenate([dtg, jnp.exp(cumg), jnp.exp(last - cumg)], axis=0)
        hi, mid = _split2(stack)
        expanded = jnp.dot(jnp.concatenate([hi, mid], axis=1), e_ref[...], preferred_element_type=F32)
        dt_x, ecum_x, ws_x = expanded[0:rows], expanded[rows:2 * rows], expanded[2 * rows:3 * rows]

        xdt = x * dt_x
        xdt_b = xdt.astype(BF16)
        xdtw_b = (xdt * ws_x).astype(BF16)
        cb = lax.dot_general(cmat, bmat, (((1,), (1,)), ((), ())), preferred_element_type=F32)
        st_old = stt_ref[:, gcols]
        y = jnp.dot(cmat, st_old.astype(BF16), preferred_element_type=F32) * ecum_x
        stt_ref[:, gcols] = st_old * ecum_x[rows - 1:rows, :] + lax.dot_general(
            bmat, xdtw_b, (((0,), (0,)), ((), ())), preferred_element_type=F32)

        cum2 = cumg * math.log2(math.e)
        cumt2 = cumtg * math.log2(math.e)
        diag = []
        for q in range(hpg // heads_per_tile):
            xq = xdt_b[:, q * LANES:(q + 1) * LANES]
            ms, xr = [], []
            for r in range(heads_per_tile):
                h = q * heads_per_tile + r
                dec = jnp.exp2(cum2[:, h:h + 1] - cumt2[h:h + 1, :])
                ms.append(jnp.where(causal, cb * dec, 0.0).astype(BF16))
                xr.append(xq * lane_masks[r])
            diag.append(jnp.dot(jnp.concatenate(ms, axis=1), jnp.concatenate(xr, axis=0),
                                preferred_element_type=F32))
        y = y + jnp.concatenate(diag, axis=1) + dskip_ref[:, gcols] * x
        y = y * zs_ref[0, :, gcols].astype(F32)
        y = y * lax.rsqrt(jnp.mean(y * y, axis=-1, keepdims=True) + RMS_EPS) * nw_ref[:, gcols]
        y_ref[0, :, gcols] = y.astype(y_ref.dtype)

    @pl.when(c == nc - 1)
    def _():
        st_ref[0] = stt_ref[...].T.reshape(st_ref.shape[1:])


def _ssd_prompt(xbc, zs, dt, cum, cumt, layer, conv_w, conv_b, dskip_x, norm_w, e_mat, *, di, n, p, groups,
                zero_fill_shape=None):
    b, l, cd = xbc.shape
    c = SSD_CHUNK
    heads = di // p
    hpg = heads // groups
    gps = 4 if groups % 4 == 0 else 1
    gw = hpg * p * gps
    nn = n * gps
    hp = dt.shape[-1]
    xoff, boff, coff = 0, di // nn, (di + groups * n) // nn

    def seg(width, off_blocks):
        return pl.BlockSpec((1, c, width), lambda i, g, k: (i, k, off_blocks + g))

    def wseg(rows, width, off_blocks):
        return _lspec(layer, (rows, width), lambda i, g, k: (0, off_blocks + g))

    small = pl.BlockSpec((1, c, hp), lambda i, g, k: (i, k, 0))
    kw = conv_w.shape[1]
    ng, nc = groups // gps, l // c
    extra_specs, extra_shapes = [], []
    if zero_fill_shape is not None and zero_fill_shape[0] % (b * ng * nc):
        zero_fill_shape = None
    if zero_fill_shape is not None:
        per_step = zero_fill_shape[0] // (b * ng * nc)
        tail = tuple(zero_fill_shape[1:])
        extra_specs = [pl.BlockSpec((per_step,) + tail,
                                    lambda i, g, k: ((i * ng + g) * nc + k,) + (0,) * len(tail))]
        extra_shapes = [jax.ShapeDtypeStruct(tuple(zero_fill_shape), F32)]
    outs = pl.pallas_call(
        functools.partial(_ssd_prompt_kernel, hpg=hpg, p=p, gps=gps, zero_fill=zero_fill_shape is not None),
        grid=(b, ng, nc),
        in_specs=[
            seg(gw, xoff), seg(nn, boff), seg(nn, coff), seg(gw, 0), small, small, small,
            wseg(kw, gw, xoff), wseg(kw, nn, boff), wseg(kw, nn, coff),
            wseg(1, gw, xoff), wseg(1, nn, boff), wseg(1, nn, coff),
            wseg(1, gw, 0), wseg(1, gw, 0),
            pl.BlockSpec(e_mat.shape, lambda i, g, k: (0, 0)),
        ],
        out_specs=[
            pl.BlockSpec((1, c, gw), lambda i, g, k: (i, k, g)),
            pl.BlockSpec((1, hpg * gps, p, n), lambda i, g, k: (i, g, 0, 0)),
        ] + extra_specs,
        out_shape=[jax.ShapeDtypeStruct((b, l, di), BF16), jax.ShapeDtypeStruct((b, heads, p, n), F32)]
        + extra_shapes,
        scratch_shapes=[
            pltpu.VMEM((n, gw), F32),
            pltpu.VMEM((kw - 1, 8, gw), F32), pltpu.VMEM((kw - 1, 8, nn), F32), pltpu.VMEM((kw - 1, 8, nn), F32),
        ],
        compiler_params=_params("parallel", "parallel", "arbitrary"),
        name="ssd_prompt",
    )(xbc, xbc, xbc, zs, dt, cum, cumt, conv_w, conv_w, conv_w, conv_b, conv_b, conv_b, dskip_x, norm_w, e_mat)
    return (outs[0], outs[1], outs[2] if extra_shapes else None)


def _ssd_sample_pre_kernel(xbc_ref, w_ref, b_ref, dtr_ref, bias_ref, alog_ref, e_ref, cbuf_ref, stack_hbm_ref,
                           xs_ref, xdt_ref, dae_ref, bm_ref, cm_ref, ncbuf_ref, *, di, gn):
    del stack_hbm_ref
    kw = w_ref.shape[0]
    xbc = xbc_ref[...]
    acc = b_ref[...]
    for k in range(kw - 1):
        acc = acc + cbuf_ref[k] * w_ref[k:k + 1, :]
    acc = _silu(acc + xbc * w_ref[kw - 1:kw, :])
    for k in range(kw - 2):
        ncbuf_ref[k] = cbuf_ref[k + 1]
    ncbuf_ref[kw - 2] = xbc
    xs = acc[:, 0:di]
    dt = _softplus(dtr_ref[...] + bias_ref[...])
    dae_ref[...] = jnp.exp(dt * (-jnp.exp(alog_ref[...])))
    hi, mid = _split2(dt)
    dt_x = jnp.dot(jnp.concatenate([hi, mid], axis=1), e_ref[...], preferred_element_type=F32)
    xs_ref[...] = xs
    xdt_ref[...] = xs * dt_x
    bm_ref[...] = acc[:, di:di + gn]
    cm_ref[...] = acc[:, di + gn:di + 2 * gn]


def _ssd_sample_pre(xbc, dtr, conv_state_t, layer, stacked_so_far, conv_w, conv_b, bias, alog, e_mat, *, di, gn):
    bs, cd = xbc.shape
    hp = dtr.shape[1]
    kw = conv_w.shape[1]
    tb = _tile(bs, 32)
    row = lambda w: pl.BlockSpec((tb, w), lambda i: (i, 0))
    return _layer_state_call(
        functools.partial(_ssd_sample_pre_kernel, di=di, gn=gn),
        layer=layer, state=conv_state_t, batch_axis=2, stacked_so_far=stacked_so_far, grid=(bs // tb,), tb=tb,
        other_operands=[xbc, conv_w, conv_b, dtr, bias, alog, e_mat],
        other_in_specs=[row(cd), _lspec(layer, (kw, cd)), _lspec(layer, (1, cd)), row(hp), _lspec(layer, (1, hp)),
                        _lspec(layer, (1, hp)), pl.BlockSpec(e_mat.shape, lambda i: (0, 0))],
        other_out_specs=[row(di), row(di), row(hp), row(gn), row(gn)],
        other_out_shapes=[jax.ShapeDtypeStruct((bs, di), F32)] * 2 + [jax.ShapeDtypeStruct((bs, hp), F32)]
        + [jax.ShapeDtypeStruct((bs, gn), F32)] * 2,
        name="ssd_sample_pre")


def _ssd_sample_state_kernel(dae_ref, xdt_ref, bm_ref, cm_ref, s_ref, stack_hbm_ref, y_ref, so_ref, *, groups):
    del stack_hbm_ref
    tb, heads, p, n = s_ref.shape
    di = heads * p
    gw = di // groups
    grow = lax.broadcasted_iota(jnp.int32, (groups, di), 0)
    gcol = lax.broadcasted_iota(jnp.int32, (groups, di), 1) // gw
    own = grow == gcol
    fill = (-groups) % 16
    for b in range(tb):
        seq = pl.program_id(0) * tb + b
        xg = jnp.where(own, jnp.broadcast_to(xdt_ref[b], (groups, di)), 0.0)
        bmat = bm_ref[b]
        if fill:
            xg = jnp.concatenate([xg, jnp.zeros((fill, di), F32)], axis=0)
            bmat = jnp.concatenate([bmat, jnp.zeros((fill, n), F32)], axis=0)
        upd = lax.dot_general(xg.astype(BF16), bmat.astype(BF16), (((0,), (0,)), ((), ())),
                              preferred_element_type=F32)
        for h in range(heads):
            so_ref[b, h] = s_ref[b, h] * dae_ref[seq, h] + upd[h * p:(h + 1) * p, :]
        s_new = so_ref[b].reshape(di, n).astype(BF16)
        yg = lax.dot_general(cm_ref[b].astype(BF16), s_new, (((1,), (1,)), ((), ())),
                             preferred_element_type=F32)
        y_ref[b] = jnp.sum(jnp.where(own, yg, 0.0), axis=0, keepdims=True)


def _ssd_sample_state(ssd_state, layer, stacked_so_far, xdt, dae, bm, cm, *, groups):
    _, bs, heads, p, n = ssd_state.shape
    di = heads * p
    tb = _tile(bs, 2)
    xdt3 = xdt.reshape(bs, 1, di)
    bm3, cm3 = bm.reshape(bs, groups, n), cm.reshape(bs, groups, n)
    rspec = pl.BlockSpec((tb, 1, di), lambda i: (i, 0, 0))
    gspec = pl.BlockSpec((tb, groups, n), lambda i: (i, 0, 0))
    y, stacked = _layer_state_call(
        functools.partial(_ssd_sample_state_kernel, groups=groups),
        layer=layer, state=ssd_state, batch_axis=1, stacked_so_far=stacked_so_far, grid=(bs // tb,), tb=tb,
        other_operands=[dae, xdt3, bm3, cm3],
        other_in_specs=[pl.BlockSpec(memory_space=pltpu.SMEM), rspec, gspec, gspec],
        other_out_specs=[rspec], other_out_shapes=[jax.ShapeDtypeStruct((bs, 1, di), F32)],
        name="ssd_sample_state")
    return stacked, y.reshape(bs, di)


def _ssd_sample_post_kernel(y_ref, xs_ref, zs_ref, dskip_ref, nw_ref, o_ref, *, groups):
    y = (y_ref[...] + dskip_ref[...] * xs_ref[...]) * zs_ref[...].astype(F32)
    gw = y.shape[1] // groups
    for g in range(groups):
        cols = slice(g * gw, (g + 1) * gw)
        yg = y[:, cols]
        yg = yg * lax.rsqrt(jnp.mean(yg * yg, axis=-1, keepdims=True) + RMS_EPS) * nw_ref[:, cols]
        o_ref[:, cols] = yg.astype(o_ref.dtype)


def _ssd_sample_post(y, xs, zs, layer, dskip_x, norm_w, *, groups):
    bs, di = y.shape
    full = pl.BlockSpec((bs, di), lambda i: (0, 0))
    return pl.pallas_call(
        functools.partial(_ssd_sample_post_kernel, groups=groups),
        grid=(1,),
        in_specs=[full, full, full, _lspec(layer, (1, di)), _lspec(layer, (1, di))],
        out_specs=full,
        out_shape=jax.ShapeDtypeStruct(y.shape, BF16),
        compiler_params=_params("arbitrary"),
        name="ssd_sample_post",
    )(y, xs, zs, dskip_x, norm_w)


def _merge_proj_ln_kernel(y_ref, sg_ref, a_ref, x_ref, wb_ref, wo_ref, g_ref, b_ref, o_ref, ob_ref, *, alpha):
    tm = y_ref.shape[0]
    halves = 2 if tm % 16 == 0 else 1
    for r in range(halves):
        rows = slice(r * tm // halves, (r + 1) * tm // halves)
        bb = jnp.dot(y_ref[rows, :], wb_ref[...], preferred_element_type=F32)
        merged = (a_ref[rows, :] + sg_ref[rows, :].astype(F32) * bb).astype(BF16)
        f = jnp.dot(merged, wo_ref[...], preferred_element_type=F32)
        h = _layer_norm(alpha * x_ref[rows, :] + f, g_ref[...], b_ref[...])
        o_ref[rows, :] = h
        ob_ref[rows, :] = h.astype(BF16)


def _merge_proj_ln(y, sg, a, x, layer, wb, wo, g, b, *, alpha):
    m, k = y.shape
    d = wo.shape[2]
    tm = _tile(m, 256)
    row = lambda width: pl.BlockSpec((tm, width), lambda i: (i, 0))
    once = pl.Buffered(1)
    return pl.pallas_call(
        functools.partial(_merge_proj_ln_kernel, alpha=alpha),
        grid=(m // tm,),
        in_specs=[row(k), pl.BlockSpec((tm, d), lambda i: (i, 1)),
                  row(d), row(d), _lspec(layer, (k, d), pipeline_mode=once),
                  _lspec(layer, (d, d), pipeline_mode=once), _lspec(layer, (1, d)), _lspec(layer, (1, d))],
        out_specs=[row(d), row(d)],
        out_shape=[jax.ShapeDtypeStruct((m, d), F32), jax.ShapeDtypeStruct((m, d), BF16)],
        compiler_params=_params("parallel"),
        name="merge_proj_ln",
    )(y, sg, a, x, wb, wo, g, b)


def _attn_prompt_kernel(hb_ref, h_ref, k_ref, v_ref, wq_ref, wo_ref, g_ref, b_ref, o_ref, ob_ref, *,
                        heads, alpha):
    q = jnp.dot(hb_ref[0], wq_ref[...], preferred_element_type=F32)
    hd = q.shape[1] // heads
    scale = 1.0 / math.sqrt(hd)
    kb = k_ref[0].astype(BF16)
    vb = v_ref[0].astype(BF16)
    outs = []
    for h in range(heads):
        cols = slice(h * hd, (h + 1) * hd)
        s = lax.dot_general(q[:, cols].astype(BF16), kb[:, cols], (((1,), (1,)), ((), ())),
                            preferred_element_type=F32) * scale
        e = jnp.exp(s - jnp.max(s, axis=-1, keepdims=True))
        pr = e / jnp.sum(e, axis=-1, keepdims=True)
        outs.append(jnp.dot(pr.astype(BF16), vb[:, cols], preferred_element_type=F32))
    o = jnp.concatenate(outs, axis=1).astype(BF16)
    ca = jnp.dot(o, wo_ref[...], preferred_element_type=F32)
    hn = _layer_norm(alpha * h_ref[0] + ca, g_ref[...], b_ref[...])
    o_ref[0] = hn
    ob_ref[0] = hn.astype(BF16)


def _attn_prompt(hb, h, mk, mv, layer, wq, wo, g, b, *, heads, alpha):
    bsz, l, d = h.shape
    nm, md = mk.shape[1], mk.shape[2]
    tq = _tile(l, 512)
    row = pl.BlockSpec((1, tq, d), lambda i, j: (i, j, 0))
    mem = pl.BlockSpec((1, nm, md), lambda i, j: (i, 0, 0))
    return pl.pallas_call(
        functools.partial(_attn_prompt_kernel, heads=heads, alpha=alpha),
        grid=(bsz, l // tq),
        in_specs=[row, row, mem, mem, _lspec(layer, (d, md)), _lspec(layer, (md, d)), _lspec(layer, (1, d)),
                  _lspec(layer, (1, d))],
        out_specs=[row, row],
        out_shape=[jax.ShapeDtypeStruct((bsz, l, d), F32), jax.ShapeDtypeStruct((bsz, l, d), BF16)],
        compiler_params=_params("parallel", "parallel"),
        name="attn_prompt",
    )(hb, h, mk, mv, wq, wo, g, b)


def _attn_sample_kernel(hb_ref, h_ref, k_ref, v_ref, wq_ref, wo_ref, g_ref, b_ref, o_ref, ob_ref, att_ref, *,
                        heads, alpha):
    tb = hb_ref.shape[0]
    rows, hd = k_ref.shape[1], k_ref.shape[2]
    reps = 8 // heads
    scale = 1.0 / math.sqrt(hd)
    hb = hb_ref[...]
    qs = [jnp.dot(hb, wq_ref[:, h * hd:(h + 1) * hd], preferred_element_type=F32) for h in range(heads)]

    def over_copies(x, op):
        out = x
        for r in range(1, reps):
            out = op(out, pltpu.roll(x, r * heads, axis=0))
        return out

    for bi in range(tb):
        q8 = jnp.concatenate([q[bi:bi + 1, :] for q in qs] * reps, axis=0)
        k3 = k_ref[bi].reshape(rows // 8, 8, hd)
        v3 = v_ref[bi].reshape(rows // 8, 8, hd)
        s = jnp.sum(k3 * q8[None], axis=-1, keepdims=True) * scale
        m = over_copies(jnp.max(s, axis=0), jnp.maximum)
        e = jnp.exp(s - m[None])
        den = over_copies(jnp.sum(e, axis=0), jnp.add)
        o8 = over_copies(jnp.sum((e / den[None]) * v3, axis=0), jnp.add)
        att_ref[bi] = o8[0:heads, :]
    ca = None
    for h in range(heads):
        t = jnp.dot(att_ref[:, h, :].astype(BF16), wo_ref[h * hd:(h + 1) * hd, :], preferred_element_type=F32)
        ca = t if ca is None else ca + t
    hn = _layer_norm(alpha * h_ref[...] + ca, g_ref[...], b_ref[...])
    o_ref[...] = hn
    ob_ref[...] = hn.astype(BF16)


def _attn_sample(hb, h, cache_k, cache_v, layer, wq, wo, g, b, *, alpha):
    bs, d = h.shape
    depth, _, nm, heads, hd = cache_k.shape
    assert 8 % heads == 0
    md = heads * hd
    ck = cache_k.reshape(depth, bs, nm * heads, hd)
    cv = cache_v.reshape(depth, bs, nm * heads, hd)
    tb = _tile(bs, 8)
    row = pl.BlockSpec((tb, d), lambda i: (i, 0))
    mem = _lspec(layer, (tb, nm * heads, hd), lambda i: (i, 0, 0))
    return pl.pallas_call(
        functools.partial(_attn_sample_kernel, heads=heads, alpha=alpha),
        grid=(bs // tb,),
        in_specs=[row, row, mem, mem, _lspec(layer, (d, md)), _lspec(layer, (md, d)), _lspec(layer, (1, d)),
                  _lspec(layer, (1, d))],
        out_specs=[row, row],
        out_shape=[jax.ShapeDtypeStruct((bs, d), F32), jax.ShapeDtypeStruct((bs, d), BF16)],
        scratch_shapes=[pltpu.VMEM((tb, heads, hd), F32)],
        compiler_params=_params("parallel"),
        name="attn_sample",
    )(hb, h, ck, cv, wq, wo, g, b)


def _ffn_kernel(hb_ref, h_ref, wg_ref, wu_ref, wd_ref, g_ref, b_ref, o_ref, ob_ref, *, alpha):
    f = pl.program_id(1)
    tm, d = o_ref.shape

    @pl.when(f == 0)
    def _():
        o_ref[...] = jnp.zeros(o_ref.shape, F32)

    x = hb_ref[...]
    gate = jnp.dot(x, wg_ref[...], preferred_element_type=F32)
    up = jnp.dot(x, wu_ref[...], preferred_element_type=F32)
    act = (_silu(gate) * up).astype(BF16)
    for c in range(2):
        cols = slice(c * d // 2, (c + 1) * d // 2)
        o_ref[:, cols] += jnp.dot(act, wd_ref[:, cols], preferred_element_type=F32)

    @pl.when(f == pl.num_programs(1) - 1)
    def _():
        for r in range(4):
            rows = slice(r * tm // 4, (r + 1) * tm // 4)
            hn = _layer_norm(alpha * h_ref[rows, :] + o_ref[rows, :], g_ref[...], b_ref[...])
            o_ref[rows, :] = hn
            ob_ref[rows, :] = hn.astype(BF16)


def _ffn(hb, h, layer, wg, wu, wd, g, b, *, alpha):
    m, d = h.shape
    ff = wg.shape[2]
    tm, tf = _tile(m, 1024), 512
    assert ff % tf == 0
    row = pl.BlockSpec((tm, d), lambda i, f: (i, 0))
    row_once = pl.BlockSpec((tm, d), lambda i, f: (i, 0), pipeline_mode=pl.Buffered(1))
    return pl.pallas_call(
        functools.partial(_ffn_kernel, alpha=alpha),
        grid=(m // tm, ff // tf),
        in_specs=[row, row_once, _lspec(layer, (d, tf), lambda i, f: (0, f)),
                  _lspec(layer, (d, tf), lambda i, f: (0, f)), _lspec(layer, (tf, d), lambda i, f: (f, 0)),
                  _lspec(layer, (1, d)), _lspec(layer, (1, d))],
        out_specs=[row_once, row],
        out_shape=[jax.ShapeDtypeStruct((m, d), F32), jax.ShapeDtypeStruct((m, d), BF16)],
        compiler_params=_params("parallel", "arbitrary"),
        name="ffn",
    )(hb, h, wg, wu, wd, g, b)


def _expansion_matrix(n_in, reps, n_pad):
    r = jnp.arange(n_pad)[:, None]
    c = jnp.arange(n_in * reps)[None, :] // reps
    e = ((r == c) & (r < n_in)).astype(BF16)
    return jnp.concatenate([e, e], axis=0)


def kernel(x_prompt, x_sample, mem_prompt, state_ssd, state_conv, state_pool, cache_mem_k, cache_mem_v, w_in, w_pool, pool_scale, w_pool_br, conv_w, conv_b, dt_bias, a_log, d_skip, ssd_norm_w, w_ssd_br, w_out, ln_mix_g, ln_mix_b, w_mq, w_mk, w_mv, w_mo, ln_mem_g, ln_mem_b, w_ffn_gate, w_ffn_up, w_ffn_down, ln_ffn_g, ln_ffn_b):
    bp, seq, d = x_prompt.shape
    bs = x_sample.shape[0]
    depth = w_in.shape[0]
    _, _, heads, p, n = state_ssd.shape
    di = heads * p
    cd = conv_w.shape[2]
    groups = (cd - di) // (2 * n)
    gn = groups * n
    hpg = heads // groups
    n_mem, mem_heads, mem_hd = cache_mem_k.shape[2:]
    md = mem_heads * mem_hd
    nbuf = state_pool.shape[2]
    kw1 = state_conv.shape[2]
    alpha = (2 * depth) ** 0.25
    assert seq % SSD_CHUNK == 0 and heads <= LANES and LANES % p == 0 and hpg % (LANES // p) == 0

    s3 = d + di + cd
    s4 = s3 + heads

    w_in_t = jnp.transpose(w_in, (0, 2, 1))
    w_dt = jnp.pad(w_in_t[:, s3:s4, :], ((0, 0), (0, LANES - heads), (0, 0)))
    bf = lambda w: w.astype(BF16)
    wpool_b, wpbr_b, wsbr_b, wout_b = bf(w_pool), bf(w_pool_br), bf(w_ssd_br), bf(w_out)
    wq_b, wk_b, wv_b, wo_b = bf(w_mq), bf(w_mk), bf(w_mv), bf(w_mo)
    wg_b, wu_b, wd_b = bf(w_ffn_gate), bf(w_ffn_up), bf(w_ffn_down)
    vec = lambda v: v.astype(F32).reshape(depth, 1, v.shape[-1])
    pad_h = lambda v: jnp.pad(v.astype(F32), ((0, 0), (0, LANES - heads))).reshape(depth, 1, LANES)
    scale, cbias, nw = vec(pool_scale), vec(conv_b), vec(ssd_norm_w)
    bias_p, alog_p = pad_h(dt_bias), pad_h(a_log)
    dskip_x = vec(jnp.repeat(d_skip, p, axis=1))
    mix_g, mix_b, mem_g, mem_b, ffn_g, ffn_b = (vec(v) for v in (ln_mix_g, ln_mix_b, ln_mem_g, ln_mem_b,
                                                                   ln_ffn_g, ln_ffn_b))
    e_group = _expansion_matrix(hpg, p, LANES)
    e_all = _expansion_matrix(heads, p, LANES)

    pool_state_t = jnp.transpose(state_pool, (0, 2, 1, 3))
    conv_state_t = jnp.transpose(state_conv, (0, 2, 1, 3))

    hp, hs = x_prompt.reshape(bp * seq, d), x_sample.reshape(bs, d)
    hp_b, hs_b = hp.astype(BF16), hs.astype(BF16)
    memory_bf = mem_prompt.reshape(bp * n_mem, d).astype(BF16)

    outs = {k: [] for k in ("ssd_p", "conv_p", "pool_p", "mk", "mv")}
    new_ssd_s = new_conv_s = new_pool_s = None
    for l in range(depth):

        def in_proj(hb):
            mm = functools.partial(_mm, hb, layer=l, w_is_nk=True)
            u, zs, xbc = _in_proj_main(hb, w_in_t, l, d=d, di=di, cd=cd)
            dtr = mm(w_dt, name="in_dt")
            gates = mm(w_in_t, col0=s4, n=2 * d, act="sigmoid", out_dtype=BF16, name="in_gates")
            return u, zs, xbc, dtr, gates

        def tail(h, y, gates, a_gated, attn):
            h1, h1b = _merge_proj_ln(y, gates, a_gated, h, l, wsbr_b, wout_b, mix_g, mix_b, alpha=alpha)
            h2, h2b = attn(h1, h1b)
            return _ffn(h2b, h2, l, wg_b, wu_b, wd_b, ffn_g, ffn_b, alpha=alpha)

        u, zs, xbc, dtr, gates = in_proj(hp_b)
        u3, xbc3 = u.reshape(bp, seq, d), xbc.reshape(bp, seq, cd)
        a_gated = _pool_prompt(u3, gates.reshape(bp, seq, 2 * d), l, wpool_b, scale, wpbr_b).reshape(bp * seq, d)
        dt, cum, cumt = _dt_prep(dtr, l, bias_p, alog_p)
        r3 = lambda a: a.reshape(bp, seq, a.shape[-1])
        fill = (depth * bs,) + state_ssd.shape[2:] if l == 0 else None
        y, st, zeros = _ssd_prompt(xbc3, r3(zs), r3(dt), r3(cum), r3(cumt), l, conv_w, cbias, dskip_x, nw, e_group,
                                   di=di, n=n, p=p, groups=groups, zero_fill_shape=fill)
        if zeros is not None:
            new_ssd_s = zeros.reshape(state_ssd.shape)
        mk_p = _mm(memory_bf, wk_b, l, name="mem_k").reshape(bp, n_mem, md)
        mv_p = _mm(memory_bf, wv_b, l, name="mem_v").reshape(bp, n_mem, md)

        def attn_p(h1, h1b):
            o, ob = _attn_prompt(h1b.reshape(bp, seq, d), h1.reshape(bp, seq, d), mk_p, mv_p, l, wq_b, wo_b,
                                 mem_g, mem_b, heads=mem_heads, alpha=alpha)
            return o.reshape(bp * seq, d), ob.reshape(bp * seq, d)

        hp, hp_b = tail(hp, y.reshape(bp * seq, di), gates, a_gated, attn_p)
        outs["ssd_p"].append(st)
        outs["conv_p"].append(xbc3[:, seq - kw1:, :])
        outs["pool_p"].append(u3[:, seq - nbuf:, :])
        outs["mk"].append(mk_p.reshape(bp, n_mem, mem_heads, mem_hd))
        outs["mv"].append(mv_p.reshape(bp, n_mem, mem_heads, mem_hd))

        u, zs, xbc, dtr, gates = in_proj(hs_b)
        a_gated, new_pool_s = _pool_sample(u, gates, pool_state_t, l, new_pool_s, wpool_b, scale, wpbr_b)
        xs, xdt, dae, bm, cm, new_conv_s = _ssd_sample_pre(xbc, dtr, conv_state_t, l, new_conv_s, conv_w, cbias,
                                                           bias_p, alog_p, e_all, di=di, gn=gn)
        new_ssd_s, ys = _ssd_sample_state(state_ssd, l, new_ssd_s, xdt, dae, bm, cm, groups=groups)
        y = _ssd_sample_post(ys, xs, zs, l, dskip_x, nw, groups=groups)

        def attn_s(h1, h1b):
            return _attn_sample(h1b, h1, cache_mem_k, cache_mem_v, l, wq_b, wo_b, mem_g, mem_b, alpha=alpha)

        hs, hs_b = tail(hs, y, gates, a_gated, attn_s)

    stack = lambda k: jnp.stack(outs[k])
    return (hp.reshape(bp, seq, d), hs.reshape(bs, 1, d), stack("ssd_p"), stack("conv_p"), stack("pool_p"),
            stack("mk"), stack("mv"), new_ssd_s, jnp.transpose(new_conv_s, (0, 2, 1, 3)),
            jnp.transpose(new_pool_s, (0, 2, 1, 3)))
```

```python
import functools
import math

import jax
import jax.numpy as jnp
from jax import lax
from jax.experimental import pallas as pl
from jax.experimental.pallas import tpu as pltpu

F32 = jnp.float32
BF16 = jnp.bfloat16

POOL_WINDOWS = (2, 4, 8, 16)
PAST_LEN = 16384
SSD_CHUNK = 128
LN_EPS = 1e-5
RMS_EPS = 1e-5
LANES = 128
VMEM_LIMIT_BYTES = 56 * 1024 * 1024


def _params(*sem):
    return pltpu.CompilerParams(dimension_semantics=sem, vmem_limit_bytes=VMEM_LIMIT_BYTES)


def _silu(x):
    return x * jax.nn.sigmoid(x)


def _softplus(x):
    return jnp.maximum(x, 0.0) + jnp.log(1.0 + jnp.exp(-jnp.abs(x)))


def _layer_norm(x, g, b):
    mu = jnp.mean(x, axis=-1, keepdims=True)
    xc = x - mu
    var = jnp.mean(xc * xc, axis=-1, keepdims=True)
    return xc * lax.rsqrt(var + LN_EPS) * g + b


def _split2(x):
    hi = x.astype(BF16)
    mid = (x - hi.astype(F32)).astype(BF16)
    return hi, mid


def _tile(n, pref):
    t = min(n, pref)
    while n % t:
        t //= 2
    return t


def _lspec(layer, block, index_map=None, **kwargs):
    nd = len(block)
    if index_map is None:
        index_map = lambda *g: (0,) * nd
    return pl.BlockSpec((None,) + tuple(block), lambda *g: (layer,) + tuple(index_map(*g)), **kwargs)


def _mm_kernel(x_ref, w_ref, o_ref, *, act, w_is_nk):
    if w_is_nk:
        acc = lax.dot_general(x_ref[...], w_ref[0].astype(BF16), (((1,), (1,)), ((), ())),
                              preferred_element_type=F32)
    else:
        acc = jnp.dot(x_ref[...], w_ref[...], preferred_element_type=F32)
    if act == "silu":
        acc = _silu(acc)
    elif act == "sigmoid":
        acc = jax.nn.sigmoid(acc)
    o_ref[...] = acc.astype(o_ref.dtype)


def _mm(x, w, layer, *, col0=0, n=None, act=None, out_dtype=F32, w_is_nk=False, name):
    m, k = x.shape
    n = w.shape[1 if w_is_nk else 2] if n is None else n
    if w_is_nk:
        tm, tn = _tile(m, 2048), _tile(n, 512)
        assert col0 % 16 == 0
        w_spec = pl.BlockSpec((pl.Element(1), pl.Element(tn), pl.Element(k)),
                              lambda i, j: (layer, pl.multiple_of(col0 + j * tn, 16), 0))
    else:
        tm, tn = _tile(m, 1024), _tile(n, 1024)
        assert col0 % tn == 0
        w_spec = _lspec(layer, (k, tn), lambda i, j: (0, col0 // tn + j))
    return pl.pallas_call(
        functools.partial(_mm_kernel, act=act, w_is_nk=w_is_nk),
        grid=(m // tm, n // tn),
        in_specs=[pl.BlockSpec((tm, k), lambda i, j: (i, 0)), w_spec],
        out_specs=pl.BlockSpec((tm, tn), lambda i, j: (i, j)),
        out_shape=jax.ShapeDtypeStruct((m, n), out_dtype),
        compiler_params=_params("parallel", "arbitrary"),
        name=name,
    )(x, w)


def _pool_project(pooled_ref, wg_ref, scale_ref, wbr_ref, sg, n_groups, gc):
    for g in range(n_groups):
        cols = slice(g * gc, (g + 1) * gc)
        pg = jnp.dot(pooled_ref[:, cols], wg_ref[g], preferred_element_type=F32) * scale_ref[:, cols]
        pooled_ref[:, cols] = pg.astype(BF16)
    a = jnp.dot(pooled_ref[...], wbr_ref[...], preferred_element_type=F32)
    return a * sg.astype(F32)


def _pool_prompt_kernel(u_ref, wg_ref, scale_ref, wbr_ref, sg_ref, o_ref, carry_ref, pooled_ref, *, tl, gc):
    j = pl.program_id(1)

    @pl.when(j == 0)
    def _():
        carry_ref[...] = jnp.zeros(carry_ref.shape, F32)

    nt = tl // 8
    sub = lax.broadcasted_iota(jnp.int32, (nt, 8, gc), 1)
    pos = j * tl + lax.broadcasted_iota(jnp.int32, (tl, 1), 0)
    for g, w in enumerate(POOL_WINDOWS):
        cols = slice(g * gc, (g + 1) * gc)
        ug = u_ref[0, :, cols]
        acc, s, level = ug, 1, 0
        while s < w:
            a3 = acc.reshape(nt, 8, gc)
            before = jnp.concatenate([carry_ref[level, :, cols][None], a3[:nt - 1]], axis=0)
            carry_ref[level, :, cols] = acc[tl - 8:tl, :]
            shifted = before if s == 8 else pltpu.roll(jnp.where(sub >= 8 - s, before, a3), s, axis=1)
            acc = acc + shifted.reshape(tl, gc)
            s, level = 2 * s, level + 1
        cnt = jnp.minimum(w, pos + 1).astype(F32)
        pooled_ref[:, cols] = (acc / cnt - ug).astype(BF16)
    o_ref[0] = _pool_project(pooled_ref, wg_ref, scale_ref, wbr_ref, sg_ref[0], len(POOL_WINDOWS), gc)


def _pool_prompt(u, sg, layer, wg, scale, wbr):
    b, l, d = u.shape
    _, ng, gc, _ = wg.shape
    tl = _tile(l, 256)
    assert all(w in (1, 2, 4, 8, 16) for w in POOL_WINDOWS) and tl % 8 == 0
    levels = max(POOL_WINDOWS).bit_length() - 1
    tok = pl.BlockSpec((1, tl, d), lambda i, j: (i, j, 0))
    return pl.pallas_call(
        functools.partial(_pool_prompt_kernel, tl=tl, gc=gc),
        grid=(b, l // tl),
        in_specs=[tok, _lspec(layer, (ng, gc, gc)), _lspec(layer, (1, d)), _lspec(layer, (d, d)), tok],
        out_specs=tok,
        out_shape=jax.ShapeDtypeStruct((b, l, d), F32),
        scratch_shapes=[pltpu.VMEM((levels, 8, d), F32), pltpu.VMEM((tl, d), BF16)],
        compiler_params=_params("parallel", "arbitrary"),
        name="pool_prompt",
    )(u, wg, scale, wbr, sg)


def _layer_state_call(kern, *, layer, state, batch_axis, stacked_so_far, grid, tb, other_operands, other_in_specs,
                      other_out_specs, other_out_shapes, scratch_shapes=(), name):
    blk = tuple(tb if a == batch_axis else s for a, s in enumerate(state.shape))[1:]
    spec = _lspec(layer, blk, lambda i: tuple(i if a == batch_axis else 0 for a in range(1, state.ndim)))
    if stacked_so_far is None:
        stacked_so_far = jnp.zeros(state.shape, state.dtype)
    operands = list(other_operands) + [state, stacked_so_far]
    in_specs = list(other_in_specs) + [spec, pl.BlockSpec(memory_space=pl.ANY)]
    return pl.pallas_call(
        kern,
        grid=grid,
        in_specs=in_specs,
        out_specs=list(other_out_specs) + [spec],
        out_shape=list(other_out_shapes) + [jax.ShapeDtypeStruct(state.shape, state.dtype)],
        input_output_aliases={len(operands) - 1: len(other_out_shapes)},
        scratch_shapes=list(scratch_shapes),
        compiler_params=_params("parallel"),
        name=name,
    )(*operands)


def _pool_sample_kernel(u_ref, wg_ref, scale_ref, wbr_ref, sg_ref, buf_ref, stack_hbm_ref,
                        o_ref, nbuf_ref, pooled_ref, *, gc):
    del stack_hbm_ref
    nbuf = buf_ref.shape[0]
    u = u_ref[...]
    for r in range(nbuf - 1):
        nbuf_ref[r] = buf_ref[r + 1]
    nbuf_ref[nbuf - 1] = u
    for g, w in enumerate(POOL_WINDOWS):
        cols = slice(g * gc, (g + 1) * gc)
        ug = u[:, cols]
        s = ug
        for k in range(1, w):
            s = s + buf_ref[nbuf - k, :, cols]
        cnt = float(min(w, PAST_LEN + 1))
        pooled_ref[:, cols] = (s / cnt - ug).astype(BF16)
    o_ref[...] = _pool_project(pooled_ref, wg_ref, scale_ref, wbr_ref, sg_ref[...], len(POOL_WINDOWS), gc)


def _pool_sample(u, sg, pool_state_t, layer, stacked_so_far, wg, scale, wbr):
    bs, d = u.shape
    _, ng, gc, _ = wg.shape
    tb = _tile(bs, 32)
    row = pl.BlockSpec((tb, d), lambda i: (i, 0))
    return _layer_state_call(
        functools.partial(_pool_sample_kernel, gc=gc),
        layer=layer, state=pool_state_t, batch_axis=2, stacked_so_far=stacked_so_far, grid=(bs // tb,), tb=tb,
        other_operands=[u, wg, scale, wbr, sg],
        other_in_specs=[row, _lspec(layer, (ng, gc, gc)), _lspec(layer, (1, d)), _lspec(layer, (d, d)), row],
        other_out_specs=[row], other_out_shapes=[jax.ShapeDtypeStruct((bs, d), F32)],
        scratch_shapes=[pltpu.VMEM((tb, d), BF16)], name="pool_sample")


def _dt_prep_kernel(dtr_ref, bias_ref, alog_ref, dt_ref, cum_ref, cumt_ref):
    c = SSD_CHUNK
    dt_ref[...] = _softplus(dtr_ref[...] + bias_ref[...])
    a = -jnp.exp(alog_ref[...])
    row = lax.broadcasted_iota(jnp.int32, (c, dtr_ref.shape[1]), 0)
    for ch in range(dtr_ref.shape[0] // c):
        rows = slice(ch * c, (ch + 1) * c)
        cum = dt_ref[rows, :] * a
        k = 1
        while k < c:
            cum = cum + jnp.where(row >= k, pltpu.roll(cum, k, axis=0), 0.0)
            k *= 2
        cum_ref[rows, :] = cum
        cumt_ref[rows, :] = cum.T


def _dt_prep(dtr, layer, bias, alog):
    t, hp = dtr.shape
    c = SSD_CHUNK * _tile(t // SSD_CHUNK, 8)
    spec = pl.BlockSpec((c, hp), lambda i: (i, 0))
    return pl.pallas_call(
        _dt_prep_kernel,
        grid=(t // c,),
        in_specs=[spec, _lspec(layer, (1, hp)), _lspec(layer, (1, hp))],
        out_specs=[spec, spec, spec],
        out_shape=[jax.ShapeDtypeStruct((t, hp), F32)] * 3,
        compiler_params=_params("parallel"),
        name="dt_prep",
    )(dtr, bias, alog)


def _conv_silu(x, carry_ref, w_ref, b_ref):
    rows, wd = x.shape
    kw = w_ref.shape[0]
    nt = rows // 8
    last_row = lax.broadcasted_iota(jnp.int32, (nt, 8, wd), 1) == 7
    t = x * w_ref[0:1, :]
    for k in range(1, kw):
        t3 = t.reshape(nt, 8, wd)
        before = jnp.concatenate([carry_ref[k - 1][None], t3[:nt - 1]], axis=0)
        carry_ref[k - 1] = t[rows - 8:rows, :]
        shifted = pltpu.roll(jnp.where(last_row, before, t3), 1, axis=1).reshape(rows, wd)
        t = x * w_ref[k:k + 1, :] + shifted
    return _silu(t + b_ref[...])


def _ssd_prompt_kernel(xs_ref, bm_ref, cm_ref, zs_ref, dt_ref, cum_ref, cumt_ref,
                       wx_ref, wb_ref, wc_ref, bx_ref, bb_ref, bc_ref, dskip_ref, nw_ref, e_ref,
                       y_ref, st_ref, *rest, hpg, p, gps, zero_fill):
    stt_ref, hx_ref, hb_ref, hc_ref = rest[-4:]
    c = pl.program_id(2)
    nc = pl.num_programs(2)
    rows = xs_ref.shape[1]
    n = bm_ref.shape[2] // gps
    gw = hpg * p
    if zero_fill:
        rest[0][...] = jnp.zeros(rest[0].shape, F32)

    @pl.when(c == 0)
    def _():
        for r in (stt_ref, hx_ref, hb_ref, hc_ref):
            r[...] = jnp.zeros(r.shape, F32)

    x_all = _conv_silu(xs_ref[0], hx_ref, wx_ref, bx_ref)
    b_all = _conv_silu(bm_ref[0], hb_ref, wb_ref, bb_ref).astype(BF16)
    c_all = _conv_silu(cm_ref[0], hc_ref, wc_ref, bc_ref).astype(BF16)
    causal = lax.broadcasted_iota(jnp.int32, (rows, rows), 0) >= lax.broadcasted_iota(jnp.int32, (rows, rows), 1)
    lane = lax.broadcasted_iota(jnp.int32, (rows, LANES), 1)
    heads_per_tile = LANES // p
    lane_masks = [jnp.where(lane // p == r, 1.0, 0.0).astype(BF16) for r in range(heads_per_tile)]

    for gi in range(gps):
        g = pl.program_id(1) * gps + gi
        gcols = slice(gi * gw, (gi + 1) * gw)
        x = x_all[:, gcols]
        bmat = b_all[:, gi * n:(gi + 1) * n]
        cmat = c_all[:, gi * n:(gi + 1) * n]
        shift = (LANES - g * hpg) % LANES
        dtg = pltpu.roll(dt_ref[0], shift, axis=1)
        cumg = pltpu.roll(cum_ref[0], shift, axis=1)
        cumtg = cumt_ref[0, pl.ds(pl.multiple_of(g * hpg, hpg), hpg), :]
        last = cumg[rows - 1:rows, :]
        stack = jnp.concatenate([dtg, jnp.exp(cumg), jnp.exp(last - cumg)], axis=0)
        hi, mid = _split2(stack)
        expanded = jnp.dot(jnp.concatenate([hi, mid], axis=1), e_ref[...], preferred_element_type=F32)
        dt_x, ecum_x, ws_x = expanded[0:rows], expanded[rows:2 * rows], expanded[2 * rows:3 * rows]

        xdt = x * dt_x
        xdt_b = xdt.astype(BF16)
        xdtw_b = (xdt * ws_x).astype(BF16)
        cb = lax.dot_general(cmat, bmat, (((1,), (1,)), ((), ())), preferred_element_type=F32)
        st_old = stt_ref[:, gcols]
        y = jnp.dot(cmat, st_old.astype(BF16), preferred_element_type=F32) * ecum_x
        stt_ref[:, gcols] = st_old * ecum_x[rows - 1:rows, :] + lax.dot_general(
            bmat, xdtw_b, (((0,), (0,)), ((), ())), preferred_element_type=F32)

        cum2 = cumg * math.log2(math.e)
        cumt2 = cumtg * math.log2(math.e)
        diag = []
        for q in range(hpg // heads_per_tile):
            xq = xdt_b[:, q * LANES:(q + 1) * LANES]
            ms, xr = [], []
            for r in range(heads_per_tile):
                h = q * heads_per_tile + r
                dec = jnp.exp2(cum2[:, h:h + 1] - cumt2[h:h + 1, :])
                ms.append(jnp.where(causal, cb * dec, 0.0).astype(BF16))
                xr.append(xq * lane_masks[r])
            diag.append(jnp.dot(jnp.concatenate(ms, axis=1), jnp.concatenate(xr, axis=0),
                                preferred_element_type=F32))
        y = y + jnp.concatenate(diag, axis=1) + dskip_ref[:, gcols] * x
        y = y * zs_ref[0, :, gcols].astype(F32)
        y = y * lax.rsqrt(jnp.mean(y * y, axis=-1, keepdims=True) + RMS_EPS) * nw_ref[:, gcols]
        y_ref[0, :, gcols] = y.astype(y_ref.dtype)

    @pl.when(c == nc - 1)
    def _():
        st_ref[0] = stt_ref[...].T.reshape(st_ref.shape[1:])


def _ssd_prompt(xbc, zs, dt, cum, cumt, layer, conv_w, conv_b, dskip_x, norm_w, e_mat, *, di, n, p, groups,
                zero_fill_shape=None):
    b, l, cd = xbc.shape
    c = SSD_CHUNK
    heads = di // p
    hpg = heads // groups
    gps = next(g for g in (8, 4, 2, 1) if groups % g == 0)
    gw = hpg * p * gps
    nn = n * gps
    hp = dt.shape[-1]
    xoff, boff, coff = 0, di // nn, (di + groups * n) // nn

    def seg(width, off_blocks):
        return pl.BlockSpec((1, c, width), lambda i, g, k: (i, k, off_blocks + g))

    def wseg(rows, width, off_blocks):
        return _lspec(layer, (rows, width), lambda i, g, k: (0, off_blocks + g))

    small = pl.BlockSpec((1, c, hp), lambda i, g, k: (i, k, 0))
    kw = conv_w.shape[1]
    ng, nc = groups // gps, l // c
    extra_specs, extra_shapes = [], []
    if zero_fill_shape is not None and zero_fill_shape[0] % (b * ng * nc):
        zero_fill_shape = None
    if zero_fill_shape is not None:
        per_step = zero_fill_shape[0] // (b * ng * nc)
        tail = tuple(zero_fill_shape[1:])
        extra_specs = [pl.BlockSpec((per_step,) + tail,
                                    lambda i, g, k: ((i * ng + g) * nc + k,) + (0,) * len(tail))]
        extra_shapes = [jax.ShapeDtypeStruct(tuple(zero_fill_shape), F32)]
    outs = pl.pallas_call(
        functools.partial(_ssd_prompt_kernel, hpg=hpg, p=p, gps=gps, zero_fill=zero_fill_shape is not None),
        grid=(b, ng, nc),
        in_specs=[
            seg(gw, xoff), seg(nn, boff), seg(nn, coff), seg(gw, 0), small, small, small,
            wseg(kw, gw, xoff), wseg(kw, nn, boff), wseg(kw, nn, coff),
            wseg(1, gw, xoff), wseg(1, nn, boff), wseg(1, nn, coff),
            wseg(1, gw, 0), wseg(1, gw, 0),
            pl.BlockSpec(e_mat.shape, lambda i, g, k: (0, 0)),
        ],
        out_specs=[
            pl.BlockSpec((1, c, gw), lambda i, g, k: (i, k, g)),
            pl.BlockSpec((1, hpg * gps, p, n), lambda i, g, k: (i, g, 0, 0)),
        ] + extra_specs,
        out_shape=[jax.ShapeDtypeStruct((b, l, di), BF16), jax.ShapeDtypeStruct((b, heads, p, n), F32)]
        + extra_shapes,
        scratch_shapes=[
            pltpu.VMEM((n, gw), F32),
            pltpu.VMEM((kw - 1, 8, gw), F32), pltpu.VMEM((kw - 1, 8, nn), F32), pltpu.VMEM((kw - 1, 8, nn), F32),
        ],
        compiler_params=_params("parallel", "parallel", "arbitrary"),
        name="ssd_prompt",
    )(xbc, xbc, xbc, zs, dt, cum, cumt, conv_w, conv_w, conv_w, conv_b, conv_b, conv_b, dskip_x, norm_w, e_mat)
    return (outs[0], outs[1], outs[2] if extra_shapes else None)


def _ssd_sample_pre_kernel(xbc_ref, w_ref, b_ref, dtr_ref, bias_ref, alog_ref, e_ref, cbuf_ref, stack_hbm_ref,
                           xs_ref, xdt_ref, dae_ref, bm_ref, cm_ref, ncbuf_ref, *, di, gn):
    del stack_hbm_ref
    kw = w_ref.shape[0]
    xbc = xbc_ref[...]
    acc = b_ref[...]
    for k in range(kw - 1):
        acc = acc + cbuf_ref[k] * w_ref[k:k + 1, :]
    acc = _silu(acc + xbc * w_ref[kw - 1:kw, :])
    for k in range(kw - 2):
        ncbuf_ref[k] = cbuf_ref[k + 1]
    ncbuf_ref[kw - 2] = xbc
    xs = acc[:, 0:di]
    dt = _softplus(dtr_ref[...] + bias_ref[...])
    dae_ref[...] = jnp.exp(dt * (-jnp.exp(alog_ref[...])))
    hi, mid = _split2(dt)
    dt_x = jnp.dot(jnp.concatenate([hi, mid], axis=1), e_ref[...], preferred_element_type=F32)
    xs_ref[...] = xs
    xdt_ref[...] = xs * dt_x
    bm_ref[...] = acc[:, di:di + gn]
    cm_ref[...] = acc[:, di + gn:di + 2 * gn]


def _ssd_sample_pre(xbc, dtr, conv_state_t, layer, stacked_so_far, conv_w, conv_b, bias, alog, e_mat, *, di, gn):
    bs, cd = xbc.shape
    hp = dtr.shape[1]
    kw = conv_w.shape[1]
    tb = _tile(bs, 32)
    row = lambda w: pl.BlockSpec((tb, w), lambda i: (i, 0))
    return _layer_state_call(
        functools.partial(_ssd_sample_pre_kernel, di=di, gn=gn),
        layer=layer, state=conv_state_t, batch_axis=2, stacked_so_far=stacked_so_far, grid=(bs // tb,), tb=tb,
        other_operands=[xbc, conv_w, conv_b, dtr, bias, alog, e_mat],
        other_in_specs=[row(cd), _lspec(layer, (kw, cd)), _lspec(layer, (1, cd)), row(hp), _lspec(layer, (1, hp)),
                        _lspec(layer, (1, hp)), pl.BlockSpec(e_mat.shape, lambda i: (0, 0))],
        other_out_specs=[row(di), row(di), row(hp), row(gn), row(gn)],
        other_out_shapes=[jax.ShapeDtypeStruct((bs, di), F32)] * 2 + [jax.ShapeDtypeStruct((bs, hp), F32)]
        + [jax.ShapeDtypeStruct((bs, gn), F32)] * 2,
        name="ssd_sample_pre")


def _ssd_sample_state_kernel(dae_ref, xdt_ref, bm_ref, cm_ref, s_ref, stack_hbm_ref, y_ref, so_ref, *, groups):
    del stack_hbm_ref
    tb, heads, p, n = s_ref.shape
    di = heads * p
    gw = di // groups
    grow = lax.broadcasted_iota(jnp.int32, (groups, di), 0)
    gcol = lax.broadcasted_iota(jnp.int32, (groups, di), 1) // gw
    own = grow == gcol
    fill = (-groups) % 16
    for b in range(tb):
        seq = pl.program_id(0) * tb + b
        xg = jnp.where(own, jnp.broadcast_to(xdt_ref[b], (groups, di)), 0.0)
        bmat = bm_ref[b]
        if fill:
            xg = jnp.concatenate([xg, jnp.zeros((fill, di), F32)], axis=0)
            bmat = jnp.concatenate([bmat, jnp.zeros((fill, n), F32)], axis=0)
        upd = lax.dot_general(xg.astype(BF16), bmat.astype(BF16), (((0,), (0,)), ((), ())),
                              preferred_element_type=F32)
        for h in range(heads):
            so_ref[b, h] = s_ref[b, h] * dae_ref[seq, h] + upd[h * p:(h + 1) * p, :]
        s_new = so_ref[b].reshape(di, n).astype(BF16)
        yg = lax.dot_general(cm_ref[b].astype(BF16), s_new, (((1,), (1,)), ((), ())),
                             preferred_element_type=F32)
        y_ref[b] = jnp.sum(jnp.where(own, yg, 0.0), axis=0, keepdims=True)


def _ssd_sample_state(ssd_state, layer, stacked_so_far, xdt, dae, bm, cm, *, groups):
    _, bs, heads, p, n = ssd_state.shape
    di = heads * p
    tb = _tile(bs, 2)
    xdt3 = xdt.reshape(bs, 1, di)
    bm3, cm3 = bm.reshape(bs, groups, n), cm.reshape(bs, groups, n)
    rspec = pl.BlockSpec((tb, 1, di), lambda i: (i, 0, 0))
    gspec = pl.BlockSpec((tb, groups, n), lambda i: (i, 0, 0))
    y, stacked = _layer_state_call(
        functools.partial(_ssd_sample_state_kernel, groups=groups),
        layer=layer, state=ssd_state, batch_axis=1, stacked_so_far=stacked_so_far, grid=(bs // tb,), tb=tb,
        other_operands=[dae, xdt3, bm3, cm3],
        other_in_specs=[pl.BlockSpec(memory_space=pltpu.SMEM), rspec, gspec, gspec],
        other_out_specs=[rspec], other_out_shapes=[jax.ShapeDtypeStruct((bs, 1, di), F32)],
        name="ssd_sample_state")
    return stacked, y.reshape(bs, di)


def _ssd_sample_post_kernel(y_ref, xs_ref, zs_ref, dskip_ref, nw_ref, o_ref, *, groups):
    y = (y_ref[...] + dskip_ref[...] * xs_ref[...]) * zs_ref[...].astype(F32)
    gw = y.shape[1] // groups
    for g in range(groups):
        cols = slice(g * gw, (g + 1) * gw)
        yg = y[:, cols]
        yg = yg * lax.rsqrt(jnp.mean(yg * yg, axis=-1, keepdims=True) + RMS_EPS) * nw_ref[:, cols]
        o_ref[:, cols] = yg.astype(o_ref.dtype)


def _ssd_sample_post(y, xs, zs, layer, dskip_x, norm_w, *, groups):
    bs, di = y.shape
    full = pl.BlockSpec((bs, di), lambda i: (0, 0))
    return pl.pallas_call(
        functools.partial(_ssd_sample_post_kernel, groups=groups),
        grid=(1,),
        in_specs=[full, full, full, _lspec(layer, (1, di)), _lspec(layer, (1, di))],
        out_specs=full,
        out_shape=jax.ShapeDtypeStruct(y.shape, BF16),
        compiler_params=_params("arbitrary"),
        name="ssd_sample_post",
    )(y, xs, zs, dskip_x, norm_w)


def _merge_proj_ln_kernel(y_ref, sg_ref, a_ref, x_ref, wb_ref, wo_ref, g_ref, b_ref, o_ref, ob_ref, *, alpha):
    tm = y_ref.shape[0]
    halves = 2 if tm % 16 == 0 else 1
    for r in range(halves):
        rows = slice(r * tm // halves, (r + 1) * tm // halves)
        bb = jnp.dot(y_ref[rows, :], wb_ref[...], preferred_element_type=F32)
        merged = (a_ref[rows, :] + sg_ref[rows, :].astype(F32) * bb).astype(BF16)
        f = jnp.dot(merged, wo_ref[...], preferred_element_type=F32)
        h = _layer_norm(alpha * x_ref[rows, :] + f, g_ref[...], b_ref[...])
        o_ref[rows, :] = h
        ob_ref[rows, :] = h.astype(BF16)


def _merge_proj_ln(y, sg, a, x, layer, wb, wo, g, b, *, alpha):
    m, k = y.shape
    d = wo.shape[2]
    tm = _tile(m, 256)
    row = lambda width: pl.BlockSpec((tm, width), lambda i: (i, 0))
    once = pl.Buffered(1)
    return pl.pallas_call(
        functools.partial(_merge_proj_ln_kernel, alpha=alpha),
        grid=(m // tm,),
        in_specs=[row(k), pl.BlockSpec((tm, d), lambda i: (i, 1)),
                  row(d), row(d), _lspec(layer, (k, d), pipeline_mode=once),
                  _lspec(layer, (d, d), pipeline_mode=once), _lspec(layer, (1, d)), _lspec(layer, (1, d))],
        out_specs=[row(d), row(d)],
        out_shape=[jax.ShapeDtypeStruct((m, d), F32), jax.ShapeDtypeStruct((m, d), BF16)],
        compiler_params=_params("parallel"),
        name="merge_proj_ln",
    )(y, sg, a, x, wb, wo, g, b)


def _attn_prompt_kernel(hb_ref, h_ref, k_ref, v_ref, wq_ref, wo_ref, g_ref, b_ref, o_ref, ob_ref, *,
                        heads, alpha):
    q = jnp.dot(hb_ref[0], wq_ref[...], preferred_element_type=F32)
    hd = q.shape[1] // heads
    scale = 1.0 / math.sqrt(hd)
    kb = k_ref[0].astype(BF16)
    vb = v_ref[0].astype(BF16)
    outs = []
    for h in range(heads):
        cols = slice(h * hd, (h + 1) * hd)
        s = lax.dot_general(q[:, cols].astype(BF16), kb[:, cols], (((1,), (1,)), ((), ())),
                            preferred_element_type=F32) * scale
        e = jnp.exp(s - jnp.max(s, axis=-1, keepdims=True))
        pr = e / jnp.sum(e, axis=-1, keepdims=True)
        outs.append(jnp.dot(pr.astype(BF16), vb[:, cols], preferred_element_type=F32))
    o = jnp.concatenate(outs, axis=1).astype(BF16)
    ca = jnp.dot(o, wo_ref[...], preferred_element_type=F32)
    hn = _layer_norm(alpha * h_ref[0] + ca, g_ref[...], b_ref[...])
    o_ref[0] = hn
    ob_ref[0] = hn.astype(BF16)


def _attn_prompt(hb, h, mk, mv, layer, wq, wo, g, b, *, heads, alpha):
    bsz, l, d = h.shape
    nm, md = mk.shape[1], mk.shape[2]
    tq = _tile(l, 512)
    row = pl.BlockSpec((1, tq, d), lambda i, j: (i, j, 0))
    mem = pl.BlockSpec((1, nm, md), lambda i, j: (i, 0, 0))
    return pl.pallas_call(
        functools.partial(_attn_prompt_kernel, heads=heads, alpha=alpha),
        grid=(bsz, l // tq),
        in_specs=[row, row, mem, mem, _lspec(layer, (d, md)), _lspec(layer, (md, d)), _lspec(layer, (1, d)),
                  _lspec(layer, (1, d))],
        out_specs=[row, row],
        out_shape=[jax.ShapeDtypeStruct((bsz, l, d), F32), jax.ShapeDtypeStruct((bsz, l, d), BF16)],
        compiler_params=_params("parallel", "parallel"),
        name="attn_prompt",
    )(hb, h, mk, mv, wq, wo, g, b)


def _attn_sample_kernel(hb_ref, h_ref, k_ref, v_ref, wq_ref, wo_ref, g_ref, b_ref, o_ref, ob_ref, att_ref, *,
                        heads, alpha):
    tb = hb_ref.shape[0]
    rows, hd = k_ref.shape[1], k_ref.shape[2]
    reps = 8 // heads
    scale = 1.0 / math.sqrt(hd)
    hb = hb_ref[...]
    qs = [jnp.dot(hb, wq_ref[:, h * hd:(h + 1) * hd], preferred_element_type=F32) for h in range(heads)]

    def over_copies(x, op):
        out = x
        for r in range(1, reps):
            out = op(out, pltpu.roll(x, r * heads, axis=0))
        return out

    for bi in range(tb):
        q8 = jnp.concatenate([q[bi:bi + 1, :] for q in qs] * reps, axis=0)
        k3 = k_ref[bi].reshape(rows // 8, 8, hd)
        v3 = v_ref[bi].reshape(rows // 8, 8, hd)
        s = jnp.sum(k3 * q8[None], axis=-1, keepdims=True) * scale
        m = over_copies(jnp.max(s, axis=0), jnp.maximum)
        e = jnp.exp(s - m[None])
        den = over_copies(jnp.sum(e, axis=0), jnp.add)
        o8 = over_copies(jnp.sum((e / den[None]) * v3, axis=0), jnp.add)
        att_ref[bi] = o8[0:heads, :]
    ca = None
    for h in range(heads):
        t = jnp.dot(att_ref[:, h, :].astype(BF16), wo_ref[h * hd:(h + 1) * hd, :], preferred_element_type=F32)
        ca = t if ca is None else ca + t
    hn = _layer_norm(alpha * h_ref[...] + ca, g_ref[...], b_ref[...])
    o_ref[...] = hn
    ob_ref[...] = hn.astype(BF16)


def _attn_sample(hb, h, cache_k, cache_v, layer, wq, wo, g, b, *, alpha):
    bs, d = h.shape
    depth, _, nm, heads, hd = cache_k.shape
    assert 8 % heads == 0
    md = heads * hd
    ck = cache_k.reshape(depth, bs, nm * heads, hd)
    cv = cache_v.reshape(depth, bs, nm * heads, hd)
    tb = _tile(bs, 8)
    row = pl.BlockSpec((tb, d), lambda i: (i, 0))
    mem = _lspec(layer, (tb, nm * heads, hd), lambda i: (i, 0, 0))
    return pl.pallas_call(
        functools.partial(_attn_sample_kernel, heads=heads, alpha=alpha),
        grid=(bs // tb,),
        in_specs=[row, row, mem, mem, _lspec(layer, (d, md)), _lspec(layer, (md, d)), _lspec(layer, (1, d)),
                  _lspec(layer, (1, d))],
        out_specs=[row, row],
        out_shape=[jax.ShapeDtypeStruct((bs, d), F32), jax.ShapeDtypeStruct((bs, d), BF16)],
        scratch_shapes=[pltpu.VMEM((tb, heads, hd), F32)],
        compiler_params=_params("parallel"),
        name="attn_sample",
    )(hb, h, ck, cv, wq, wo, g, b)


def _ffn_kernel(hb_ref, h_ref, wg_ref, wu_ref, wd_ref, g_ref, b_ref, o_ref, ob_ref, *, alpha):
    f = pl.program_id(1)
    tm, d = o_ref.shape

    @pl.when(f == 0)
    def _():
        o_ref[...] = jnp.zeros(o_ref.shape, F32)

    x = hb_ref[...]
    gate = jnp.dot(x, wg_ref[...], preferred_element_type=F32)
    up = jnp.dot(x, wu_ref[...], preferred_element_type=F32)
    act = (_silu(gate) * up).astype(BF16)
    for c in range(2):
        cols = slice(c * d // 2, (c + 1) * d // 2)
        o_ref[:, cols] += jnp.dot(act, wd_ref[:, cols], preferred_element_type=F32)

    @pl.when(f == pl.num_programs(1) - 1)
    def _():
        for r in range(4):
            rows = slice(r * tm // 4, (r + 1) * tm // 4)
            hn = _layer_norm(alpha * h_ref[rows, :] + o_ref[rows, :], g_ref[...], b_ref[...])
            o_ref[rows, :] = hn
            ob_ref[rows, :] = hn.astype(BF16)


def _ffn(hb, h, layer, wg, wu, wd, g, b, *, alpha):
    m, d = h.shape
    ff = wg.shape[2]
    tm, tf = _tile(m, 1024), 512
    assert ff % tf == 0
    row = pl.BlockSpec((tm, d), lambda i, f: (i, 0))
    row_once = pl.BlockSpec((tm, d), lambda i, f: (i, 0), pipeline_mode=pl.Buffered(1))
    return pl.pallas_call(
        functools.partial(_ffn_kernel, alpha=alpha),
        grid=(m // tm, ff // tf),
        in_specs=[row, row_once, _lspec(layer, (d, tf), lambda i, f: (0, f)),
                  _lspec(layer, (d, tf), lambda i, f: (0, f)), _lspec(layer, (tf, d), lambda i, f: (f, 0)),
                  _lspec(layer, (1, d)), _lspec(layer, (1, d))],
        out_specs=[row_once, row],
        out_shape=[jax.ShapeDtypeStruct((m, d), F32), jax.ShapeDtypeStruct((m, d), BF16)],
        compiler_params=_params("parallel", "arbitrary"),
        name="ffn",
    )(hb, h, wg, wu, wd, g, b)


def _expansion_matrix(n_in, reps, n_pad):
    r = jnp.arange(n_pad)[:, None]
    c = jnp.arange(n_in * reps)[None, :] // reps
    e = ((r == c) & (r < n_in)).astype(BF16)
    return jnp.concatenate([e, e], axis=0)


def kernel(x_prompt, x_sample, mem_prompt, state_ssd, state_conv, state_pool, cache_mem_k, cache_mem_v, w_in, w_pool, pool_scale, w_pool_br, conv_w, conv_b, dt_bias, a_log, d_skip, ssd_norm_w, w_ssd_br, w_out, ln_mix_g, ln_mix_b, w_mq, w_mk, w_mv, w_mo, ln_mem_g, ln_mem_b, w_ffn_gate, w_ffn_up, w_ffn_down, ln_ffn_g, ln_ffn_b):
    bp, seq, d = x_prompt.shape
    bs = x_sample.shape[0]
    depth = w_in.shape[0]
    _, _, heads, p, n = state_ssd.shape
    di = heads * p
    cd = conv_w.shape[2]
    groups = (cd - di) // (2 * n)
    gn = groups * n
    hpg = heads // groups
    n_mem, mem_heads, mem_hd = cache_mem_k.shape[2:]
    md = mem_heads * mem_hd
    nbuf = state_pool.shape[2]
    kw1 = state_conv.shape[2]
    alpha = (2 * depth) ** 0.25
    assert seq % SSD_CHUNK == 0 and heads <= LANES and LANES % p == 0 and hpg % (LANES // p) == 0

    s3 = d + di + cd
    s4 = s3 + heads

    w_in_t = jnp.transpose(w_in, (0, 2, 1))
    w_dt = jnp.pad(w_in_t[:, s3:s4, :], ((0, 0), (0, LANES - heads), (0, 0)))
    bf = lambda w: w.astype(BF16)
    wpool_b, wpbr_b, wsbr_b, wout_b = bf(w_pool), bf(w_pool_br), bf(w_ssd_br), bf(w_out)
    wq_b, wk_b, wv_b, wo_b = bf(w_mq), bf(w_mk), bf(w_mv), bf(w_mo)
    wg_b, wu_b, wd_b = bf(w_ffn_gate), bf(w_ffn_up), bf(w_ffn_down)
    vec = lambda v: v.astype(F32).reshape(depth, 1, v.shape[-1])
    pad_h = lambda v: jnp.pad(v.astype(F32), ((0, 0), (0, LANES - heads))).reshape(depth, 1, LANES)
    scale, cbias, nw = vec(pool_scale), vec(conv_b), vec(ssd_norm_w)
    bias_p, alog_p = pad_h(dt_bias), pad_h(a_log)
    dskip_x = vec(jnp.repeat(d_skip, p, axis=1))
    mix_g, mix_b, mem_g, mem_b, ffn_g, ffn_b = (vec(v) for v in (ln_mix_g, ln_mix_b, ln_mem_g, ln_mem_b,
                                                                   ln_ffn_g, ln_ffn_b))
    e_group = _expansion_matrix(hpg, p, LANES)
    e_all = _expansion_matrix(heads, p, LANES)

    pool_state_t = jnp.transpose(state_pool, (0, 2, 1, 3))
    conv_state_t = jnp.transpose(state_conv, (0, 2, 1, 3))

    hp, hs = x_prompt.reshape(bp * seq, d), x_sample.reshape(bs, d)
    hp_b, hs_b = hp.astype(BF16), hs.astype(BF16)
    memory_bf = mem_prompt.reshape(bp * n_mem, d).astype(BF16)

    outs = {k: [] for k in ("ssd_p", "conv_p", "pool_p", "mk", "mv")}
    new_ssd_s = new_conv_s = new_pool_s = None
    for l in range(depth):

        def in_proj(hb):
            mm = functools.partial(_mm, hb, layer=l, w_is_nk=True)
            u = mm(w_in_t, col0=0, n=d, name="in_u")
            zs = mm(w_in_t, col0=d, n=di, act="silu", out_dtype=BF16, name="in_z")
            xbc = mm(w_in_t, col0=d + di, n=cd, name="in_xbc")
            dtr = mm(w_dt, name="in_dt")
            gates = mm(w_in_t, col0=s4, n=2 * d, act="sigmoid", out_dtype=BF16, name="in_gates")
            return u, zs, xbc, dtr, gates

        def tail(h, y, gates, a_gated, attn):
            h1, h1b = _merge_proj_ln(y, gates, a_gated, h, l, wsbr_b, wout_b, mix_g, mix_b, alpha=alpha)
            h2, h2b = attn(h1, h1b)
            return _ffn(h2b, h2, l, wg_b, wu_b, wd_b, ffn_g, ffn_b, alpha=alpha)

        u, zs, xbc, dtr, gates = in_proj(hp_b)
        u3, xbc3 = u.reshape(bp, seq, d), xbc.reshape(bp, seq, cd)
        a_gated = _pool_prompt(u3, gates.reshape(bp, seq, 2 * d), l, wpool_b, scale, wpbr_b).reshape(bp * seq, d)
        dt, cum, cumt = _dt_prep(dtr, l, bias_p, alog_p)
        r3 = lambda a: a.reshape(bp, seq, a.shape[-1])
        fill = (depth * bs,) + state_ssd.shape[2:] if l == 0 else None
        y, st, zeros = _ssd_prompt(xbc3, r3(zs), r3(dt), r3(cum), r3(cumt), l, conv_w, cbias, dskip_x, nw, e_group,
                                   di=di, n=n, p=p, groups=groups, zero_fill_shape=fill)
        if zeros is not None:
            new_ssd_s = zeros.reshape(state_ssd.shape)
        mk_p = _mm(memory_bf, wk_b, l, name="mem_k").reshape(bp, n_mem, md)
        mv_p = _mm(memory_bf, wv_b, l, name="mem_v").reshape(bp, n_mem, md)

        def attn_p(h1, h1b):
            o, ob = _attn_prompt(h1b.reshape(bp, seq, d), h1.reshape(bp, seq, d), mk_p, mv_p, l, wq_b, wo_b,
                                 mem_g, mem_b, heads=mem_heads, alpha=alpha)
            return o.reshape(bp * seq, d), ob.reshape(bp * seq, d)

        hp, hp_b = tail(hp, y.reshape(bp * seq, di), gates, a_gated, attn_p)
        outs["ssd_p"].append(st)
        outs["conv_p"].append(xbc3[:, seq - kw1:, :])
        outs["pool_p"].append(u3[:, seq - nbuf:, :])
        outs["mk"].append(mk_p.reshape(bp, n_mem, mem_heads, mem_hd))
        outs["mv"].append(mv_p.reshape(bp, n_mem, mem_heads, mem_hd))

        u, zs, xbc, dtr, gates = in_proj(hs_b)
        a_gated, new_pool_s = _pool_sample(u, gates, pool_state_t, l, new_pool_s, wpool_b, scale, wpbr_b)
        xs, xdt, dae, bm, cm, new_conv_s = _ssd_sample_pre(xbc, dtr, conv_state_t, l, new_conv_s, conv_w, cbias,
                                                           bias_p, alog_p, e_all, di=di, gn=gn)
        new_ssd_s, ys = _ssd_sample_state(state_ssd, l, new_ssd_s, xdt, dae, bm, cm, groups=groups)
        y = _ssd_sample_post(ys, xs, zs, l, dskip_x, nw, groups=groups)

        def attn_s(h1, h1b):
            return _attn_sample(h1b, h1, cache_mem_k, cache_mem_v, l, wq_b, wo_b, mem_g, mem_b, alpha=alpha)

        hs, hs_b = tail(hs, y, gates, a_gated, attn_s)

    stack = lambda k: jnp.stack(outs[k])
    return (hp.reshape(bp, seq, d), hs.reshape(bs, 1, d), stack("ssd_p"), stack("conv_p"), stack("pool_p"),
            stack("mk"), stack("mv"), new_ssd_s, jnp.transpose(new_conv_s, (0, 2, 1, 3)),
            jnp.transpose(new_pool_s, (0, 2, 1, 3)))
```

```python
import functools
import math

import jax
import jax.numpy as jnp
from jax import lax
from jax.experimental import pallas as pl
from jax.experimental.pallas import tpu as pltpu

F32 = jnp.float32
BF16 = jnp.bfloat16

POOL_WINDOWS = (2, 4, 8, 16)
PAST_LEN = 16384
SSD_CHUNK = 128
LN_EPS = 1e-5
RMS_EPS = 1e-5
LANES = 128
VMEM_LIMIT_BYTES = 56 * 1024 * 1024


def _params(*sem):
    return pltpu.CompilerParams(dimension_semantics=sem, vmem_limit_bytes=VMEM_LIMIT_BYTES)


def _silu(x):
    return x * jax.nn.sigmoid(x)


def _softplus(x):
    return jnp.maximum(x, 0.0) + jnp.log(1.0 + jnp.exp(-jnp.abs(x)))


def _layer_norm(x, g, b):
    mu = jnp.mean(x, axis=-1, keepdims=True)
    xc = x - mu
    var = jnp.mean(xc * xc, axis=-1, keepdims=True)
    return xc * lax.rsqrt(var + LN_EPS) * g + b


def _split2(x):
    hi = x.astype(BF16)
    mid = (x - hi.astype(F32)).astype(BF16)
    return hi, mid


def _tile(n, pref):
    t = min(n, pref)
    while n % t:
        t //= 2
    return t


def _lspec(layer, block, index_map=None, **kwargs):
    nd = len(block)
    if index_map is None:
        index_map = lambda *g: (0,) * nd
    return pl.BlockSpec((None,) + tuple(block), lambda *g: (layer,) + tuple(index_map(*g)), **kwargs)


def _mm_kernel(x_ref, w_ref, o_ref, *, act, w_is_nk):
    if w_is_nk:
        acc = lax.dot_general(x_ref[...], w_ref[0].astype(BF16), (((1,), (1,)), ((), ())),
                              preferred_element_type=F32)
    else:
        acc = jnp.dot(x_ref[...], w_ref[...], preferred_element_type=F32)
    if act == "silu":
        acc = _silu(acc)
    elif act == "sigmoid":
        acc = jax.nn.sigmoid(acc)
    o_ref[...] = acc.astype(o_ref.dtype)


def _mm(x, w, layer, *, col0=0, n=None, act=None, out_dtype=F32, w_is_nk=False, name):
    m, k = x.shape
    n = w.shape[1 if w_is_nk else 2] if n is None else n
    if w_is_nk:
        tm = _tile(m, 2048)
        tn = _tile(n, 512 if tm > 512 else 1024)
        assert col0 % 16 == 0
        w_spec = pl.BlockSpec((pl.Element(1), pl.Element(tn), pl.Element(k)),
                              lambda i, j: (layer, pl.multiple_of(col0 + j * tn, 16), 0))
    else:
        tm, tn = _tile(m, 1024), _tile(n, 1024)
        assert col0 % tn == 0
        w_spec = _lspec(layer, (k, tn), lambda i, j: (0, col0 // tn + j))
    return pl.pallas_call(
        functools.partial(_mm_kernel, act=act, w_is_nk=w_is_nk),
        grid=(m // tm, n // tn),
        in_specs=[pl.BlockSpec((tm, k), lambda i, j: (i, 0)), w_spec],
        out_specs=pl.BlockSpec((tm, tn), lambda i, j: (i, j)),
        out_shape=jax.ShapeDtypeStruct((m, n), out_dtype),
        compiler_params=_params("parallel", "arbitrary"),
        name=name,
    )(x, w)


def _pool_project(pooled_ref, wg_ref, scale_ref, wbr_ref, sg, n_groups, gc):
    for g in range(n_groups):
        cols = slice(g * gc, (g + 1) * gc)
        pg = jnp.dot(pooled_ref[:, cols], wg_ref[g], preferred_element_type=F32) * scale_ref[:, cols]
        pooled_ref[:, cols] = pg.astype(BF16)
    a = jnp.dot(pooled_ref[...], wbr_ref[...], preferred_element_type=F32)
    return a * sg.astype(F32)


def _pool_prompt_kernel(u_ref, wg_ref, scale_ref, wbr_ref, sg_ref, o_ref, carry_ref, pooled_ref, *, tl, gc):
    j = pl.program_id(1)

    @pl.when(j == 0)
    def _():
        carry_ref[...] = jnp.zeros(carry_ref.shape, F32)

    nt = tl // 8
    sub = lax.broadcasted_iota(jnp.int32, (nt, 8, gc), 1)
    pos = j * tl + lax.broadcasted_iota(jnp.int32, (tl, 1), 0)
    for g, w in enumerate(POOL_WINDOWS):
        cols = slice(g * gc, (g + 1) * gc)
        ug = u_ref[0, :, cols]
        acc, s, level = ug, 1, 0
        while s < w:
            a3 = acc.reshape(nt, 8, gc)
            before = jnp.concatenate([carry_ref[level, :, cols][None], a3[:nt - 1]], axis=0)
            carry_ref[level, :, cols] = acc[tl - 8:tl, :]
            shifted = before if s == 8 else pltpu.roll(jnp.where(sub >= 8 - s, before, a3), s, axis=1)
            acc = acc + shifted.reshape(tl, gc)
            s, level = 2 * s, level + 1
        cnt = jnp.minimum(w, pos + 1).astype(F32)
        pooled_ref[:, cols] = (acc / cnt - ug).astype(BF16)
    o_ref[0] = _pool_project(pooled_ref, wg_ref, scale_ref, wbr_ref, sg_ref[0], len(POOL_WINDOWS), gc)


def _pool_prompt(u, sg, layer, wg, scale, wbr):
    b, l, d = u.shape
    _, ng, gc, _ = wg.shape
    tl = _tile(l, 256)
    assert all(w in (1, 2, 4, 8, 16) for w in POOL_WINDOWS) and tl % 8 == 0
    levels = max(POOL_WINDOWS).bit_length() - 1
    tok = pl.BlockSpec((1, tl, d), lambda i, j: (i, j, 0))
    return pl.pallas_call(
        functools.partial(_pool_prompt_kernel, tl=tl, gc=gc),
        grid=(b, l // tl),
        in_specs=[tok, _lspec(layer, (ng, gc, gc)), _lspec(layer, (1, d)), _lspec(layer, (d, d)), tok],
        out_specs=tok,
        out_shape=jax.ShapeDtypeStruct((b, l, d), F32),
        scratch_shapes=[pltpu.VMEM((levels, 8, d), F32), pltpu.VMEM((tl, d), BF16)],
        compiler_params=_params("parallel", "arbitrary"),
        name="pool_prompt",
    )(u, wg, scale, wbr, sg)


def _layer_state_call(kern, *, layer, state, batch_axis, stacked_so_far, grid, tb, other_operands, other_in_specs,
                      other_out_specs, other_out_shapes, scratch_shapes=(), name):
    blk = tuple(tb if a == batch_axis else s for a, s in enumerate(state.shape))[1:]
    spec = _lspec(layer, blk, lambda i: tuple(i if a == batch_axis else 0 for a in range(1, state.ndim)))
    if stacked_so_far is None:
        stacked_so_far = jnp.zeros(state.shape, state.dtype)
    operands = list(other_operands) + [state, stacked_so_far]
    in_specs = list(other_in_specs) + [spec, pl.BlockSpec(memory_space=pl.ANY)]
    return pl.pallas_call(
        kern,
        grid=grid,
        in_specs=in_specs,
        out_specs=list(other_out_specs) + [spec],
        out_shape=list(other_out_shapes) + [jax.ShapeDtypeStruct(state.shape, state.dtype)],
        input_output_aliases={len(operands) - 1: len(other_out_shapes)},
        scratch_shapes=list(scratch_shapes),
        compiler_params=_params("parallel"),
        name=name,
    )(*operands)


def _pool_sample_kernel(u_ref, wg_ref, scale_ref, wbr_ref, sg_ref, buf_ref, stack_hbm_ref,
                        o_ref, nbuf_ref, pooled_ref, *, gc):
    del stack_hbm_ref
    nbuf = buf_ref.shape[0]
    u = u_ref[...]
    for r in range(nbuf - 1):
        nbuf_ref[r] = buf_ref[r + 1]
    nbuf_ref[nbuf - 1] = u
    for g, w in enumerate(POOL_WINDOWS):
        cols = slice(g * gc, (g + 1) * gc)
        ug = u[:, cols]
        s = ug
        for k in range(1, w):
            s = s + buf_ref[nbuf - k, :, cols]
        cnt = float(min(w, PAST_LEN + 1))
        pooled_ref[:, cols] = (s / cnt - ug).astype(BF16)
    o_ref[...] = _pool_project(pooled_ref, wg_ref, scale_ref, wbr_ref, sg_ref[...], len(POOL_WINDOWS), gc)


def _pool_sample(u, sg, pool_state_t, layer, stacked_so_far, wg, scale, wbr):
    bs, d = u.shape
    _, ng, gc, _ = wg.shape
    tb = _tile(bs, 32)
    row = pl.BlockSpec((tb, d), lambda i: (i, 0))
    return _layer_state_call(
        functools.partial(_pool_sample_kernel, gc=gc),
        layer=layer, state=pool_state_t, batch_axis=2, stacked_so_far=stacked_so_far, grid=(bs // tb,), tb=tb,
        other_operands=[u, wg, scale, wbr, sg],
        other_in_specs=[row, _lspec(layer, (ng, gc, gc)), _lspec(layer, (1, d)), _lspec(layer, (d, d)), row],
        other_out_specs=[row], other_out_shapes=[jax.ShapeDtypeStruct((bs, d), F32)],
        scratch_shapes=[pltpu.VMEM((tb, d), BF16)], name="pool_sample")


def _dt_prep_kernel(dtr_ref, bias_ref, alog_ref, dt_ref, cum_ref, cumt_ref):
    c = SSD_CHUNK
    dt_ref[...] = _softplus(dtr_ref[...] + bias_ref[...])
    a = -jnp.exp(alog_ref[...])
    row = lax.broadcasted_iota(jnp.int32, (c, dtr_ref.shape[1]), 0)
    for ch in range(dtr_ref.shape[0] // c):
        rows = slice(ch * c, (ch + 1) * c)
        cum = dt_ref[rows, :] * a
        k = 1
        while k < c:
            cum = cum + jnp.where(row >= k, pltpu.roll(cum, k, axis=0), 0.0)
            k *= 2
        cum_ref[rows, :] = cum
        cumt_ref[rows, :] = cum.T


def _dt_prep(dtr, layer, bias, alog):
    t, hp = dtr.shape
    c = SSD_CHUNK * _tile(t // SSD_CHUNK, 8)
    spec = pl.BlockSpec((c, hp), lambda i: (i, 0))
    return pl.pallas_call(
        _dt_prep_kernel,
        grid=(t // c,),
        in_specs=[spec, _lspec(layer, (1, hp)), _lspec(layer, (1, hp))],
        out_specs=[spec, spec, spec],
        out_shape=[jax.ShapeDtypeStruct((t, hp), F32)] * 3,
        compiler_params=_params("parallel"),
        name="dt_prep",
    )(dtr, bias, alog)


def _conv_silu(x, carry_ref, w_ref, b_ref):
    rows, wd = x.shape
    kw = w_ref.shape[0]
    nt = rows // 8
    last_row = lax.broadcasted_iota(jnp.int32, (nt, 8, wd), 1) == 7
    t = x * w_ref[0:1, :]
    for k in range(1, kw):
        t3 = t.reshape(nt, 8, wd)
        before = jnp.concatenate([carry_ref[k - 1][None], t3[:nt - 1]], axis=0)
        carry_ref[k - 1] = t[rows - 8:rows, :]
        shifted = pltpu.roll(jnp.where(last_row, before, t3), 1, axis=1).reshape(rows, wd)
        t = x * w_ref[k:k + 1, :] + shifted
    return _silu(t + b_ref[...])


def _ssd_prompt_kernel(xs_ref, bm_ref, cm_ref, zs_ref, dt_ref, cum_ref, cumt_ref,
                       wx_ref, wb_ref, wc_ref, bx_ref, bb_ref, bc_ref, dskip_ref, nw_ref, e_ref,
                       y_ref, st_ref, *rest, hpg, p, gps, zero_fill):
    stt_ref, hx_ref, hb_ref, hc_ref = rest[-4:]
    c = pl.program_id(2)
    nc = pl.num_programs(2)
    rows = xs_ref.shape[1]
    n = bm_ref.shape[2] // gps
    gw = hpg * p
    if zero_fill:
        rest[0][...] = jnp.zeros(rest[0].shape, F32)

    @pl.when(c == 0)
    def _():
        for r in (stt_ref, hx_ref, hb_ref, hc_ref):
            r[...] = jnp.zeros(r.shape, F32)

    x_all = _conv_silu(xs_ref[0], hx_ref, wx_ref, bx_ref)
    b_all = _conv_silu(bm_ref[0], hb_ref, wb_ref, bb_ref).astype(BF16)
    c_all = _conv_silu(cm_ref[0], hc_ref, wc_ref, bc_ref).astype(BF16)
    causal = lax.broadcasted_iota(jnp.int32, (rows, rows), 0) >= lax.broadcasted_iota(jnp.int32, (rows, rows), 1)
    lane = lax.broadcasted_iota(jnp.int32, (rows, LANES), 1)
    heads_per_tile = LANES // p
    lane_masks = [jnp.where(lane // p == r, 1.0, 0.0).astype(BF16) for r in range(heads_per_tile)]

    for gi in range(gps):
        g = pl.program_id(1) * gps + gi
        gcols = slice(gi * gw, (gi + 1) * gw)
        x = x_all[:, gcols]
        bmat = b_all[:, gi * n:(gi + 1) * n]
        cmat = c_all[:, gi * n:(gi + 1) * n]
        shift = (LANES - g * hpg) % LANES
        dtg = pltpu.roll(dt_ref[0], shift, axis=1)
        cumg = pltpu.roll(cum_ref[0], shift, axis=1)
        cumtg = cumt_ref[0, pl.ds(pl.multiple_of(g * hpg, hpg), hpg), :]
        last = cumg[rows - 1:rows, :]
        stack = jnp.concatenate([dtg, jnp.exp(cumg), jnp.exp(last - cumg)], axis=0)
        hi, mid = _split2(stack)
        expanded = jnp.dot(jnp.concatenate([hi, mid], axis=1), e_ref[...], preferred_element_type=F32)
        dt_x, ecum_x, ws_x = expanded[0:rows], expanded[rows:2 * rows], expanded[2 * rows:3 * rows]

        xdt = x * dt_x
        xdt_b = xdt.astype(BF16)
        xdtw_b = (xdt * ws_x).astype(BF16)
        cb = lax.dot_general(cmat, bmat, (((1,), (1,)), ((), ())), preferred_element_type=F32)
        st_old = stt_ref[:, gcols]
        y = jnp.dot(cmat, st_old.astype(BF16), preferred_element_type=F32) * ecum_x
        stt_ref[:, gcols] = st_old * ecum_x[rows - 1:rows, :] + lax.dot_general(
            bmat, xdtw_b, (((0,), (0,)), ((), ())), preferred_element_type=F32)

        cum2 = cumg * math.log2(math.e)
        cumt2 = cumtg * math.log2(math.e)
        diag = []
        for q in range(hpg // heads_per_tile):
            xq = xdt_b[:, q * LANES:(q + 1) * LANES]
            ms, xr = [], []
            for r in range(heads_per_tile):
                h = q * heads_per_tile + r
                dec = jnp.exp2(cum2[:, h:h + 1] - cumt2[h:h + 1, :])
                ms.append(jnp.where(causal, cb * dec, 0.0).astype(BF16))
                xr.append(xq * lane_masks[r])
            diag.append(jnp.dot(jnp.concatenate(ms, axis=1), jnp.concatenate(xr, axis=0),
                                preferred_element_type=F32))
        y = y + jnp.concatenate(diag, axis=1) + dskip_ref[:, gcols] * x
        y = y * zs_ref[0, :, gcols].astype(F32)
        y = y * lax.rsqrt(jnp.mean(y * y, axis=-1, keepdims=True) + RMS_EPS) * nw_ref[:, gcols]
        y_ref[0, :, gcols] = y.astype(y_ref.dtype)

    @pl.when(c == nc - 1)
    def _():
        st_ref[0] = stt_ref[...].T.reshape(st_ref.shape[1:])


def _ssd_prompt(xbc, zs, dt, cum, cumt, layer, conv_w, conv_b, dskip_x, norm_w, e_mat, *, di, n, p, groups,
                zero_fill_shape=None):
    b, l, cd = xbc.shape
    c = SSD_CHUNK
    heads = di // p
    hpg = heads // groups
    gps = next(g for g in (8, 4, 2, 1) if groups % g == 0)
    gw = hpg * p * gps
    nn = n * gps
    hp = dt.shape[-1]
    xoff, boff, coff = 0, di // nn, (di + groups * n) // nn

    def seg(width, off_blocks):
        return pl.BlockSpec((1, c, width), lambda i, g, k: (i, k, off_blocks + g))

    def wseg(rows, width, off_blocks):
        return _lspec(layer, (rows, width), lambda i, g, k: (0, off_blocks + g))

    small = pl.BlockSpec((1, c, hp), lambda i, g, k: (i, k, 0))
    kw = conv_w.shape[1]
    ng, nc = groups // gps, l // c
    extra_specs, extra_shapes = [], []
    if zero_fill_shape is not None and zero_fill_shape[0] % (b * ng * nc):
        zero_fill_shape = None
    if zero_fill_shape is not None:
        per_step = zero_fill_shape[0] // (b * ng * nc)
        tail = tuple(zero_fill_shape[1:])
        extra_specs = [pl.BlockSpec((per_step,) + tail,
                                    lambda i, g, k: ((i * ng + g) * nc + k,) + (0,) * len(tail))]
        extra_shapes = [jax.ShapeDtypeStruct(tuple(zero_fill_shape), F32)]
    outs = pl.pallas_call(
        functools.partial(_ssd_prompt_kernel, hpg=hpg, p=p, gps=gps, zero_fill=zero_fill_shape is not None),
        grid=(b, ng, nc),
        in_specs=[
            seg(gw, xoff), seg(nn, boff), seg(nn, coff), seg(gw, 0), small, small, small,
            wseg(kw, gw, xoff), wseg(kw, nn, boff), wseg(kw, nn, coff),
            wseg(1, gw, xoff), wseg(1, nn, boff), wseg(1, nn, coff),
            wseg(1, gw, 0), wseg(1, gw, 0),
            pl.BlockSpec(e_mat.shape, lambda i, g, k: (0, 0)),
        ],
        out_specs=[
            pl.BlockSpec((1, c, gw), lambda i, g, k: (i, k, g)),
            pl.BlockSpec((1, hpg * gps, p, n), lambda i, g, k: (i, g, 0, 0)),
        ] + extra_specs,
        out_shape=[jax.ShapeDtypeStruct((b, l, di), BF16), jax.ShapeDtypeStruct((b, heads, p, n), F32)]
        + extra_shapes,
        scratch_shapes=[
            pltpu.VMEM((n, gw), F32),
            pltpu.VMEM((kw - 1, 8, gw), F32), pltpu.VMEM((kw - 1, 8, nn), F32), pltpu.VMEM((kw - 1, 8, nn), F32),
        ],
        compiler_params=_params("parallel", "parallel", "arbitrary"),
        name="ssd_prompt",
    )(xbc, xbc, xbc, zs, dt, cum, cumt, conv_w, conv_w, conv_w, conv_b, conv_b, conv_b, dskip_x, norm_w, e_mat)
    return (outs[0], outs[1], outs[2] if extra_shapes else None)


def _ssd_sample_pre_kernel(xbc_ref, w_ref, b_ref, dtr_ref, bias_ref, alog_ref, e_ref, cbuf_ref, stack_hbm_ref,
                           xs_ref, xdt_ref, dae_ref, bm_ref, cm_ref, ncbuf_ref, *, di, gn):
    del stack_hbm_ref
    kw = w_ref.shape[0]
    xbc = xbc_ref[...]
    acc = b_ref[...]
    for k in range(kw - 1):
        acc = acc + cbuf_ref[k] * w_ref[k:k + 1, :]
    acc = _silu(acc + xbc * w_ref[kw - 1:kw, :])
    for k in range(kw - 2):
        ncbuf_ref[k] = cbuf_ref[k + 1]
    ncbuf_ref[kw - 2] = xbc
    xs = acc[:, 0:di]
    dt = _softplus(dtr_ref[...] + bias_ref[...])
    dae_ref[...] = jnp.exp(dt * (-jnp.exp(alog_ref[...])))
    hi, mid = _split2(dt)
    dt_x = jnp.dot(jnp.concatenate([hi, mid], axis=1), e_ref[...], preferred_element_type=F32)
    xs_ref[...] = xs
    xdt_ref[...] = xs * dt_x
    bm_ref[...] = acc[:, di:di + gn]
    cm_ref[...] = acc[:, di + gn:di + 2 * gn]


def _ssd_sample_pre(xbc, dtr, conv_state_t, layer, stacked_so_far, conv_w, conv_b, bias, alog, e_mat, *, di, gn):
    bs, cd = xbc.shape
    hp = dtr.shape[1]
    kw = conv_w.shape[1]
    tb = _tile(bs, 32)
    row = lambda w: pl.BlockSpec((tb, w), lambda i: (i, 0))
    return _layer_state_call(
        functools.partial(_ssd_sample_pre_kernel, di=di, gn=gn),
        layer=layer, state=conv_state_t, batch_axis=2, stacked_so_far=stacked_so_far, grid=(bs // tb,), tb=tb,
        other_operands=[xbc, conv_w, conv_b, dtr, bias, alog, e_mat],
        other_in_specs=[row(cd), _lspec(layer, (kw, cd)), _lspec(layer, (1, cd)), row(hp), _lspec(layer, (1, hp)),
                        _lspec(layer, (1, hp)), pl.BlockSpec(e_mat.shape, lambda i: (0, 0))],
        other_out_specs=[row(di), row(di), row(hp), row(gn), row(gn)],
        other_out_shapes=[jax.ShapeDtypeStruct((bs, di), F32)] * 2 + [jax.ShapeDtypeStruct((bs, hp), F32)]
        + [jax.ShapeDtypeStruct((bs, gn), F32)] * 2,
        name="ssd_sample_pre")


def _ssd_sample_state_kernel(dae_ref, xdt_ref, bm_ref, cm_ref, s_ref, stack_hbm_ref, y_ref, so_ref, *, groups):
    del stack_hbm_ref
    tb, heads, p, n = s_ref.shape
    di = heads * p
    gw = di // groups
    grow = lax.broadcasted_iota(jnp.int32, (groups, di), 0)
    gcol = lax.broadcasted_iota(jnp.int32, (groups, di), 1) // gw
    own = grow == gcol
    fill = (-groups) % 16
    for b in range(tb):
        seq = pl.program_id(0) * tb + b
        xg = jnp.where(own, jnp.broadcast_to(xdt_ref[b], (groups, di)), 0.0)
        bmat = bm_ref[b]
        if fill:
            xg = jnp.concatenate([xg, jnp.zeros((fill, di), F32)], axis=0)
            bmat = jnp.concatenate([bmat, jnp.zeros((fill, n), F32)], axis=0)
        upd = lax.dot_general(xg.astype(BF16), bmat.astype(BF16), (((0,), (0,)), ((), ())),
                              preferred_element_type=F32)
        for h in range(heads):
            so_ref[b, h] = s_ref[b, h] * dae_ref[seq, h] + upd[h * p:(h + 1) * p, :]
        s_new = so_ref[b].reshape(di, n).astype(BF16)
        yg = lax.dot_general(cm_ref[b].astype(BF16), s_new, (((1,), (1,)), ((), ())),
                             preferred_element_type=F32)
        y_ref[b] = jnp.sum(jnp.where(own, yg, 0.0), axis=0, keepdims=True)


def _ssd_sample_state(ssd_state, layer, stacked_so_far, xdt, dae, bm, cm, *, groups):
    _, bs, heads, p, n = ssd_state.shape
    di = heads * p
    tb = _tile(bs, 4)
    xdt3 = xdt.reshape(bs, 1, di)
    bm3, cm3 = bm.reshape(bs, groups, n), cm.reshape(bs, groups, n)
    rspec = pl.BlockSpec((tb, 1, di), lambda i: (i, 0, 0))
    gspec = pl.BlockSpec((tb, groups, n), lambda i: (i, 0, 0))
    y, stacked = _layer_state_call(
        functools.partial(_ssd_sample_state_kernel, groups=groups),
        layer=layer, state=ssd_state, batch_axis=1, stacked_so_far=stacked_so_far, grid=(bs // tb,), tb=tb,
        other_operands=[dae, xdt3, bm3, cm3],
        other_in_specs=[pl.BlockSpec(memory_space=pltpu.SMEM), rspec, gspec, gspec],
        other_out_specs=[rspec], other_out_shapes=[jax.ShapeDtypeStruct((bs, 1, di), F32)],
        name="ssd_sample_state")
    return stacked, y.reshape(bs, di)


def _ssd_sample_post_kernel(y_ref, xs_ref, zs_ref, dskip_ref, nw_ref, o_ref, *, groups):
    y = (y_ref[...] + dskip_ref[...] * xs_ref[...]) * zs_ref[...].astype(F32)
    gw = y.shape[1] // groups
    for g in range(groups):
        cols = slice(g * gw, (g + 1) * gw)
        yg = y[:, cols]
        yg = yg * lax.rsqrt(jnp.mean(yg * yg, axis=-1, keepdims=True) + RMS_EPS) * nw_ref[:, cols]
        o_ref[:, cols] = yg.astype(o_ref.dtype)


def _ssd_sample_post(y, xs, zs, layer, dskip_x, norm_w, *, groups):
    bs, di = y.shape
    full = pl.BlockSpec((bs, di), lambda i: (0, 0))
    return pl.pallas_call(
        functools.partial(_ssd_sample_post_kernel, groups=groups),
        grid=(1,),
        in_specs=[full, full, full, _lspec(layer, (1, di)), _lspec(layer, (1, di))],
        out_specs=full,
        out_shape=jax.ShapeDtypeStruct(y.shape, BF16),
        compiler_params=_params("arbitrary"),
        name="ssd_sample_post",
    )(y, xs, zs, dskip_x, norm_w)


def _merge_proj_ln_kernel(y_ref, sg_ref, a_ref, x_ref, wb_ref, wo_ref, g_ref, b_ref, o_ref, ob_ref, *, alpha):
    tm = y_ref.shape[0]
    halves = 2 if tm % 16 == 0 else 1
    for r in range(halves):
        rows = slice(r * tm // halves, (r + 1) * tm // halves)
        bb = jnp.dot(y_ref[rows, :], wb_ref[...], preferred_element_type=F32)
        merged = (a_ref[rows, :] + sg_ref[rows, :].astype(F32) * bb).astype(BF16)
        f = jnp.dot(merged, wo_ref[...], preferred_element_type=F32)
        h = _layer_norm(alpha * x_ref[rows, :] + f, g_ref[...], b_ref[...])
        o_ref[rows, :] = h
        ob_ref[rows, :] = h.astype(BF16)


def _merge_proj_ln(y, sg, a, x, layer, wb, wo, g, b, *, alpha):
    m, k = y.shape
    d = wo.shape[2]
    tm = _tile(m, 256)
    row = lambda width: pl.BlockSpec((tm, width), lambda i: (i, 0))
    once = pl.Buffered(1)
    return pl.pallas_call(
        functools.partial(_merge_proj_ln_kernel, alpha=alpha),
        grid=(m // tm,),
        in_specs=[row(k), pl.BlockSpec((tm, d), lambda i: (i, 1)),
                  row(d), row(d), _lspec(layer, (k, d), pipeline_mode=once),
                  _lspec(layer, (d, d), pipeline_mode=once), _lspec(layer, (1, d)), _lspec(layer, (1, d))],
        out_specs=[row(d), row(d)],
        out_shape=[jax.ShapeDtypeStruct((m, d), F32), jax.ShapeDtypeStruct((m, d), BF16)],
        compiler_params=_params("parallel"),
        name="merge_proj_ln",
    )(y, sg, a, x, wb, wo, g, b)


def _attn_prompt_kernel(hb_ref, h_ref, k_ref, v_ref, wq_ref, wo_ref, g_ref, b_ref, o_ref, ob_ref, *,
                        heads, alpha):
    q = jnp.dot(hb_ref[0], wq_ref[...], preferred_element_type=F32)
    hd = q.shape[1] // heads
    scale = 1.0 / math.sqrt(hd)
    kb = k_ref[0].astype(BF16)
    vb = v_ref[0].astype(BF16)
    outs = []
    for h in range(heads):
        cols = slice(h * hd, (h + 1) * hd)
        s = lax.dot_general(q[:, cols].astype(BF16), kb[:, cols], (((1,), (1,)), ((), ())),
                            preferred_element_type=F32) * scale
        e = jnp.exp(s - jnp.max(s, axis=-1, keepdims=True))
        pr = e / jnp.sum(e, axis=-1, keepdims=True)
        outs.append(jnp.dot(pr.astype(BF16), vb[:, cols], preferred_element_type=F32))
    o = jnp.concatenate(outs, axis=1).astype(BF16)
    ca = jnp.dot(o, wo_ref[...], preferred_element_type=F32)
    hn = _layer_norm(alpha * h_ref[0] + ca, g_ref[...], b_ref[...])
    o_ref[0] = hn
    ob_ref[0] = hn.astype(BF16)


def _attn_prompt(hb, h, mk, mv, layer, wq, wo, g, b, *, heads, alpha):
    bsz, l, d = h.shape
    nm, md = mk.shape[1], mk.shape[2]
    tq = _tile(l, 512)
    row = pl.BlockSpec((1, tq, d), lambda i, j: (i, j, 0))
    mem = pl.BlockSpec((1, nm, md), lambda i, j: (i, 0, 0))
    return pl.pallas_call(
        functools.partial(_attn_prompt_kernel, heads=heads, alpha=alpha),
        grid=(bsz, l // tq),
        in_specs=[row, row, mem, mem, _lspec(layer, (d, md)), _lspec(layer, (md, d)), _lspec(layer, (1, d)),
                  _lspec(layer, (1, d))],
        out_specs=[row, row],
        out_shape=[jax.ShapeDtypeStruct((bsz, l, d), F32), jax.ShapeDtypeStruct((bsz, l, d), BF16)],
        compiler_params=_params("parallel", "parallel"),
        name="attn_prompt",
    )(hb, h, mk, mv, wq, wo, g, b)


def _attn_sample_kernel(hb_ref, h_ref, k_ref, v_ref, wq_ref, wo_ref, g_ref, b_ref, o_ref, ob_ref, att_ref, *,
                        heads, alpha):
    tb = hb_ref.shape[0]
    rows, hd = k_ref.shape[1], k_ref.shape[2]
    reps = 8 // heads
    scale = 1.0 / math.sqrt(hd)
    hb = hb_ref[...]
    qs = [jnp.dot(hb, wq_ref[:, h * hd:(h + 1) * hd], preferred_element_type=F32) for h in range(heads)]

    def over_copies(x, op):
        out = x
        for r in range(1, reps):
            out = op(out, pltpu.roll(x, r * heads, axis=0))
        return out

    for bi in range(tb):
        q8 = jnp.concatenate([q[bi:bi + 1, :] for q in qs] * reps, axis=0)
        k3 = k_ref[bi].reshape(rows // 8, 8, hd)
        v3 = v_ref[bi].reshape(rows // 8, 8, hd)
        s = jnp.sum(k3 * q8[None], axis=-1, keepdims=True) * scale
        m = over_copies(jnp.max(s, axis=0), jnp.maximum)
        e = jnp.exp(s - m[None])
        den = over_copies(jnp.sum(e, axis=0), jnp.add)
        o8 = over_copies(jnp.sum((e / den[None]) * v3, axis=0), jnp.add)
        att_ref[bi] = o8[0:heads, :]
    ca = None
    for h in range(heads):
        t = jnp.dot(att_ref[:, h, :].astype(BF16), wo_ref[h * hd:(h + 1) * hd, :], preferred_element_type=F32)
        ca = t if ca is None else ca + t
    hn = _layer_norm(alpha * h_ref[...] + ca, g_ref[...], b_ref[...])
    o_ref[...] = hn
    ob_ref[...] = hn.astype(BF16)


def _attn_sample(hb, h, cache_k, cache_v, layer, wq, wo, g, b, *, alpha):
    bs, d = h.shape
    depth, _, nm, heads, hd = cache_k.shape
    assert 8 % heads == 0
    md = heads * hd
    ck = cache_k.reshape(depth, bs, nm * heads, hd)
    cv = cache_v.reshape(depth, bs, nm * heads, hd)
    tb = _tile(bs, 8)
    row = pl.BlockSpec((tb, d), lambda i: (i, 0))
    mem = _lspec(layer, (tb, nm * heads, hd), lambda i: (i, 0, 0))
    return pl.pallas_call(
        functools.partial(_attn_sample_kernel, heads=heads, alpha=alpha),
        grid=(bs // tb,),
        in_specs=[row, row, mem, mem, _lspec(layer, (d, md)), _lspec(layer, (md, d)), _lspec(layer, (1, d)),
                  _lspec(layer, (1, d))],
        out_specs=[row, row],
        out_shape=[jax.ShapeDtypeStruct((bs, d), F32), jax.ShapeDtypeStruct((bs, d), BF16)],
        scratch_shapes=[pltpu.VMEM((tb, heads, hd), F32)],
        compiler_params=_params("parallel"),
        name="attn_sample",
    )(hb, h, ck, cv, wq, wo, g, b)


def _ffn_kernel(hb_ref, h_ref, wg_ref, wu_ref, wd_ref, g_ref, b_ref, o_ref, ob_ref, *, alpha):
    f = pl.program_id(1)
    tm, d = o_ref.shape

    @pl.when(f == 0)
    def _():
        o_ref[...] = jnp.zeros(o_ref.shape, F32)

    x = hb_ref[...]
    gate = jnp.dot(x, wg_ref[...], preferred_element_type=F32)
    up = jnp.dot(x, wu_ref[...], preferred_element_type=F32)
    act = (_silu(gate) * up).astype(BF16)
    for c in range(2):
        cols = slice(c * d // 2, (c + 1) * d // 2)
        o_ref[:, cols] += jnp.dot(act, wd_ref[:, cols], preferred_element_type=F32)

    @pl.when(f == pl.num_programs(1) - 1)
    def _():
        for r in range(4):
            rows = slice(r * tm // 4, (r + 1) * tm // 4)
            hn = _layer_norm(alpha * h_ref[rows, :] + o_ref[rows, :], g_ref[...], b_ref[...])
            o_ref[rows, :] = hn
            ob_ref[rows, :] = hn.astype(BF16)


def _ffn(hb, h, layer, wg, wu, wd, g, b, *, alpha):
    m, d = h.shape
    ff = wg.shape[2]
    tm, tf = _tile(m, 1024), 512
    assert ff % tf == 0
    row = pl.BlockSpec((tm, d), lambda i, f: (i, 0))
    row_once = pl.BlockSpec((tm, d), lambda i, f: (i, 0), pipeline_mode=pl.Buffered(1))
    return pl.pallas_call(
        functools.partial(_ffn_kernel, alpha=alpha),
        grid=(m // tm, ff // tf),
        in_specs=[row, row_once, _lspec(layer, (d, tf), lambda i, f: (0, f)),
                  _lspec(layer, (d, tf), lambda i, f: (0, f)), _lspec(layer, (tf, d), lambda i, f: (f, 0)),
                  _lspec(layer, (1, d)), _lspec(layer, (1, d))],
        out_specs=[row_once, row],
        out_shape=[jax.ShapeDtypeStruct((m, d), F32), jax.ShapeDtypeStruct((m, d), BF16)],
        compiler_params=_params("parallel", "arbitrary"),
        name="ffn",
    )(hb, h, wg, wu, wd, g, b)


def _expansion_matrix(n_in, reps, n_pad):
    r = jnp.arange(n_pad)[:, None]
    c = jnp.arange(n_in * reps)[None, :] // reps
    e = ((r == c) & (r < n_in)).astype(BF16)
    return jnp.concatenate([e, e], axis=0)


def kernel(x_prompt, x_sample, mem_prompt, state_ssd, state_conv, state_pool, cache_mem_k, cache_mem_v, w_in, w_pool, pool_scale, w_pool_br, conv_w, conv_b, dt_bias, a_log, d_skip, ssd_norm_w, w_ssd_br, w_out, ln_mix_g, ln_mix_b, w_mq, w_mk, w_mv, w_mo, ln_mem_g, ln_mem_b, w_ffn_gate, w_ffn_up, w_ffn_down, ln_ffn_g, ln_ffn_b):
    bp, seq, d = x_prompt.shape
    bs = x_sample.shape[0]
    depth = w_in.shape[0]
    _, _, heads, p, n = state_ssd.shape
    di = heads * p
    cd = conv_w.shape[2]
    groups = (cd - di) // (2 * n)
    gn = groups * n
    hpg = heads // groups
    n_mem, mem_heads, mem_hd = cache_mem_k.shape[2:]
    md = mem_heads * mem_hd
    nbuf = state_pool.shape[2]
    kw1 = state_conv.shape[2]
    alpha = (2 * depth) ** 0.25
    assert seq % SSD_CHUNK == 0 and heads <= LANES and LANES % p == 0 and hpg % (LANES // p) == 0

    s3 = d + di + cd
    s4 = s3 + heads

    w_in_t = jnp.transpose(w_in, (0, 2, 1))
    w_dt = jnp.pad(w_in_t[:, s3:s4, :], ((0, 0), (0, LANES - heads), (0, 0)))
    bf = lambda w: w.astype(BF16)
    wpool_b, wpbr_b, wsbr_b, wout_b = bf(w_pool), bf(w_pool_br), bf(w_ssd_br), bf(w_out)
    wq_b, wk_b, wv_b, wo_b = bf(w_mq), bf(w_mk), bf(w_mv), bf(w_mo)
    wg_b, wu_b, wd_b = bf(w_ffn_gate), bf(w_ffn_up), bf(w_ffn_down)
    vec = lambda v: v.astype(F32).reshape(depth, 1, v.shape[-1])
    pad_h = lambda v: jnp.pad(v.astype(F32), ((0, 0), (0, LANES - heads))).reshape(depth, 1, LANES)
    scale, cbias, nw = vec(pool_scale), vec(conv_b), vec(ssd_norm_w)
    bias_p, alog_p = pad_h(dt_bias), pad_h(a_log)
    dskip_x = vec(jnp.repeat(d_skip, p, axis=1))
    mix_g, mix_b, mem_g, mem_b, ffn_g, ffn_b = (vec(v) for v in (ln_mix_g, ln_mix_b, ln_mem_g, ln_mem_b,
                                                                   ln_ffn_g, ln_ffn_b))
    e_group = _expansion_matrix(hpg, p, LANES)
    e_all = _expansion_matrix(heads, p, LANES)

    pool_state_t = jnp.transpose(state_pool, (0, 2, 1, 3))
    conv_state_t = jnp.transpose(state_conv, (0, 2, 1, 3))

    hp, hs = x_prompt.reshape(bp * seq, d), x_sample.reshape(bs, d)
    hp_b, hs_b = hp.astype(BF16), hs.astype(BF16)
    memory_bf = mem_prompt.reshape(bp * n_mem, d).astype(BF16)

    outs = {k: [] for k in ("ssd_p", "conv_p", "pool_p", "mk", "mv")}
    new_ssd_s = new_conv_s = new_pool_s = None
    for l in range(depth):

        def in_proj(hb):
            mm = functools.partial(_mm, hb, layer=l, w_is_nk=True)
            u = mm(w_in_t, col0=0, n=d, name="in_u")
            zs = mm(w_in_t, col0=d, n=di, act="silu", out_dtype=BF16, name="in_z")
            xbc = mm(w_in_t, col0=d + di, n=cd, name="in_xbc")
            dtr = mm(w_dt, name="in_dt")
            gates = mm(w_in_t, col0=s4, n=2 * d, act="sigmoid", out_dtype=BF16, name="in_gates")
            return u, zs, xbc, dtr, gates

        def tail(h, y, gates, a_gated, attn):
            h1, h1b = _merge_proj_ln(y, gates, a_gated, h, l, wsbr_b, wout_b, mix_g, mix_b, alpha=alpha)
            h2, h2b = attn(h1, h1b)
            return _ffn(h2b, h2, l, wg_b, wu_b, wd_b, ffn_g, ffn_b, alpha=alpha)

        u, zs, xbc, dtr, gates = in_proj(hp_b)
        u3, xbc3 = u.reshape(bp, seq, d), xbc.reshape(bp, seq, cd)
        a_gated = _pool_prompt(u3, gates.reshape(bp, seq, 2 * d), l, wpool_b, scale, wpbr_b).reshape(bp * seq, d)
        dt, cum, cumt = _dt_prep(dtr, l, bias_p, alog_p)
        r3 = lambda a: a.reshape(bp, seq, a.shape[-1])
        fill = (depth * bs,) + state_ssd.shape[2:] if l == 0 else None
        y, st, zeros = _ssd_prompt(xbc3, r3(zs), r3(dt), r3(cum), r3(cumt), l, conv_w, cbias, dskip_x, nw, e_group,
                                   di=di, n=n, p=p, groups=groups, zero_fill_shape=fill)
        if zeros is not None:
            new_ssd_s = zeros.reshape(state_ssd.shape)
        mk_p = _mm(memory_bf, wk_b, l, name="mem_k").reshape(bp, n_mem, md)
        mv_p = _mm(memory_bf, wv_b, l, name="mem_v").reshape(bp, n_mem, md)

        def attn_p(h1, h1b):
            o, ob = _attn_prompt(h1b.reshape(bp, seq, d), h1.reshape(bp, seq, d), mk_p, mv_p, l, wq_b, wo_b,
                                 mem_g, mem_b, heads=mem_heads, alpha=alpha)
            return o.reshape(bp * seq, d), ob.reshape(bp * seq, d)

        hp, hp_b = tail(hp, y.reshape(bp * seq, di), gates, a_gated, attn_p)
        outs["ssd_p"].append(st)
        outs["conv_p"].append(xbc3[:, seq - kw1:, :])
        outs["pool_p"].append(u3[:, seq - nbuf:, :])
        outs["mk"].append(mk_p.reshape(bp, n_mem, mem_heads, mem_hd))
        outs["mv"].append(mv_p.reshape(bp, n_mem, mem_heads, mem_hd))

        u, zs, xbc, dtr, gates = in_proj(hs_b)
        a_gated, new_pool_s = _pool_sample(u, gates, pool_state_t, l, new_pool_s, wpool_b, scale, wpbr_b)
        xs, xdt, dae, bm, cm, new_conv_s = _ssd_sample_pre(xbc, dtr, conv_state_t, l, new_conv_s, conv_w, cbias,
                                                           bias_p, alog_p, e_all, di=di, gn=gn)
        new_ssd_s, ys = _ssd_sample_state(state_ssd, l, new_ssd_s, xdt, dae, bm, cm, groups=groups)
        y = _ssd_sample_post(ys, xs, zs, l, dskip_x, nw, groups=groups)

        def attn_s(h1, h1b):
            return _attn_sample(h1b, h1, cache_mem_k, cache_mem_v, l, wq_b, wo_b, mem_g, mem_b, alpha=alpha)

        hs, hs_b = tail(hs, y, gates, a_gated, attn_s)

    stack = lambda k: jnp.stack(outs[k])
    return (hp.reshape(bp, seq, d), hs.reshape(bs, 1, d), stack("ssd_p"), stack("conv_p"), stack("pool_p"),
            stack("mk"), stack("mv"), new_ssd_s, jnp.transpose(new_conv_s, (0, 2, 1, 3)),
            jnp.transpose(new_pool_s, (0, 2, 1, 3)))
```

```python
import functools
import math

import jax
import jax.numpy as jnp
from jax import lax
from jax.experimental import pallas as pl
from jax.experimental.pallas import tpu as pltpu

F32 = jnp.float32
BF16 = jnp.bfloat16

POOL_WINDOWS = (2, 4, 8, 16)
PAST_LEN = 16384
SSD_CHUNK = 128
LN_EPS = 1e-5
RMS_EPS = 1e-5
LANES = 128
VMEM_LIMIT_BYTES = 56 * 1024 * 1024


def _params(*sem):
    return pltpu.CompilerParams(dimension_semantics=sem, vmem_limit_bytes=VMEM_LIMIT_BYTES)


def _silu(x):
    return x * jax.nn.sigmoid(x)


def _softplus(x):
    return jnp.maximum(x, 0.0) + jnp.log(1.0 + jnp.exp(-jnp.abs(x)))


def _layer_norm(x, g, b):
    mu = jnp.mean(x, axis=-1, keepdims=True)
    xc = x - mu
    var = jnp.mean(xc * xc, axis=-1, keepdims=True)
    return xc * lax.rsqrt(var + LN_EPS) * g + b


def _split2(x):
    hi = x.astype(BF16)
    mid = (x - hi.astype(F32)).astype(BF16)
    return hi, mid


def _tile(n, pref):
    t = min(n, pref)
    while n % t:
        t //= 2
    return t


def _lspec(layer, block, index_map=None, **kwargs):
    nd = len(block)
    if index_map is None:
        index_map = lambda *g: (0,) * nd
    return pl.BlockSpec((None,) + tuple(block), lambda *g: (layer,) + tuple(index_map(*g)), **kwargs)


def _mm_kernel(x_ref, w_ref, o_ref, *, act, w_is_nk):
    if w_is_nk:
        acc = lax.dot_general(x_ref[...], w_ref[0].astype(BF16), (((1,), (1,)), ((), ())),
                              preferred_element_type=F32)
    else:
        acc = jnp.dot(x_ref[...], w_ref[...], preferred_element_type=F32)
    if act == "silu":
        acc = _silu(acc)
    elif act == "sigmoid":
        acc = jax.nn.sigmoid(acc)
    o_ref[...] = acc.astype(o_ref.dtype)


def _mm(x, w, layer, *, col0=0, n=None, act=None, out_dtype=F32, w_is_nk=False, name):
    m, k = x.shape
    n = w.shape[1 if w_is_nk else 2] if n is None else n
    if w_is_nk:
        tm = _tile(m, 2048)
        tn = _tile(n, 512 if tm > 512 else 1024)
        assert col0 % 16 == 0
        w_spec = pl.BlockSpec((pl.Element(1), pl.Element(tn), pl.Element(k)),
                              lambda i, j: (layer, pl.multiple_of(col0 + j * tn, 16), 0))
    else:
        tm, tn = _tile(m, 1024), _tile(n, 1024)
        assert col0 % tn == 0
        w_spec = _lspec(layer, (k, tn), lambda i, j: (0, col0 // tn + j))
    return pl.pallas_call(
        functools.partial(_mm_kernel, act=act, w_is_nk=w_is_nk),
        grid=(m // tm, n // tn),
        in_specs=[pl.BlockSpec((tm, k), lambda i, j: (i, 0)), w_spec],
        out_specs=pl.BlockSpec((tm, tn), lambda i, j: (i, j)),
        out_shape=jax.ShapeDtypeStruct((m, n), out_dtype),
        compiler_params=_params("parallel", "arbitrary"),
        name=name,
    )(x, w)


def _pool_project(pooled_ref, wg_ref, scale_ref, wbr_ref, sg, n_groups, gc):
    for g in range(n_groups):
        cols = slice(g * gc, (g + 1) * gc)
        pg = jnp.dot(pooled_ref[:, cols], wg_ref[g], preferred_element_type=F32) * scale_ref[:, cols]
        pooled_ref[:, cols] = pg.astype(BF16)
    a = jnp.dot(pooled_ref[...], wbr_ref[...], preferred_element_type=F32)
    return a * sg.astype(F32)


def _pool_prompt_kernel(u_ref, wg_ref, scale_ref, wbr_ref, sg_ref, o_ref, carry_ref, pooled_ref, *, tl, gc):
    j = pl.program_id(1)

    @pl.when(j == 0)
    def _():
        carry_ref[...] = jnp.zeros(carry_ref.shape, F32)

    nt = tl // 8
    sub = lax.broadcasted_iota(jnp.int32, (nt, 8, gc), 1)
    pos = j * tl + lax.broadcasted_iota(jnp.int32, (tl, 1), 0)
    for g, w in enumerate(POOL_WINDOWS):
        cols = slice(g * gc, (g + 1) * gc)
        ug = u_ref[0, :, cols]
        acc, s, level = ug, 1, 0
        while s < w:
            a3 = acc.reshape(nt, 8, gc)
            before = jnp.concatenate([carry_ref[level, :, cols][None], a3[:nt - 1]], axis=0)
            carry_ref[level, :, cols] = acc[tl - 8:tl, :]
            shifted = before if s == 8 else pltpu.roll(jnp.where(sub >= 8 - s, before, a3), s, axis=1)
            acc = acc + shifted.reshape(tl, gc)
            s, level = 2 * s, level + 1
        cnt = jnp.minimum(w, pos + 1).astype(F32)
        pooled_ref[:, cols] = (acc / cnt - ug).astype(BF16)
    o_ref[0] = _pool_project(pooled_ref, wg_ref, scale_ref, wbr_ref, sg_ref[0], len(POOL_WINDOWS), gc)


def _pool_prompt(u, sg, layer, wg, scale, wbr):
    b, l, d = u.shape
    _, ng, gc, _ = wg.shape
    tl = _tile(l, 256)
    assert all(w in (1, 2, 4, 8, 16) for w in POOL_WINDOWS) and tl % 8 == 0
    levels = max(POOL_WINDOWS).bit_length() - 1
    tok = pl.BlockSpec((1, tl, d), lambda i, j: (i, j, 0))
    return pl.pallas_call(
        functools.partial(_pool_prompt_kernel, tl=tl, gc=gc),
        grid=(b, l // tl),
        in_specs=[tok, _lspec(layer, (ng, gc, gc)), _lspec(layer, (1, d)), _lspec(layer, (d, d)), tok],
        out_specs=tok,
        out_shape=jax.ShapeDtypeStruct((b, l, d), F32),
        scratch_shapes=[pltpu.VMEM((levels, 8, d), F32), pltpu.VMEM((tl, d), BF16)],
        compiler_params=_params("parallel", "arbitrary"),
        name="pool_prompt",
    )(u, wg, scale, wbr, sg)


def _layer_state_call(kern, *, layer, state, batch_axis, stacked_so_far, grid, tb, other_operands, other_in_specs,
                      other_out_specs, other_out_shapes, scratch_shapes=(), name):
    blk = tuple(tb if a == batch_axis else s for a, s in enumerate(state.shape))[1:]
    spec = _lspec(layer, blk, lambda i: tuple(i if a == batch_axis else 0 for a in range(1, state.ndim)))
    if stacked_so_far is None:
        stacked_so_far = jnp.zeros(state.shape, state.dtype)
    operands = list(other_operands) + [state, stacked_so_far]
    in_specs = list(other_in_specs) + [spec, pl.BlockSpec(memory_space=pl.ANY)]
    return pl.pallas_call(
        kern,
        grid=grid,
        in_specs=in_specs,
        out_specs=list(other_out_specs) + [spec],
        out_shape=list(other_out_shapes) + [jax.ShapeDtypeStruct(state.shape, state.dtype)],
        input_output_aliases={len(operands) - 1: len(other_out_shapes)},
        scratch_shapes=list(scratch_shapes),
        compiler_params=_params("parallel"),
        name=name,
    )(*operands)


def _pool_sample_kernel(u_ref, wg_ref, scale_ref, wbr_ref, sg_ref, buf_ref, stack_hbm_ref,
                        o_ref, nbuf_ref, pooled_ref, *, gc):
    del stack_hbm_ref
    nbuf = buf_ref.shape[0]
    u = u_ref[...]
    for r in range(nbuf - 1):
        nbuf_ref[r] = buf_ref[r + 1]
    nbuf_ref[nbuf - 1] = u
    for g, w in enumerate(POOL_WINDOWS):
        cols = slice(g * gc, (g + 1) * gc)
        ug = u[:, cols]
        s = ug
        for k in range(1, w):
            s = s + buf_ref[nbuf - k, :, cols]
        cnt = float(min(w, PAST_LEN + 1))
        pooled_ref[:, cols] = (s / cnt - ug).astype(BF16)
    o_ref[...] = _pool_project(pooled_ref, wg_ref, scale_ref, wbr_ref, sg_ref[...], len(POOL_WINDOWS), gc)


def _pool_sample(u, sg, pool_state_t, layer, stacked_so_far, wg, scale, wbr):
    bs, d = u.shape
    _, ng, gc, _ = wg.shape
    tb = _tile(bs, 32)
    row = pl.BlockSpec((tb, d), lambda i: (i, 0))
    return _layer_state_call(
        functools.partial(_pool_sample_kernel, gc=gc),
        layer=layer, state=pool_state_t, batch_axis=2, stacked_so_far=stacked_so_far, grid=(bs // tb,), tb=tb,
        other_operands=[u, wg, scale, wbr, sg],
        other_in_specs=[row, _lspec(layer, (ng, gc, gc)), _lspec(layer, (1, d)), _lspec(layer, (d, d)), row],
        other_out_specs=[row], other_out_shapes=[jax.ShapeDtypeStruct((bs, d), F32)],
        scratch_shapes=[pltpu.VMEM((tb, d), BF16)], name="pool_sample")


def _dt_prep_kernel(dtr_ref, bias_ref, alog_ref, dt_ref, cum_ref, cumt_ref):
    c = SSD_CHUNK
    dt_ref[...] = _softplus(dtr_ref[...] + bias_ref[...])
    a = -jnp.exp(alog_ref[...])
    row = lax.broadcasted_iota(jnp.int32, (c, dtr_ref.shape[1]), 0)
    for ch in range(dtr_ref.shape[0] // c):
        rows = slice(ch * c, (ch + 1) * c)
        cum = dt_ref[rows, :] * a
        k = 1
        while k < c:
            cum = cum + jnp.where(row >= k, pltpu.roll(cum, k, axis=0), 0.0)
            k *= 2
        cum_ref[rows, :] = cum
        cumt_ref[rows, :] = cum.T


def _dt_prep(dtr, layer, bias, alog):
    t, hp = dtr.shape
    c = SSD_CHUNK * _tile(t // SSD_CHUNK, 8)
    spec = pl.BlockSpec((c, hp), lambda i: (i, 0))
    return pl.pallas_call(
        _dt_prep_kernel,
        grid=(t // c,),
        in_specs=[spec, _lspec(layer, (1, hp)), _lspec(layer, (1, hp))],
        out_specs=[spec, spec, spec],
        out_shape=[jax.ShapeDtypeStruct((t, hp), F32)] * 3,
        compiler_params=_params("parallel"),
        name="dt_prep",
    )(dtr, bias, alog)


def _conv_silu(x, carry_ref, w_ref, b_ref):
    rows, wd = x.shape
    kw = w_ref.shape[0]
    nt = rows // 8
    last_row = lax.broadcasted_iota(jnp.int32, (nt, 8, wd), 1) == 7
    t = x * w_ref[0:1, :]
    for k in range(1, kw):
        t3 = t.reshape(nt, 8, wd)
        before = jnp.concatenate([carry_ref[k - 1][None], t3[:nt - 1]], axis=0)
        carry_ref[k - 1] = t[rows - 8:rows, :]
        shifted = pltpu.roll(jnp.where(last_row, before, t3), 1, axis=1).reshape(rows, wd)
        t = x * w_ref[k:k + 1, :] + shifted
    return _silu(t + b_ref[...])


def _ssd_prompt_kernel(xs_ref, bm_ref, cm_ref, zs_ref, dt_ref, cum_ref, cumt_ref,
                       wx_ref, wb_ref, wc_ref, bx_ref, bb_ref, bc_ref, dskip_ref, nw_ref, e_ref,
                       y_ref, st_ref, *rest, hpg, p, gps, zero_fill):
    stt_ref, hx_ref, hb_ref, hc_ref = rest[-4:]
    c = pl.program_id(2)
    nc = pl.num_programs(2)
    rows = xs_ref.shape[1]
    n = bm_ref.shape[2] // gps
    gw = hpg * p
    if zero_fill:
        rest[0][...] = jnp.zeros(rest[0].shape, F32)

    @pl.when(c == 0)
    def _():
        for r in (stt_ref, hx_ref, hb_ref, hc_ref):
            r[...] = jnp.zeros(r.shape, F32)

    x_all = _conv_silu(xs_ref[0], hx_ref, wx_ref, bx_ref)
    b_all = _conv_silu(bm_ref[0], hb_ref, wb_ref, bb_ref).astype(BF16)
    c_all = _conv_silu(cm_ref[0], hc_ref, wc_ref, bc_ref).astype(BF16)
    causal = lax.broadcasted_iota(jnp.int32, (rows, rows), 0) >= lax.broadcasted_iota(jnp.int32, (rows, rows), 1)
    lane = lax.broadcasted_iota(jnp.int32, (rows, LANES), 1)
    heads_per_tile = LANES // p
    lane_masks = [jnp.where(lane // p == r, 1.0, 0.0).astype(BF16) for r in range(heads_per_tile)]

    for gi in range(gps):
        g = pl.program_id(1) * gps + gi
        gcols = slice(gi * gw, (gi + 1) * gw)
        x = x_all[:, gcols]
        bmat = b_all[:, gi * n:(gi + 1) * n]
        cmat = c_all[:, gi * n:(gi + 1) * n]
        shift = (LANES - g * hpg) % LANES
        dtg = pltpu.roll(dt_ref[0], shift, axis=1)
        cumg = pltpu.roll(cum_ref[0], shift, axis=1)
        cumtg = cumt_ref[0, pl.ds(pl.multiple_of(g * hpg, hpg), hpg), :]
        last = cumg[rows - 1:rows, :]
        stack = jnp.concatenate([dtg, jnp.exp(cumg), jnp.exp(last - cumg)], axis=0)
        hi, mid = _split2(stack)
        expanded = jnp.dot(jnp.concatenate([hi, mid], axis=1), e_ref[...], preferred_element_type=F32)
        dt_x, ecum_x, ws_x = expanded[0:rows], expanded[rows:2 * rows], expanded[2 * rows:3 * rows]

        xdt = x * dt_x
        xdt_b = xdt.astype(BF16)
        xdtw_b = (xdt * ws_x).astype(BF16)
        cb = lax.dot_general(cmat, bmat, (((1,), (1,)), ((), ())), preferred_element_type=F32)
        st_old = stt_ref[:, gcols]
        y = jnp.dot(cmat, st_old.astype(BF16), preferred_element_type=F32) * ecum_x
        stt_ref[:, gcols] = st_old * ecum_x[rows - 1:rows, :] + lax.dot_general(
            bmat, xdtw_b, (((0,), (0,)), ((), ())), preferred_element_type=F32)

        cum2 = cumg * math.log2(math.e)
        cumt2 = cumtg * math.log2(math.e)
        diag = []
        for q in range(hpg // heads_per_tile):
            xq = xdt_b[:, q * LANES:(q + 1) * LANES]
            ms, xr = [], []
            for r in range(heads_per_tile):
                h = q * heads_per_tile + r
                dec = jnp.exp2(cum2[:, h:h + 1] - cumt2[h:h + 1, :])
                ms.append(jnp.where(causal, cb * dec, 0.0).astype(BF16))
                xr.append(xq * lane_masks[r])
            diag.append(jnp.dot(jnp.concatenate(ms, axis=1), jnp.concatenate(xr, axis=0),
                                preferred_element_type=F32))
        y = y + jnp.concatenate(diag, axis=1) + dskip_ref[:, gcols] * x
        y = y * zs_ref[0, :, gcols].astype(F32)
        y = y * lax.rsqrt(jnp.mean(y * y, axis=-1, keepdims=True) + RMS_EPS) * nw_ref[:, gcols]
        y_ref[0, :, gcols] = y.astype(y_ref.dtype)

    @pl.when(c == nc - 1)
    def _():
        st_ref[0] = stt_ref[...].T.reshape(st_ref.shape[1:])


def _ssd_prompt(xbc, zs, dt, cum, cumt, layer, conv_w, conv_b, dskip_x, norm_w, e_mat, *, di, n, p, groups,
                zero_fill_shape=None):
    b, l, cd = xbc.shape
    c = SSD_CHUNK
    heads = di // p
    hpg = heads // groups
    gps = next(g for g in (8, 4, 2, 1) if groups % g == 0)
    gw = hpg * p * gps
    nn = n * gps
    hp = dt.shape[-1]
    xoff, boff, coff = 0, di // nn, (di + groups * n) // nn

    def seg(width, off_blocks):
        return pl.BlockSpec((1, c, width), lambda i, g, k: (i, k, off_blocks + g))

    def wseg(rows, width, off_blocks):
        return _lspec(layer, (rows, width), lambda i, g, k: (0, off_blocks + g))

    small = pl.BlockSpec((1, c, hp), lambda i, g, k: (i, k, 0))
    kw = conv_w.shape[1]
    ng, nc = groups // gps, l // c
    extra_specs, extra_shapes = [], []
    if zero_fill_shape is not None and zero_fill_shape[0] % (b * ng * nc):
        zero_fill_shape = None
    if zero_fill_shape is not None:
        per_step = zero_fill_shape[0] // (b * ng * nc)
        tail = tuple(zero_fill_shape[1:])
        extra_specs = [pl.BlockSpec((per_step,) + tail,
                                    lambda i, g, k: ((i * ng + g) * nc + k,) + (0,) * len(tail))]
        extra_shapes = [jax.ShapeDtypeStruct(tuple(zero_fill_shape), F32)]
    outs = pl.pallas_call(
        functools.partial(_ssd_prompt_kernel, hpg=hpg, p=p, gps=gps, zero_fill=zero_fill_shape is not None),
        grid=(b, ng, nc),
        in_specs=[
            seg(gw, xoff), seg(nn, boff), seg(nn, coff), seg(gw, 0), small, small, small,
            wseg(kw, gw, xoff), wseg(kw, nn, boff), wseg(kw, nn, coff),
            wseg(1, gw, xoff), wseg(1, nn, boff), wseg(1, nn, coff),
            wseg(1, gw, 0), wseg(1, gw, 0),
            pl.BlockSpec(e_mat.shape, lambda i, g, k: (0, 0)),
        ],
        out_specs=[
            pl.BlockSpec((1, c, gw), lambda i, g, k: (i, k, g)),
            pl.BlockSpec((1, hpg * gps, p, n), lambda i, g, k: (i, g, 0, 0)),
        ] + extra_specs,
        out_shape=[jax.ShapeDtypeStruct((b, l, di), BF16), jax.ShapeDtypeStruct((b, heads, p, n), F32)]
        + extra_shapes,
        scratch_shapes=[
            pltpu.VMEM((n, gw), F32),
            pltpu.VMEM((kw - 1, 8, gw), F32), pltpu.VMEM((kw - 1, 8, nn), F32), pltpu.VMEM((kw - 1, 8, nn), F32),
        ],
        compiler_params=_params("parallel", "parallel", "arbitrary"),
        name="ssd_prompt",
    )(xbc, xbc, xbc, zs, dt, cum, cumt, conv_w, conv_w, conv_w, conv_b, conv_b, conv_b, dskip_x, norm_w, e_mat)
    return (outs[0], outs[1], outs[2] if extra_shapes else None)


def _ssd_sample_pre_kernel(xbc_ref, w_ref, b_ref, dtr_ref, bias_ref, alog_ref, e_ref, cbuf_ref, stack_hbm_ref,
                           xs_ref, xdt_ref, dae_ref, bm_ref, cm_ref, ncbuf_ref, *, di, gn):
    del stack_hbm_ref
    kw = w_ref.shape[0]
    xbc = xbc_ref[...]
    acc = b_ref[...]
    for k in range(kw - 1):
        acc = acc + cbuf_ref[k] * w_ref[k:k + 1, :]
    acc = _silu(acc + xbc * w_ref[kw - 1:kw, :])
    for k in range(kw - 2):
        ncbuf_ref[k] = cbuf_ref[k + 1]
    ncbuf_ref[kw - 2] = xbc
    xs = acc[:, 0:di]
    dt = _softplus(dtr_ref[...] + bias_ref[...])
    dae_ref[...] = jnp.exp(dt * (-jnp.exp(alog_ref[...])))
    hi, mid = _split2(dt)
    dt_x = jnp.dot(jnp.concatenate([hi, mid], axis=1), e_ref[...], preferred_element_type=F32)
    xs_ref[...] = xs
    xdt_ref[...] = xs * dt_x
    bm_ref[...] = acc[:, di:di + gn]
    cm_ref[...] = acc[:, di + gn:di + 2 * gn]


def _ssd_sample_pre(xbc, dtr, conv_state_t, layer, stacked_so_far, conv_w, conv_b, bias, alog, e_mat, *, di, gn):
    bs, cd = xbc.shape
    hp = dtr.shape[1]
    kw = conv_w.shape[1]
    tb = _tile(bs, 32)
    row = lambda w: pl.BlockSpec((tb, w), lambda i: (i, 0))
    return _layer_state_call(
        functools.partial(_ssd_sample_pre_kernel, di=di, gn=gn),
        layer=layer, state=conv_state_t, batch_axis=2, stacked_so_far=stacked_so_far, grid=(bs // tb,), tb=tb,
        other_operands=[xbc, conv_w, conv_b, dtr, bias, alog, e_mat],
        other_in_specs=[row(cd), _lspec(layer, (kw, cd)), _lspec(layer, (1, cd)), row(hp), _lspec(layer, (1, hp)),
                        _lspec(layer, (1, hp)), pl.BlockSpec(e_mat.shape, lambda i: (0, 0))],
        other_out_specs=[row(di), row(di), row(hp), row(gn), row(gn)],
        other_out_shapes=[jax.ShapeDtypeStruct((bs, di), F32)] * 2 + [jax.ShapeDtypeStruct((bs, hp), F32)]
        + [jax.ShapeDtypeStruct((bs, gn), F32)] * 2,
        name="ssd_sample_pre")


def _ssd_sample_state_kernel(dae_ref, xdt_ref, bm_ref, cm_ref, s_ref, stack_hbm_ref, y_ref, so_ref, *, groups):
    del stack_hbm_ref
    tb, heads, p, n = s_ref.shape
    di = heads * p
    gw = di // groups
    grow = lax.broadcasted_iota(jnp.int32, (groups, di), 0)
    gcol = lax.broadcasted_iota(jnp.int32, (groups, di), 1) // gw
    own = grow == gcol
    fill = (-groups) % 16
    for b in range(tb):
        seq = pl.program_id(0) * tb + b
        xg = jnp.where(own, jnp.broadcast_to(xdt_ref[b], (groups, di)), 0.0)
        bmat = bm_ref[b]
        if fill:
            xg = jnp.concatenate([xg, jnp.zeros((fill, di), F32)], axis=0)
            bmat = jnp.concatenate([bmat, jnp.zeros((fill, n), F32)], axis=0)
        upd = lax.dot_general(xg.astype(BF16), bmat.astype(BF16), (((0,), (0,)), ((), ())),
                              preferred_element_type=F32)
        for h in range(heads):
            so_ref[b, h] = s_ref[b, h] * dae_ref[seq, h] + upd[h * p:(h + 1) * p, :]
        s_new = so_ref[b].reshape(di, n).astype(BF16)
        yg = lax.dot_general(cm_ref[b].astype(BF16), s_new, (((1,), (1,)), ((), ())),
                             preferred_element_type=F32)
        y_ref[b] = jnp.sum(jnp.where(own, yg, 0.0), axis=0, keepdims=True)


def _ssd_sample_state(ssd_state, layer, stacked_so_far, xdt, dae, bm, cm, *, groups):
    _, bs, heads, p, n = ssd_state.shape
    di = heads * p
    tb = _tile(bs, 4)
    xdt3 = xdt.reshape(bs, 1, di)
    bm3, cm3 = bm.reshape(bs, groups, n), cm.reshape(bs, groups, n)
    rspec = pl.BlockSpec((tb, 1, di), lambda i: (i, 0, 0))
    gspec = pl.BlockSpec((tb, groups, n), lambda i: (i, 0, 0))
    y, stacked = _layer_state_call(
        functools.partial(_ssd_sample_state_kernel, groups=groups),
        layer=layer, state=ssd_state, batch_axis=1, stacked_so_far=stacked_so_far, grid=(bs // tb,), tb=tb,
        other_operands=[dae, xdt3, bm3, cm3],
        other_in_specs=[pl.BlockSpec(memory_space=pltpu.SMEM), rspec, gspec, gspec],
        other_out_specs=[rspec], other_out_shapes=[jax.ShapeDtypeStruct((bs, 1, di), F32)],
        name="ssd_sample_state")
    return stacked, y.reshape(bs, di)


def _ssd_sample_post_kernel(y_ref, xs_ref, zs_ref, dskip_ref, nw_ref, o_ref, *, groups):
    y = (y_ref[...] + dskip_ref[...] * xs_ref[...]) * zs_ref[...].astype(F32)
    gw = y.shape[1] // groups
    for g in range(groups):
        cols = slice(g * gw, (g + 1) * gw)
        yg = y[:, cols]
        yg = yg * lax.rsqrt(jnp.mean(yg * yg, axis=-1, keepdims=True) + RMS_EPS) * nw_ref[:, cols]
        o_ref[:, cols] = yg.astype(o_ref.dtype)


def _ssd_sample_post(y, xs, zs, layer, dskip_x, norm_w, *, groups):
    bs, di = y.shape
    full = pl.BlockSpec((bs, di), lambda i: (0, 0))
    return pl.pallas_call(
        functools.partial(_ssd_sample_post_kernel, groups=groups),
        grid=(1,),
        in_specs=[full, full, full, _lspec(layer, (1, di)), _lspec(layer, (1, di))],
        out_specs=full,
        out_shape=jax.ShapeDtypeStruct(y.shape, BF16),
        compiler_params=_params("arbitrary"),
        name="ssd_sample_post",
    )(y, xs, zs, dskip_x, norm_w)


def _merge_proj_ln_kernel(y_ref, sg_ref, a_ref, x_ref, wb_ref, wo_ref, g_ref, b_ref, o_ref, ob_ref, *, alpha):
    tm = y_ref.shape[0]
    halves = 2 if tm % 16 == 0 else 1
    for r in range(halves):
        rows = slice(r * tm // halves, (r + 1) * tm // halves)
        bb = jnp.dot(y_ref[rows, :], wb_ref[...], preferred_element_type=F32)
        merged = (a_ref[rows, :] + sg_ref[rows, :].astype(F32) * bb).astype(BF16)
        f = jnp.dot(merged, wo_ref[...], preferred_element_type=F32)
        h = _layer_norm(alpha * x_ref[rows, :] + f, g_ref[...], b_ref[...])
        o_ref[rows, :] = h
        ob_ref[rows, :] = h.astype(BF16)


def _merge_proj_ln(y, sg, a, x, layer, wb, wo, g, b, *, alpha):
    m, k = y.shape
    d = wo.shape[2]
    tm = _tile(m, 256)
    row = lambda width: pl.BlockSpec((tm, width), lambda i: (i, 0))
    once = pl.Buffered(1)
    return pl.pallas_call(
        functools.partial(_merge_proj_ln_kernel, alpha=alpha),
        grid=(m // tm,),
        in_specs=[row(k), pl.BlockSpec((tm, d), lambda i: (i, 1)),
                  row(d), row(d), _lspec(layer, (k, d), pipeline_mode=once),
                  _lspec(layer, (d, d), pipeline_mode=once), _lspec(layer, (1, d)), _lspec(layer, (1, d))],
        out_specs=[row(d), row(d)],
        out_shape=[jax.ShapeDtypeStruct((m, d), F32), jax.ShapeDtypeStruct((m, d), BF16)],
        compiler_params=_params("parallel"),
        name="merge_proj_ln",
    )(y, sg, a, x, wb, wo, g, b)


def _attn_prompt_kernel(hb_ref, h_ref, k_ref, v_ref, wq_ref, wo_ref, g_ref, b_ref, o_ref, ob_ref, *,
                        heads, alpha):
    q = jnp.dot(hb_ref[0], wq_ref[...], preferred_element_type=F32)
    hd = q.shape[1] // heads
    scale = 1.0 / math.sqrt(hd)
    kb = k_ref[0].astype(BF16)
    vb = v_ref[0].astype(BF16)
    outs = []
    for h in range(heads):
        cols = slice(h * hd, (h + 1) * hd)
        s = lax.dot_general(q[:, cols].astype(BF16), kb[:, cols], (((1,), (1,)), ((), ())),
                            preferred_element_type=F32) * scale
        e = jnp.exp(s - jnp.max(s, axis=-1, keepdims=True))
        pr = e / jnp.sum(e, axis=-1, keepdims=True)
        outs.append(jnp.dot(pr.astype(BF16), vb[:, cols], preferred_element_type=F32))
    o = jnp.concatenate(outs, axis=1).astype(BF16)
    ca = jnp.dot(o, wo_ref[...], preferred_element_type=F32)
    hn = _layer_norm(alpha * h_ref[0] + ca, g_ref[...], b_ref[...])
    o_ref[0] = hn
    ob_ref[0] = hn.astype(BF16)


def _attn_prompt(hb, h, mk, mv, layer, wq, wo, g, b, *, heads, alpha):
    bsz, l, d = h.shape
    nm, md = mk.shape[1], mk.shape[2]
    tq = _tile(l, 512)
    row = pl.BlockSpec((1, tq, d), lambda i, j: (i, j, 0))
    mem = pl.BlockSpec((1, nm, md), lambda i, j: (i, 0, 0))
    return pl.pallas_call(
        functools.partial(_attn_prompt_kernel, heads=heads, alpha=alpha),
        grid=(bsz, l // tq),
        in_specs=[row, row, mem, mem, _lspec(layer, (d, md)), _lspec(layer, (md, d)), _lspec(layer, (1, d)),
                  _lspec(layer, (1, d))],
        out_specs=[row, row],
        out_shape=[jax.ShapeDtypeStruct((bsz, l, d), F32), jax.ShapeDtypeStruct((bsz, l, d), BF16)],
        compiler_params=_params("parallel", "parallel"),
        name="attn_prompt",
    )(hb, h, mk, mv, wq, wo, g, b)


def _attn_sample_kernel(hb_ref, h_ref, k_ref, v_ref, wq_ref, wo_ref, g_ref, b_ref, o_ref, ob_ref, att_ref, *,
                        heads, alpha):
    tb = hb_ref.shape[0]
    rows, hd = k_ref.shape[1], k_ref.shape[2]
    reps = 8 // heads
    scale = 1.0 / math.sqrt(hd)
    hb = hb_ref[...]
    qs = [jnp.dot(hb, wq_ref[:, h * hd:(h + 1) * hd], preferred_element_type=F32) for h in range(heads)]

    def over_copies(x, op):
        out = x
        for r in range(1, reps):
            out = op(out, pltpu.roll(x, r * heads, axis=0))
        return out

    for bi in range(tb):
        q8 = jnp.concatenate([q[bi:bi + 1, :] for q in qs] * reps, axis=0)
        k3 = k_ref[bi].reshape(rows // 8, 8, hd)
        v3 = v_ref[bi].reshape(rows // 8, 8, hd)
        s = jnp.sum(k3 * q8[None], axis=-1, keepdims=True) * scale
        m = over_copies(jnp.max(s, axis=0), jnp.maximum)
        e = jnp.exp(s - m[None])
        den = over_copies(jnp.sum(e, axis=0), jnp.add)
        o8 = over_copies(jnp.sum((e / den[None]) * v3, axis=0), jnp.add)
        att_ref[bi] = o8[0:heads, :]
    ca = None
    for h in range(heads):
        t = jnp.dot(att_ref[:, h, :].astype(BF16), wo_ref[h * hd:(h + 1) * hd, :], preferred_element_type=F32)
        ca = t if ca is None else ca + t
    hn = _layer_norm(alpha * h_ref[...] + ca, g_ref[...], b_ref[...])
    o_ref[...] = hn
    ob_ref[...] = hn.astype(BF16)


def _attn_sample(hb, h, cache_k, cache_v, layer, wq, wo, g, b, *, alpha):
    bs, d = h.shape
    depth, _, nm, heads, hd = cache_k.shape
    assert 8 % heads == 0
    md = heads * hd
    ck = cache_k.reshape(depth, bs, nm * heads, hd)
    cv = cache_v.reshape(depth, bs, nm * heads, hd)
    tb = _tile(bs, 8)
    row = pl.BlockSpec((tb, d), lambda i: (i, 0))
    mem = _lspec(layer, (tb, nm * heads, hd), lambda i: (i, 0, 0))
    return pl.pallas_call(
        functools.partial(_attn_sample_kernel, heads=heads, alpha=alpha),
        grid=(bs // tb,),
        in_specs=[row, row, mem, mem, _lspec(layer, (d, md)), _lspec(layer, (md, d)), _lspec(layer, (1, d)),
                  _lspec(layer, (1, d))],
        out_specs=[row, row],
        out_shape=[jax.ShapeDtypeStruct((bs, d), F32), jax.ShapeDtypeStruct((bs, d), BF16)],
        scratch_shapes=[pltpu.VMEM((tb, heads, hd), F32)],
        compiler_params=_params("parallel"),
        name="attn_sample",
    )(hb, h, ck, cv, wq, wo, g, b)


def _ffn_kernel(hb_ref, h_hbm_ref, wg_ref, wu_ref, wd_ref, g_ref, b_ref, o_ref, ob_ref, h_ref, h_sem, *, alpha):
    i = pl.program_id(0)
    f = pl.program_id(1)
    tm, d = o_ref.shape
    h_copy = pltpu.make_async_copy(h_hbm_ref.at[pl.ds(pl.multiple_of(i * tm, 8), tm), :], h_ref, h_sem)

    @pl.when(f == 0)
    def _():
        h_copy.start()
        o_ref[...] = jnp.zeros(o_ref.shape, F32)

    x = hb_ref[...]
    gate = jnp.dot(x, wg_ref[...], preferred_element_type=F32)
    up = jnp.dot(x, wu_ref[...], preferred_element_type=F32)
    act = (_silu(gate) * up).astype(BF16)
    for c in range(2):
        cols = slice(c * d // 2, (c + 1) * d // 2)
        o_ref[:, cols] += jnp.dot(act, wd_ref[:, cols], preferred_element_type=F32)

    @pl.when(f == pl.num_programs(1) - 1)
    def _():
        h_copy.wait()
        for r in range(4):
            rows = slice(r * tm // 4, (r + 1) * tm // 4)
            hn = _layer_norm(alpha * h_ref[rows, :] + o_ref[rows, :], g_ref[...], b_ref[...])
            o_ref[rows, :] = hn
            ob_ref[rows, :] = hn.astype(BF16)


def _ffn(hb, h, layer, wg, wu, wd, g, b, *, alpha):
    m, d = h.shape
    ff = wg.shape[2]
    tm, tf = _tile(m, 1024), 512
    assert ff % tf == 0
    row = pl.BlockSpec((tm, d), lambda i, f: (i, 0))
    row_once = pl.BlockSpec((tm, d), lambda i, f: (i, 0), pipeline_mode=pl.Buffered(1))
    return pl.pallas_call(
        functools.partial(_ffn_kernel, alpha=alpha),
        grid=(m // tm, ff // tf),
        in_specs=[row, pl.BlockSpec(memory_space=pl.ANY), _lspec(layer, (d, tf), lambda i, f: (0, f)),
                  _lspec(layer, (d, tf), lambda i, f: (0, f)), _lspec(layer, (tf, d), lambda i, f: (f, 0)),
                  _lspec(layer, (1, d)), _lspec(layer, (1, d))],
        out_specs=[row_once, row],
        out_shape=[jax.ShapeDtypeStruct((m, d), F32), jax.ShapeDtypeStruct((m, d), BF16)],
        scratch_shapes=[pltpu.VMEM((tm, d), F32), pltpu.SemaphoreType.DMA(())],
        compiler_params=_params("parallel", "arbitrary"),
        name="ffn",
    )(hb, h, wg, wu, wd, g, b)


def _expansion_matrix(n_in, reps, n_pad):
    r = jnp.arange(n_pad)[:, None]
    c = jnp.arange(n_in * reps)[None, :] // reps
    e = ((r == c) & (r < n_in)).astype(BF16)
    return jnp.concatenate([e, e], axis=0)


def kernel(x_prompt, x_sample, mem_prompt, state_ssd, state_conv, state_pool, cache_mem_k, cache_mem_v, w_in, w_pool, pool_scale, w_pool_br, conv_w, conv_b, dt_bias, a_log, d_skip, ssd_norm_w, w_ssd_br, w_out, ln_mix_g, ln_mix_b, w_mq, w_mk, w_mv, w_mo, ln_mem_g, ln_mem_b, w_ffn_gate, w_ffn_up, w_ffn_down, ln_ffn_g, ln_ffn_b):
    bp, seq, d = x_prompt.shape
    bs = x_sample.shape[0]
    depth = w_in.shape[0]
    _, _, heads, p, n = state_ssd.shape
    di = heads * p
    cd = conv_w.shape[2]
    groups = (cd - di) // (2 * n)
    gn = groups * n
    hpg = heads // groups
    n_mem, mem_heads, mem_hd = cache_mem_k.shape[2:]
    md = mem_heads * mem_hd
    nbuf = state_pool.shape[2]
    kw1 = state_conv.shape[2]
    alpha = (2 * depth) ** 0.25
    assert seq % SSD_CHUNK == 0 and heads <= LANES and LANES % p == 0 and hpg % (LANES // p) == 0

    s3 = d + di + cd
    s4 = s3 + heads

    w_in_t = jnp.transpose(w_in, (0, 2, 1))
    w_dt = jnp.pad(w_in_t[:, s3:s4, :], ((0, 0), (0, LANES - heads), (0, 0)))
    bf = lambda w: w.astype(BF16)
    wpool_b, wpbr_b, wsbr_b, wout_b = bf(w_pool), bf(w_pool_br), bf(w_ssd_br), bf(w_out)
    wq_b, wk_b, wv_b, wo_b = bf(w_mq), bf(w_mk), bf(w_mv), bf(w_mo)
    wg_b, wu_b, wd_b = bf(w_ffn_gate), bf(w_ffn_up), bf(w_ffn_down)
    vec = lambda v: v.astype(F32).reshape(depth, 1, v.shape[-1])
    pad_h = lambda v: jnp.pad(v.astype(F32), ((0, 0), (0, LANES - heads))).reshape(depth, 1, LANES)
    scale, cbias, nw = vec(pool_scale), vec(conv_b), vec(ssd_norm_w)
    bias_p, alog_p = pad_h(dt_bias), pad_h(a_log)
    dskip_x = vec(jnp.repeat(d_skip, p, axis=1))
    mix_g, mix_b, mem_g, mem_b, ffn_g, ffn_b = (vec(v) for v in (ln_mix_g, ln_mix_b, ln_mem_g, ln_mem_b,
                                                                   ln_ffn_g, ln_ffn_b))
    e_group = _expansion_matrix(hpg, p, LANES)
    e_all = _expansion_matrix(heads, p, LANES)

    pool_state_t = jnp.transpose(state_pool, (0, 2, 1, 3))
    conv_state_t = jnp.transpose(state_conv, (0, 2, 1, 3))

    hp, hs = x_prompt.reshape(bp * seq, d), x_sample.reshape(bs, d)
    hp_b, hs_b = hp.astype(BF16), hs.astype(BF16)
    memory_bf = mem_prompt.reshape(bp * n_mem, d).astype(BF16)

    outs = {k: [] for k in ("ssd_p", "conv_p", "pool_p", "mk", "mv")}
    new_ssd_s = new_conv_s = new_pool_s = None
    for l in range(depth):

        def in_proj(hb):
            mm = functools.partial(_mm, hb, layer=l, w_is_nk=True)
            u = mm(w_in_t, col0=0, n=d, name="in_u")
            zs = mm(w_in_t, col0=d, n=di, act="silu", out_dtype=BF16, name="in_z")
            xbc = mm(w_in_t, col0=d + di, n=cd, name="in_xbc")
            dtr = mm(w_dt, name="in_dt")
            gates = mm(w_in_t, col0=s4, n=2 * d, act="sigmoid", out_dtype=BF16, name="in_gates")
            return u, zs, xbc, dtr, gates

        def tail(h, y, gates, a_gated, attn):
            h1, h1b = _merge_proj_ln(y, gates, a_gated, h, l, wsbr_b, wout_b, mix_g, mix_b, alpha=alpha)
            h2, h2b = attn(h1, h1b)
            return _ffn(h2b, h2, l, wg_b, wu_b, wd_b, ffn_g, ffn_b, alpha=alpha)

        u, zs, xbc, dtr, gates = in_proj(hp_b)
        u3, xbc3 = u.reshape(bp, seq, d), xbc.reshape(bp, seq, cd)
        a_gated = _pool_prompt(u3, gates.reshape(bp, seq, 2 * d), l, wpool_b, scale, wpbr_b).reshape(bp * seq, d)
        dt, cum, cumt = _dt_prep(dtr, l, bias_p, alog_p)
        r3 = lambda a: a.reshape(bp, seq, a.shape[-1])
        fill = (depth * bs,) + state_ssd.shape[2:] if l == 0 else None
        y, st, zeros = _ssd_prompt(xbc3, r3(zs), r3(dt), r3(cum), r3(cumt), l, conv_w, cbias, dskip_x, nw, e_group,
                                   di=di, n=n, p=p, groups=groups, zero_fill_shape=fill)
        if zeros is not None:
            new_ssd_s = zeros.reshape(state_ssd.shape)
        mk_p = _mm(memory_bf, wk_b, l, name="mem_k").reshape(bp, n_mem, md)
        mv_p = _mm(memory_bf, wv_b, l, name="mem_v").reshape(bp, n_mem, md)

        def attn_p(h1, h1b):
            o, ob = _attn_prompt(h1b.reshape(bp, seq, d), h1.reshape(bp, seq, d), mk_p, mv_p, l, wq_b, wo_b,
                                 mem_g, mem_b, heads=mem_heads, alpha=alpha)
            return o.reshape(bp * seq, d), ob.reshape(bp * seq, d)

        hp, hp_b = tail(hp, y.reshape(bp * seq, di), gates, a_gated, attn_p)
        outs["ssd_p"].append(st)
        outs["conv_p"].append(xbc3[:, seq - kw1:, :])
        outs["pool_p"].append(u3[:, seq - nbuf:, :])
        outs["mk"].append(mk_p.reshape(bp, n_mem, mem_heads, mem_hd))
        outs["mv"].append(mv_p.reshape(bp, n_mem, mem_heads, mem_hd))

        u, zs, xbc, dtr, gates = in_proj(hs_b)
        a_gated, new_pool_s = _pool_sample(u, gates, pool_state_t, l, new_pool_s, wpool_b, scale, wpbr_b)
        xs, xdt, dae, bm, cm, new_conv_s = _ssd_sample_pre(xbc, dtr, conv_state_t, l, new_conv_s, conv_w, cbias,
                                                           bias_p, alog_p, e_all, di=di, gn=gn)
        new_ssd_s, ys = _ssd_sample_state(state_ssd, l, new_ssd_s, xdt, dae, bm, cm, groups=groups)
        y = _ssd_sample_post(ys, xs, zs, l, dskip_x, nw, groups=groups)

        def attn_s(h1, h1b):
            return _attn_sample(h1b, h1, cache_mem_k, cache_mem_v, l, wq_b, wo_b, mem_g, mem_b, alpha=alpha)

        hs, hs_b = tail(hs, y, gates, a_gated, attn_s)

    stack = lambda k: jnp.stack(outs[k])
    return (hp.reshape(bp, seq, d), hs.reshape(bs, 1, d), stack("ssd_p"), stack("conv_p"), stack("pool_p"),
            stack("mk"), stack("mv"), new_ssd_s, jnp.transpose(new_conv_s, (0, 2, 1, 3)),
            jnp.transpose(new_pool_s, (0, 2, 1, 3)))
```

```python
import functools
import math

import jax
import jax.numpy as jnp
from jax import lax
from jax.experimental import pallas as pl
from jax.experimental.pallas import tpu as pltpu

F32 = jnp.float32
BF16 = jnp.bfloat16

POOL_WINDOWS = (2, 4, 8, 16)
PAST_LEN = 16384
SSD_CHUNK = 128
LN_EPS = 1e-5
RMS_EPS = 1e-5
LANES = 128
VMEM_LIMIT_BYTES = 56 * 1024 * 1024


def _params(*sem):
    return pltpu.CompilerParams(dimension_semantics=sem, vmem_limit_bytes=VMEM_LIMIT_BYTES)


def _silu(x):
    return x * jax.nn.sigmoid(x)


def _softplus(x):
    return jnp.maximum(x, 0.0) + jnp.log(1.0 + jnp.exp(-jnp.abs(x)))


def _layer_norm(x, g, b):
    mu = jnp.mean(x, axis=-1, keepdims=True)
    xc = x - mu
    var = jnp.mean(xc * xc, axis=-1, keepdims=True)
    return xc * lax.rsqrt(var + LN_EPS) * g + b


def _split2(x):
    hi = x.astype(BF16)
    mid = (x - hi.astype(F32)).astype(BF16)
    return hi, mid


def _tile(n, pref):
    t = min(n, pref)
    while n % t:
        t //= 2
    return t


def _lspec(layer, block, index_map=None, **kwargs):
    nd = len(block)
    if index_map is None:
        index_map = lambda *g: (0,) * nd
    return pl.BlockSpec((None,) + tuple(block), lambda *g: (layer,) + tuple(index_map(*g)), **kwargs)


def _mm_kernel(x_ref, w_ref, o_ref, *, act, w_is_nk):
    if w_is_nk:
        acc = lax.dot_general(x_ref[...], w_ref[0].astype(BF16), (((1,), (1,)), ((), ())),
                              preferred_element_type=F32)
    else:
        acc = jnp.dot(x_ref[...], w_ref[...], preferred_element_type=F32)
    if act == "silu":
        acc = _silu(acc)
    elif act == "sigmoid":
        acc = jax.nn.sigmoid(acc)
    o_ref[...] = acc.astype(o_ref.dtype)


def _mm(x, w, layer, *, col0=0, n=None, act=None, out_dtype=F32, w_is_nk=False, name):
    m, k = x.shape
    n = w.shape[1 if w_is_nk else 2] if n is None else n
    if w_is_nk:
        tm = _tile(m, 2048)
        tn = _tile(n, 512 if tm > 512 else 1024)
        assert col0 % 16 == 0
        w_spec = pl.BlockSpec((pl.Element(1), pl.Element(tn), pl.Element(k)),
                              lambda i, j: (layer, pl.multiple_of(col0 + j * tn, 16), 0))
    else:
        tm, tn = _tile(m, 1024), _tile(n, 1024)
        assert col0 % tn == 0
        w_spec = _lspec(layer, (k, tn), lambda i, j: (0, col0 // tn + j))
    return pl.pallas_call(
        functools.partial(_mm_kernel, act=act, w_is_nk=w_is_nk),
        grid=(m // tm, n // tn),
        in_specs=[pl.BlockSpec((tm, k), lambda i, j: (i, 0)), w_spec],
        out_specs=pl.BlockSpec((tm, tn), lambda i, j: (i, j)),
        out_shape=jax.ShapeDtypeStruct((m, n), out_dtype),
        compiler_params=_params("parallel", "arbitrary"),
        name=name,
    )(x, w)


def _pool_project(pooled_ref, wg_ref, scale_ref, wbr_ref, sg, n_groups, gc):
    for g in range(n_groups):
        cols = slice(g * gc, (g + 1) * gc)
        pg = jnp.dot(pooled_ref[:, cols], wg_ref[g], preferred_element_type=F32) * scale_ref[:, cols]
        pooled_ref[:, cols] = pg.astype(BF16)
    a = jnp.dot(pooled_ref[...], wbr_ref[...], preferred_element_type=F32)
    return a * sg.astype(F32)


def _pool_prompt_kernel(u_ref, wg_ref, scale_ref, wbr_ref, sg_ref, o_ref, carry_ref, pooled_ref, *, tl, gc):
    j = pl.program_id(1)

    @pl.when(j == 0)
    def _():
        carry_ref[...] = jnp.zeros(carry_ref.shape, F32)

    nt = tl // 8
    sub = lax.broadcasted_iota(jnp.int32, (nt, 8, gc), 1)
    pos = j * tl + lax.broadcasted_iota(jnp.int32, (tl, 1), 0)
    for g, w in enumerate(POOL_WINDOWS):
        cols = slice(g * gc, (g + 1) * gc)
        ug = u_ref[0, :, cols]
        acc, s, level = ug, 1, 0
        while s < w:
            a3 = acc.reshape(nt, 8, gc)
            before = jnp.concatenate([carry_ref[level, :, cols][None], a3[:nt - 1]], axis=0)
            carry_ref[level, :, cols] = acc[tl - 8:tl, :]
            shifted = before if s == 8 else pltpu.roll(jnp.where(sub >= 8 - s, before, a3), s, axis=1)
            acc = acc + shifted.reshape(tl, gc)
            s, level = 2 * s, level + 1
        cnt = jnp.minimum(w, pos + 1).astype(F32)
        pooled_ref[:, cols] = (acc / cnt - ug).astype(BF16)
    o_ref[0] = _pool_project(pooled_ref, wg_ref, scale_ref, wbr_ref, sg_ref[0], len(POOL_WINDOWS), gc)


def _pool_prompt(u, sg, layer, wg, scale, wbr):
    b, l, d = u.shape
    _, ng, gc, _ = wg.shape
    tl = _tile(l, 256)
    assert all(w in (1, 2, 4, 8, 16) for w in POOL_WINDOWS) and tl % 8 == 0
    levels = max(POOL_WINDOWS).bit_length() - 1
    tok = pl.BlockSpec((1, tl, d), lambda i, j: (i, j, 0))
    return pl.pallas_call(
        functools.partial(_pool_prompt_kernel, tl=tl, gc=gc),
        grid=(b, l // tl),
        in_specs=[tok, _lspec(layer, (ng, gc, gc)), _lspec(layer, (1, d)), _lspec(layer, (d, d)), tok],
        out_specs=tok,
        out_shape=jax.ShapeDtypeStruct((b, l, d), F32),
        scratch_shapes=[pltpu.VMEM((levels, 8, d), F32), pltpu.VMEM((tl, d), BF16)],
        compiler_params=_params("parallel", "arbitrary"),
        name="pool_prompt",
    )(u, wg, scale, wbr, sg)


def _layer_state_call(kern, *, layer, state, batch_axis, stacked_so_far, grid, tb, other_operands, other_in_specs,
                      other_out_specs, other_out_shapes, scratch_shapes=(), name):
    blk = tuple(tb if a == batch_axis else s for a, s in enumerate(state.shape))[1:]
    spec = _lspec(layer, blk, lambda i: tuple(i if a == batch_axis else 0 for a in range(1, state.ndim)))
    if stacked_so_far is None:
        stacked_so_far = jnp.zeros(state.shape, state.dtype)
    operands = list(other_operands) + [state, stacked_so_far]
    in_specs = list(other_in_specs) + [spec, pl.BlockSpec(memory_space=pl.ANY)]
    return pl.pallas_call(
        kern,
        grid=grid,
        in_specs=in_specs,
        out_specs=list(other_out_specs) + [spec],
        out_shape=list(other_out_shapes) + [jax.ShapeDtypeStruct(state.shape, state.dtype)],
        input_output_aliases={len(operands) - 1: len(other_out_shapes)},
        scratch_shapes=list(scratch_shapes),
        compiler_params=_params("parallel"),
        name=name,
    )(*operands)


def _pool_sample_kernel(u_ref, wg_ref, scale_ref, wbr_ref, sg_ref, buf_ref, stack_hbm_ref,
                        o_ref, nbuf_ref, pooled_ref, *, gc):
    del stack_hbm_ref
    nbuf = buf_ref.shape[0]
    u = u_ref[...]
    for r in range(nbuf - 1):
        nbuf_ref[r] = buf_ref[r + 1]
    nbuf_ref[nbuf - 1] = u
    for g, w in enumerate(POOL_WINDOWS):
        cols = slice(g * gc, (g + 1) * gc)
        ug = u[:, cols]
        s = ug
        for k in range(1, w):
            s = s + buf_ref[nbuf - k, :, cols]
        cnt = float(min(w, PAST_LEN + 1))
        pooled_ref[:, cols] = (s / cnt - ug).astype(BF16)
    o_ref[...] = _pool_project(pooled_ref, wg_ref, scale_ref, wbr_ref, sg_ref[...], len(POOL_WINDOWS), gc)


def _pool_sample(u, sg, pool_state_t, layer, stacked_so_far, wg, scale, wbr):
    bs, d = u.shape
    _, ng, gc, _ = wg.shape
    tb = _tile(bs, 32)
    row = pl.BlockSpec((tb, d), lambda i: (i, 0))
    return _layer_state_call(
        functools.partial(_pool_sample_kernel, gc=gc),
        layer=layer, state=pool_state_t, batch_axis=2, stacked_so_far=stacked_so_far, grid=(bs // tb,), tb=tb,
        other_operands=[u, wg, scale, wbr, sg],
        other_in_specs=[row, _lspec(layer, (ng, gc, gc)), _lspec(layer, (1, d)), _lspec(layer, (d, d)), row],
        other_out_specs=[row], other_out_shapes=[jax.ShapeDtypeStruct((bs, d), F32)],
        scratch_shapes=[pltpu.VMEM((tb, d), BF16)], name="pool_sample")


def _dt_prep_kernel(dtr_ref, bias_ref, alog_ref, dt_ref, cum_ref, cumt_ref):
    c = SSD_CHUNK
    dt_ref[...] = _softplus(dtr_ref[...] + bias_ref[...])
    a = -jnp.exp(alog_ref[...])
    row = lax.broadcasted_iota(jnp.int32, (c, dtr_ref.shape[1]), 0)
    for ch in range(dtr_ref.shape[0] // c):
        rows = slice(ch * c, (ch + 1) * c)
        cum = dt_ref[rows, :] * a
        k = 1
        while k < c:
            cum = cum + jnp.where(row >= k, pltpu.roll(cum, k, axis=0), 0.0)
            k *= 2
        cum_ref[rows, :] = cum
        cumt_ref[rows, :] = cum.T


def _dt_prep(dtr, layer, bias, alog):
    t, hp = dtr.shape
    c = SSD_CHUNK * _tile(t // SSD_CHUNK, 8)
    spec = pl.BlockSpec((c, hp), lambda i: (i, 0))
    return pl.pallas_call(
        _dt_prep_kernel,
        grid=(t // c,),
        in_specs=[spec, _lspec(layer, (1, hp)), _lspec(layer, (1, hp))],
        out_specs=[spec, spec, spec],
        out_shape=[jax.ShapeDtypeStruct((t, hp), F32)] * 3,
        compiler_params=_params("parallel"),
        name="dt_prep",
    )(dtr, bias, alog)


def _conv_silu(x, carry_ref, w_ref, b_ref):
    rows, wd = x.shape
    kw = w_ref.shape[0]
    nt = rows // 8
    last_row = lax.broadcasted_iota(jnp.int32, (nt, 8, wd), 1) == 7
    t = x * w_ref[0:1, :]
    for k in range(1, kw):
        t3 = t.reshape(nt, 8, wd)
        before = jnp.concatenate([carry_ref[k - 1][None], t3[:nt - 1]], axis=0)
        carry_ref[k - 1] = t[rows - 8:rows, :]
        shifted = pltpu.roll(jnp.where(last_row, before, t3), 1, axis=1).reshape(rows, wd)
        t = x * w_ref[k:k + 1, :] + shifted
    return _silu(t + b_ref[...])


def _ssd_prompt_kernel(xs_ref, bm_ref, cm_ref, zs_ref, dt_ref, cum_ref, cumt_ref,
                       wx_ref, wb_ref, wc_ref, bx_ref, bb_ref, bc_ref, dskip_ref, nw_ref, e_ref,
                       y_ref, st_ref, *rest, hpg, p, gps, zero_fill):
    stt_ref, hx_ref, hb_ref, hc_ref = rest[-4:]
    c = pl.program_id(2)
    nc = pl.num_programs(2)
    rows = xs_ref.shape[1]
    n = bm_ref.shape[2] // gps
    gw = hpg * p
    if zero_fill:
        rest[0][...] = jnp.zeros(rest[0].shape, F32)

    @pl.when(c == 0)
    def _():
        for r in (stt_ref, hx_ref, hb_ref, hc_ref):
            r[...] = jnp.zeros(r.shape, F32)

    x_all = _conv_silu(xs_ref[0], hx_ref, wx_ref, bx_ref)
    b_all = _conv_silu(bm_ref[0], hb_ref, wb_ref, bb_ref).astype(BF16)
    c_all = _conv_silu(cm_ref[0], hc_ref, wc_ref, bc_ref).astype(BF16)
    causal = lax.broadcasted_iota(jnp.int32, (rows, rows), 0) >= lax.broadcasted_iota(jnp.int32, (rows, rows), 1)
    lane = lax.broadcasted_iota(jnp.int32, (rows, LANES), 1)
    heads_per_tile = LANES // p
    lane_masks = [jnp.where(lane // p == r, 1.0, 0.0).astype(BF16) for r in range(heads_per_tile)]

    for gi in range(gps):
        g = pl.program_id(1) * gps + gi
        gcols = slice(gi * gw, (gi + 1) * gw)
        x = x_all[:, gcols]
        bmat = b_all[:, gi * n:(gi + 1) * n]
        cmat = c_all[:, gi * n:(gi + 1) * n]
        shift = (LANES - g * hpg) % LANES
        dtg = pltpu.roll(dt_ref[0], shift, axis=1)
        cumg = pltpu.roll(cum_ref[0], shift, axis=1)
        cumtg = cumt_ref[0, pl.ds(pl.multiple_of(g * hpg, hpg), hpg), :]
        last = cumg[rows - 1:rows, :]
        stack = jnp.concatenate([dtg, jnp.exp(cumg), jnp.exp(last - cumg)], axis=0)
        hi, mid = _split2(stack)
        expanded = jnp.dot(jnp.concatenate([hi, mid], axis=1), e_ref[...], preferred_element_type=F32)
        dt_x, ecum_x, ws_x = expanded[0:rows], expanded[rows:2 * rows], expanded[2 * rows:3 * rows]

        xdt = x * dt_x
        xdt_b = xdt.astype(BF16)
        xdtw_b = (xdt * ws_x).astype(BF16)
        cb = lax.dot_general(cmat, bmat, (((1,), (1,)), ((), ())), preferred_element_type=F32)
        st_old = stt_ref[:, gcols]
        y = jnp.dot(cmat, st_old.astype(BF16), preferred_element_type=F32) * ecum_x
        stt_ref[:, gcols] = st_old * ecum_x[rows - 1:rows, :] + lax.dot_general(
            bmat, xdtw_b, (((0,), (0,)), ((), ())), preferred_element_type=F32)

        cum2 = cumg * math.log2(math.e)
        cumt2 = cumtg * math.log2(math.e)
        diag = []
        for q in range(hpg // heads_per_tile):
            xq = xdt_b[:, q * LANES:(q + 1) * LANES]
            ms, xr = [], []
            for r in range(heads_per_tile):
                h = q * heads_per_tile + r
                dec = jnp.exp2(cum2[:, h:h + 1] - cumt2[h:h + 1, :])
                ms.append(jnp.where(causal, cb * dec, 0.0).astype(BF16))
                xr.append(xq * lane_masks[r])
            diag.append(jnp.dot(jnp.concatenate(ms, axis=1), jnp.concatenate(xr, axis=0),
                                preferred_element_type=F32))
        y = y + jnp.concatenate(diag, axis=1) + dskip_ref[:, gcols] * x
        y = y * zs_ref[0, :, gcols].astype(F32)
        y = y * lax.rsqrt(jnp.mean(y * y, axis=-1, keepdims=True) + RMS_EPS) * nw_ref[:, gcols]
        y_ref[0, :, gcols] = y.astype(y_ref.dtype)

    @pl.when(c == nc - 1)
    def _():
        st_ref[0] = stt_ref[...].T.reshape(st_ref.shape[1:])


def _ssd_prompt(xbc, zs, dt, cum, cumt, layer, conv_w, conv_b, dskip_x, norm_w, e_mat, *, di, n, p, groups,
                zero_fill_shape=None):
    b, l, cd = xbc.shape
    c = SSD_CHUNK
    heads = di // p
    hpg = heads // groups
    gps = next(g for g in (8, 4, 2, 1) if groups % g == 0)
    gw = hpg * p * gps
    nn = n * gps
    hp = dt.shape[-1]
    xoff, boff, coff = 0, di // nn, (di + groups * n) // nn

    def seg(width, off_blocks):
        return pl.BlockSpec((1, c, width), lambda i, g, k: (i, k, off_blocks + g))

    def wseg(rows, width, off_blocks):
        return _lspec(layer, (rows, width), lambda i, g, k: (0, off_blocks + g))

    small = pl.BlockSpec((1, c, hp), lambda i, g, k: (i, k, 0))
    kw = conv_w.shape[1]
    ng, nc = groups // gps, l // c
    extra_specs, extra_shapes = [], []
    if zero_fill_shape is not None and zero_fill_shape[0] % (b * ng * nc):
        zero_fill_shape = None
    if zero_fill_shape is not None:
        per_step = zero_fill_shape[0] // (b * ng * nc)
        tail = tuple(zero_fill_shape[1:])
        extra_specs = [pl.BlockSpec((per_step,) + tail,
                                    lambda i, g, k: ((i * ng + g) * nc + k,) + (0,) * len(tail))]
        extra_shapes = [jax.ShapeDtypeStruct(tuple(zero_fill_shape), F32)]
    outs = pl.pallas_call(
        functools.partial(_ssd_prompt_kernel, hpg=hpg, p=p, gps=gps, zero_fill=zero_fill_shape is not None),
        grid=(b, ng, nc),
        in_specs=[
            seg(gw, xoff), seg(nn, boff), seg(nn, coff), seg(gw, 0), small, small, small,
            wseg(kw, gw, xoff), wseg(kw, nn, boff), wseg(kw, nn, coff),
            wseg(1, gw, xoff), wseg(1, nn, boff), wseg(1, nn, coff),
            wseg(1, gw, 0), wseg(1, gw, 0),
            pl.BlockSpec(e_mat.shape, lambda i, g, k: (0, 0)),
        ],
        out_specs=[
            pl.BlockSpec((1, c, gw), lambda i, g, k: (i, k, g)),
            pl.BlockSpec((1, hpg * gps, p, n), lambda i, g, k: (i, g, 0, 0)),
        ] + extra_specs,
        out_shape=[jax.ShapeDtypeStruct((b, l, di), BF16), jax.ShapeDtypeStruct((b, heads, p, n), F32)]
        + extra_shapes,
        scratch_shapes=[
            pltpu.VMEM((n, gw), F32),
            pltpu.VMEM((kw - 1, 8, gw), F32), pltpu.VMEM((kw - 1, 8, nn), F32), pltpu.VMEM((kw - 1, 8, nn), F32),
        ],
        compiler_params=_params("parallel", "parallel", "arbitrary"),
        name="ssd_prompt",
    )(xbc, xbc, xbc, zs, dt, cum, cumt, conv_w, conv_w, conv_w, conv_b, conv_b, conv_b, dskip_x, norm_w, e_mat)
    return (outs[0], outs[1], outs[2] if extra_shapes else None)


def _ssd_sample_pre_kernel(xbc_ref, w_ref, b_ref, dtr_ref, bias_ref, alog_ref, e_ref, cbuf_ref, stack_hbm_ref,
                           xs_ref, xdt_ref, dae_ref, bm_ref, cm_ref, ncbuf_ref, *, di, gn):
    del stack_hbm_ref
    kw = w_ref.shape[0]
    xbc = xbc_ref[...]
    acc = b_ref[...]
    for k in range(kw - 1):
        acc = acc + cbuf_ref[k] * w_ref[k:k + 1, :]
    acc = _silu(acc + xbc * w_ref[kw - 1:kw, :])
    for k in range(kw - 2):
        ncbuf_ref[k] = cbuf_ref[k + 1]
    ncbuf_ref[kw - 2] = xbc
    xs = acc[:, 0:di]
    dt = _softplus(dtr_ref[...] + bias_ref[...])
    dae_ref[...] = jnp.exp(dt * (-jnp.exp(alog_ref[...])))
    hi, mid = _split2(dt)
    dt_x = jnp.dot(jnp.concatenate([hi, mid], axis=1), e_ref[...], preferred_element_type=F32)
    xs_ref[...] = xs
    xdt_ref[...] = xs * dt_x
    bm_ref[...] = acc[:, di:di + gn]
    cm_ref[...] = acc[:, di + gn:di + 2 * gn]


def _ssd_sample_pre(xbc, dtr, conv_state_t, layer, stacked_so_far, conv_w, conv_b, bias, alog, e_mat, *, di, gn):
    bs, cd = xbc.shape
    hp = dtr.shape[1]
    kw = conv_w.shape[1]
    tb = _tile(bs, 32)
    row = lambda w: pl.BlockSpec((tb, w), lambda i: (i, 0))
    return _layer_state_call(
        functools.partial(_ssd_sample_pre_kernel, di=di, gn=gn),
        layer=layer, state=conv_state_t, batch_axis=2, stacked_so_far=stacked_so_far, grid=(bs // tb,), tb=tb,
        other_operands=[xbc, conv_w, conv_b, dtr, bias, alog, e_mat],
        other_in_specs=[row(cd), _lspec(layer, (kw, cd)), _lspec(layer, (1, cd)), row(hp), _lspec(layer, (1, hp)),
                        _lspec(layer, (1, hp)), pl.BlockSpec(e_mat.shape, lambda i: (0, 0))],
        other_out_specs=[row(di), row(di), row(hp), row(gn), row(gn)],
        other_out_shapes=[jax.ShapeDtypeStruct((bs, di), F32)] * 2 + [jax.ShapeDtypeStruct((bs, hp), F32)]
        + [jax.ShapeDtypeStruct((bs, gn), F32)] * 2,
        name="ssd_sample_pre")


def _ssd_sample_state_kernel(dae_ref, xdt_ref, bm_ref, cm_ref, s_ref, stack_hbm_ref, y_ref, so_ref, *, groups):
    del stack_hbm_ref
    tb, heads, p, n = s_ref.shape
    di = heads * p
    gw = di // groups
    grow = lax.broadcasted_iota(jnp.int32, (groups, di), 0)
    gcol = lax.broadcasted_iota(jnp.int32, (groups, di), 1) // gw
    own = grow == gcol
    fill = (-groups) % 16
    for b in range(tb):
        seq = pl.program_id(0) * tb + b
        xg = jnp.where(own, jnp.broadcast_to(xdt_ref[b], (groups, di)), 0.0)
        bmat = bm_ref[b]
        if fill:
            xg = jnp.concatenate([xg, jnp.zeros((fill, di), F32)], axis=0)
            bmat = jnp.concatenate([bmat, jnp.zeros((fill, n), F32)], axis=0)
        upd = lax.dot_general(xg.astype(BF16), bmat.astype(BF16), (((0,), (0,)), ((), ())),
                              preferred_element_type=F32)
        for h in range(heads):
            so_ref[b, h] = s_ref[b, h] * dae_ref[seq, h] + upd[h * p:(h + 1) * p, :]
        s_new = so_ref[b].reshape(di, n).astype(BF16)
        yg = lax.dot_general(cm_ref[b].astype(BF16), s_new, (((1,), (1,)), ((), ())),
                             preferred_element_type=F32)
        y_ref[b] = jnp.sum(jnp.where(own, yg, 0.0), axis=0, keepdims=True)


def _ssd_sample_state(ssd_state, layer, stacked_so_far, xdt, dae, bm, cm, *, groups):
    _, bs, heads, p, n = ssd_state.shape
    di = heads * p
    tb = _tile(bs, 4)
    xdt3 = xdt.reshape(bs, 1, di)
    bm3, cm3 = bm.reshape(bs, groups, n), cm.reshape(bs, groups, n)
    rspec = pl.BlockSpec((tb, 1, di), lambda i: (i, 0, 0))
    gspec = pl.BlockSpec((tb, groups, n), lambda i: (i, 0, 0))
    y, stacked = _layer_state_call(
        functools.partial(_ssd_sample_state_kernel, groups=groups),
        layer=layer, state=ssd_state, batch_axis=1, stacked_so_far=stacked_so_far, grid=(bs // tb,), tb=tb,
        other_operands=[dae, xdt3, bm3, cm3],
        other_in_specs=[pl.BlockSpec(memory_space=pltpu.SMEM), rspec, gspec, gspec],
        other_out_specs=[rspec], other_out_shapes=[jax.ShapeDtypeStruct((bs, 1, di), F32)],
        name="ssd_sample_state")
    return stacked, y.reshape(bs, di)


def _ssd_sample_post_kernel(y_ref, xs_ref, zs_ref, dskip_ref, nw_ref, o_ref, *, groups):
    y = (y_ref[...] + dskip_ref[...] * xs_ref[...]) * zs_ref[...].astype(F32)
    gw = y.shape[1] // groups
    for g in range(groups):
        cols = slice(g * gw, (g + 1) * gw)
        yg = y[:, cols]
        yg = yg * lax.rsqrt(jnp.mean(yg * yg, axis=-1, keepdims=True) + RMS_EPS) * nw_ref[:, cols]
        o_ref[:, cols] = yg.astype(o_ref.dtype)


def _ssd_sample_post(y, xs, zs, layer, dskip_x, norm_w, *, groups):
    bs, di = y.shape
    full = pl.BlockSpec((bs, di), lambda i: (0, 0))
    return pl.pallas_call(
        functools.partial(_ssd_sample_post_kernel, groups=groups),
        grid=(1,),
        in_specs=[full, full, full, _lspec(layer, (1, di)), _lspec(layer, (1, di))],
        out_specs=full,
        out_shape=jax.ShapeDtypeStruct(y.shape, BF16),
        compiler_params=_params("arbitrary"),
        name="ssd_sample_post",
    )(y, xs, zs, dskip_x, norm_w)


def _merge_proj_ln_kernel(y_ref, sg_ref, a_ref, x_ref, wb_ref, wo_ref, g_ref, b_ref, o_ref, ob_ref, *, alpha):
    tm = y_ref.shape[0]
    halves = 2 if tm % 16 == 0 else 1
    for r in range(halves):
        rows = slice(r * tm // halves, (r + 1) * tm // halves)
        bb = jnp.dot(y_ref[rows, :], wb_ref[...], preferred_element_type=F32)
        merged = (a_ref[rows, :] + sg_ref[rows, :].astype(F32) * bb).astype(BF16)
        f = jnp.dot(merged, wo_ref[...], preferred_element_type=F32)
        h = _layer_norm(alpha * x_ref[rows, :] + f, g_ref[...], b_ref[...])
        o_ref[rows, :] = h
        ob_ref[rows, :] = h.astype(BF16)


def _merge_proj_ln(y, sg, a, x, layer, wb, wo, g, b, *, alpha):
    m, k = y.shape
    d = wo.shape[2]
    tm = _tile(m, 256)
    row = lambda width: pl.BlockSpec((tm, width), lambda i: (i, 0))
    once = pl.Buffered(1)
    return pl.pallas_call(
        functools.partial(_merge_proj_ln_kernel, alpha=alpha),
        grid=(m // tm,),
        in_specs=[row(k), pl.BlockSpec((tm, d), lambda i: (i, 1)),
                  row(d), row(d), _lspec(layer, (k, d), pipeline_mode=once),
                  _lspec(layer, (d, d), pipeline_mode=once), _lspec(layer, (1, d)), _lspec(layer, (1, d))],
        out_specs=[row(d), row(d)],
        out_shape=[jax.ShapeDtypeStruct((m, d), F32), jax.ShapeDtypeStruct((m, d), BF16)],
        compiler_params=_params("parallel"),
        name="merge_proj_ln",
    )(y, sg, a, x, wb, wo, g, b)


def _attn_prompt_kernel(hb_ref, h_ref, k_ref, v_ref, wq_ref, wo_ref, g_ref, b_ref, o_ref, ob_ref, *,
                        heads, alpha):
    q = jnp.dot(hb_ref[0], wq_ref[...], preferred_element_type=F32)
    hd = q.shape[1] // heads
    scale = 1.0 / math.sqrt(hd)
    kb = k_ref[0].astype(BF16)
    vb = v_ref[0].astype(BF16)
    outs = []
    for h in range(heads):
        cols = slice(h * hd, (h + 1) * hd)
        s = lax.dot_general(q[:, cols].astype(BF16), kb[:, cols], (((1,), (1,)), ((), ())),
                            preferred_element_type=F32) * scale
        e = jnp.exp(s - jnp.max(s, axis=-1, keepdims=True))
        pr = e / jnp.sum(e, axis=-1, keepdims=True)
        outs.append(jnp.dot(pr.astype(BF16), vb[:, cols], preferred_element_type=F32))
    o = jnp.concatenate(outs, axis=1).astype(BF16)
    ca = jnp.dot(o, wo_ref[...], preferred_element_type=F32)
    hn = _layer_norm(alpha * h_ref[0] + ca, g_ref[...], b_ref[...])
    o_ref[0] = hn
    ob_ref[0] = hn.astype(BF16)


def _attn_prompt(hb, h, mk, mv, layer, wq, wo, g, b, *, heads, alpha):
    bsz, l, d = h.shape
    nm, md = mk.shape[1], mk.shape[2]
    tq = _tile(l, 512)
    row = pl.BlockSpec((1, tq, d), lambda i, j: (i, j, 0))
    mem = pl.BlockSpec((1, nm, md), lambda i, j: (i, 0, 0))
    return pl.pallas_call(
        functools.partial(_attn_prompt_kernel, heads=heads, alpha=alpha),
        grid=(bsz, l // tq),
        in_specs=[row, row, mem, mem, _lspec(layer, (d, md)), _lspec(layer, (md, d)), _lspec(layer, (1, d)),
                  _lspec(layer, (1, d))],
        out_specs=[row, row],
        out_shape=[jax.ShapeDtypeStruct((bsz, l, d), F32), jax.ShapeDtypeStruct((bsz, l, d), BF16)],
        compiler_params=_params("parallel", "parallel"),
        name="attn_prompt",
    )(hb, h, mk, mv, wq, wo, g, b)


def _attn_sample_kernel(hb_ref, h_ref, k_ref, v_ref, wq_ref, wo_ref, g_ref, b_ref, o_ref, ob_ref, att_ref, *,
                        heads, alpha):
    tb = hb_ref.shape[0]
    rows, hd = k_ref.shape[1], k_ref.shape[2]
    reps = 8 // heads
    scale = 1.0 / math.sqrt(hd)
    hb = hb_ref[...]
    qs = [jnp.dot(hb, wq_ref[:, h * hd:(h + 1) * hd], preferred_element_type=F32) for h in range(heads)]

    def over_copies(x, op):
        out = x
        for r in range(1, reps):
            out = op(out, pltpu.roll(x, r * heads, axis=0))
        return out

    for bi in range(tb):
        q8 = jnp.concatenate([q[bi:bi + 1, :] for q in qs] * reps, axis=0)
        k3 = k_ref[bi].reshape(rows // 8, 8, hd)
        v3 = v_ref[bi].reshape(rows // 8, 8, hd)
        s = jnp.sum(k3 * q8[None], axis=-1, keepdims=True) * scale
        m = over_copies(jnp.max(s, axis=0), jnp.maximum)
        e = jnp.exp(s - m[None])
        den = over_copies(jnp.sum(e, axis=0), jnp.add)
        o8 = over_copies(jnp.sum((e / den[None]) * v3, axis=0), jnp.add)
        att_ref[bi] = o8[0:heads, :]
    ca = None
    for h in range(heads):
        t = jnp.dot(att_ref[:, h, :].astype(BF16), wo_ref[h * hd:(h + 1) * hd, :], preferred_element_type=F32)
        ca = t if ca is None else ca + t
    hn = _layer_norm(alpha * h_ref[...] + ca, g_ref[...], b_ref[...])
    o_ref[...] = hn
    ob_ref[...] = hn.astype(BF16)


def _attn_sample(hb, h, cache_k, cache_v, layer, wq, wo, g, b, *, alpha):
    bs, d = h.shape
    depth, _, nm, heads, hd = cache_k.shape
    assert 8 % heads == 0
    md = heads * hd
    ck = cache_k.reshape(depth, bs, nm * heads, hd)
    cv = cache_v.reshape(depth, bs, nm * heads, hd)
    tb = _tile(bs, 8)
    row = pl.BlockSpec((tb, d), lambda i: (i, 0))
    mem = _lspec(layer, (tb, nm * heads, hd), lambda i: (i, 0, 0))
    return pl.pallas_call(
        functools.partial(_attn_sample_kernel, heads=heads, alpha=alpha),
        grid=(bs // tb,),
        in_specs=[row, row, mem, mem, _lspec(layer, (d, md)), _lspec(layer, (md, d)), _lspec(layer, (1, d)),
                  _lspec(layer, (1, d))],
        out_specs=[row, row],
        out_shape=[jax.ShapeDtypeStruct((bs, d), F32), jax.ShapeDtypeStruct((bs, d), BF16)],
        scratch_shapes=[pltpu.VMEM((tb, heads, hd), F32)],
        compiler_params=_params("parallel"),
        name="attn_sample",
    )(hb, h, ck, cv, wq, wo, g, b)


def _ffn_kernel(hb_ref, h_hbm_ref, wg_ref, wu_ref, wd_ref, g_ref, b_ref, o_hbm_ref, ob_ref,
                h_ref, acc_ref, h_sem, o_sem, *, alpha):
    i = pl.program_id(0)
    f = pl.program_id(1)
    ni, nf = pl.num_programs(0), pl.num_programs(1)
    tm, d = acc_ref.shape
    rows_of = lambda t: pl.ds(pl.multiple_of(t * tm, 8), tm)
    h_copy = pltpu.make_async_copy(h_hbm_ref.at[rows_of(i), :], h_ref, h_sem)
    o_copy = lambda t: pltpu.make_async_copy(acc_ref, o_hbm_ref.at[rows_of(t), :], o_sem)

    x = hb_ref[...]
    gate = jnp.dot(x, wg_ref[...], preferred_element_type=F32)
    up = jnp.dot(x, wu_ref[...], preferred_element_type=F32)
    act = (_silu(gate) * up).astype(BF16)

    @pl.when((f == 0) & (i > 0))
    def _():
        o_copy(i - 1).wait()

    @pl.when(f == 0)
    def _():
        h_copy.start()
        acc_ref[...] = jnp.zeros(acc_ref.shape, F32)

    for c in range(2):
        cols = slice(c * d // 2, (c + 1) * d // 2)
        acc_ref[:, cols] += jnp.dot(act, wd_ref[:, cols], preferred_element_type=F32)

    @pl.when(f == nf - 1)
    def _():
        h_copy.wait()
        for r in range(4):
            rows = slice(r * tm // 4, (r + 1) * tm // 4)
            hn = _layer_norm(alpha * h_ref[rows, :] + acc_ref[rows, :], g_ref[...], b_ref[...])
            acc_ref[rows, :] = hn
            ob_ref[rows, :] = hn.astype(BF16)
        o_copy(i).start()

    @pl.when((f == nf - 1) & (i == ni - 1))
    def _():
        o_copy(i).wait()


def _ffn(hb, h, layer, wg, wu, wd, g, b, *, alpha):
    m, d = h.shape
    ff = wg.shape[2]
    tm, tf = _tile(m, 1024), 512
    assert ff % tf == 0
    row = pl.BlockSpec((tm, d), lambda i, f: (i, 0))
    hbm = pl.BlockSpec(memory_space=pl.ANY)
    return pl.pallas_call(
        functools.partial(_ffn_kernel, alpha=alpha),
        grid=(m // tm, ff // tf),
        in_specs=[row, hbm, _lspec(layer, (d, tf), lambda i, f: (0, f)),
                  _lspec(layer, (d, tf), lambda i, f: (0, f)), _lspec(layer, (tf, d), lambda i, f: (f, 0)),
                  _lspec(layer, (1, d)), _lspec(layer, (1, d))],
        out_specs=[hbm, row],
        out_shape=[jax.ShapeDtypeStruct((m, d), F32), jax.ShapeDtypeStruct((m, d), BF16)],
        scratch_shapes=[pltpu.VMEM((tm, d), F32), pltpu.VMEM((tm, d), F32), pltpu.SemaphoreType.DMA(()),
                        pltpu.SemaphoreType.DMA(())],
        compiler_params=_params("arbitrary", "arbitrary"),
        name="ffn",
    )(hb, h, wg, wu, wd, g, b)


def _expansion_matrix(n_in, reps, n_pad):
    r = jnp.arange(n_pad)[:, None]
    c = jnp.arange(n_in * reps)[None, :] // reps
    e = ((r == c) & (r < n_in)).astype(BF16)
    return jnp.concatenate([e, e], axis=0)


def kernel(x_prompt, x_sample, mem_prompt, state_ssd, state_conv, state_pool, cache_mem_k, cache_mem_v, w_in, w_pool, pool_scale, w_pool_br, conv_w, conv_b, dt_bias, a_log, d_skip, ssd_norm_w, w_ssd_br, w_out, ln_mix_g, ln_mix_b, w_mq, w_mk, w_mv, w_mo, ln_mem_g, ln_mem_b, w_ffn_gate, w_ffn_up, w_ffn_down, ln_ffn_g, ln_ffn_b):
    bp, seq, d = x_prompt.shape
    bs = x_sample.shape[0]
    depth = w_in.shape[0]
    _, _, heads, p, n = state_ssd.shape
    di = heads * p
    cd = conv_w.shape[2]
    groups = (cd - di) // (2 * n)
    gn = groups * n
    hpg = heads // groups
    n_mem, mem_heads, mem_hd = cache_mem_k.shape[2:]
    md = mem_heads * mem_hd
    nbuf = state_pool.shape[2]
    kw1 = state_conv.shape[2]
    alpha = (2 * depth) ** 0.25
    assert seq % SSD_CHUNK == 0 and heads <= LANES and LANES % p == 0 and hpg % (LANES // p) == 0

    s3 = d + di + cd
    s4 = s3 + heads

    w_in_t = jnp.transpose(w_in, (0, 2, 1))
    w_dt = jnp.pad(w_in_t[:, s3:s4, :], ((0, 0), (0, LANES - heads), (0, 0)))
    bf = lambda w: w.astype(BF16)
    wpool_b, wpbr_b, wsbr_b, wout_b = bf(w_pool), bf(w_pool_br), bf(w_ssd_br), bf(w_out)
    wq_b, wk_b, wv_b, wo_b = bf(w_mq), bf(w_mk), bf(w_mv), bf(w_mo)
    wg_b, wu_b, wd_b = bf(w_ffn_gate), bf(w_ffn_up), bf(w_ffn_down)
    vec = lambda v: v.astype(F32).reshape(depth, 1, v.shape[-1])
    pad_h = lambda v: jnp.pad(v.astype(F32), ((0, 0), (0, LANES - heads))).reshape(depth, 1, LANES)
    scale, cbias, nw = vec(pool_scale), vec(conv_b), vec(ssd_norm_w)
    bias_p, alog_p = pad_h(dt_bias), pad_h(a_log)
    dskip_x = vec(jnp.repeat(d_skip, p, axis=1))
    mix_g, mix_b, mem_g, mem_b, ffn_g, ffn_b = (vec(v) for v in (ln_mix_g, ln_mix_b, ln_mem_g, ln_mem_b,
                                                                   ln_ffn_g, ln_ffn_b))
    e_group = _expansion_matrix(hpg, p, LANES)
    e_all = _expansion_matrix(heads, p, LANES)

    pool_state_t = jnp.transpose(state_pool, (0, 2, 1, 3))
    conv_state_t = jnp.transpose(state_conv, (0, 2, 1, 3))

    hp, hs = x_prompt.reshape(bp * seq, d), x_sample.reshape(bs, d)
    hp_b, hs_b = hp.astype(BF16), hs.astype(BF16)
    memory_bf = mem_prompt.reshape(bp * n_mem, d).astype(BF16)

    outs = {k: [] for k in ("ssd_p", "conv_p", "pool_p", "mk", "mv")}
    new_ssd_s = new_conv_s = new_pool_s = None
    for l in range(depth):

        def in_proj(hb):
            mm = functools.partial(_mm, hb, layer=l, w_is_nk=True)
            u = mm(w_in_t, col0=0, n=d, name="in_u")
            zs = mm(w_in_t, col0=d, n=di, act="silu", out_dtype=BF16, name="in_z")
            xbc = mm(w_in_t, col0=d + di, n=cd, name="in_xbc")
            dtr = mm(w_dt, name="in_dt")
            gates = mm(w_in_t, col0=s4, n=2 * d, act="sigmoid", out_dtype=BF16, name="in_gates")
            return u, zs, xbc, dtr, gates

        def tail(h, y, gates, a_gated, attn):
            h1, h1b = _merge_proj_ln(y, gates, a_gated, h, l, wsbr_b, wout_b, mix_g, mix_b, alpha=alpha)
            h2, h2b = attn(h1, h1b)
            return _ffn(h2b, h2, l, wg_b, wu_b, wd_b, ffn_g, ffn_b, alpha=alpha)

        u, zs, xbc, dtr, gates = in_proj(hp_b)
        u3, xbc3 = u.reshape(bp, seq, d), xbc.reshape(bp, seq, cd)
        a_gated = _pool_prompt(u3, gates.reshape(bp, seq, 2 * d), l, wpool_b, scale, wpbr_b).reshape(bp * seq, d)
        dt, cum, cumt = _dt_prep(dtr, l, bias_p, alog_p)
        r3 = lambda a: a.reshape(bp, seq, a.shape[-1])
        fill = (depth * bs,) + state_ssd.shape[2:] if l == 0 else None
        y, st, zeros = _ssd_prompt(xbc3, r3(zs), r3(dt), r3(cum), r3(cumt), l, conv_w, cbias, dskip_x, nw, e_group,
                                   di=di, n=n, p=p, groups=groups, zero_fill_shape=fill)
        if zeros is not None:
            new_ssd_s = zeros.reshape(state_ssd.shape)
        mk_p = _mm(memory_bf, wk_b, l, name="mem_k").reshape(bp, n_mem, md)
        mv_p = _mm(memory_bf, wv_b, l, name="mem_v").reshape(bp, n_mem, md)

        def attn_p(h1, h1b):
            o, ob = _attn_prompt(h1b.reshape(bp, seq, d), h1.reshape(bp, seq, d), mk_p, mv_p, l, wq_b, wo_b,
                                 mem_g, mem_b, heads=mem_heads, alpha=alpha)
            return o.reshape(bp * seq, d), ob.reshape(bp * seq, d)

        hp, hp_b = tail(hp, y.reshape(bp * seq, di), gates, a_gated, attn_p)
        outs["ssd_p"].append(st)
        outs["conv_p"].append(xbc3[:, seq - kw1:, :])
        outs["pool_p"].append(u3[:, seq - nbuf:, :])
        outs["mk"].append(mk_p.reshape(bp, n_mem, mem_heads, mem_hd))
        outs["mv"].append(mv_p.reshape(bp, n_mem, mem_heads, mem_hd))

        u, zs, xbc, dtr, gates = in_proj(hs_b)
        a_gated, new_pool_s = _pool_sample(u, gates, pool_state_t, l, new_pool_s, wpool_b, scale, wpbr_b)
        xs, xdt, dae, bm, cm, new_conv_s = _ssd_sample_pre(xbc, dtr, conv_state_t, l, new_conv_s, conv_w, cbias,
                                                           bias_p, alog_p, e_all, di=di, gn=gn)
        new_ssd_s, ys = _ssd_sample_state(state_ssd, l, new_ssd_s, xdt, dae, bm, cm, groups=groups)
        y = _ssd_sample_post(ys, xs, zs, l, dskip_x, nw, groups=groups)

        def attn_s(h1, h1b):
            return _attn_sample(h1b, h1, cache_mem_k, cache_mem_v, l, wq_b, wo_b, mem_g, mem_b, alpha=alpha)

        hs, hs_b = tail(hs, y, gates, a_gated, attn_s)

    stack = lambda k: jnp.stack(outs[k])
    return (hp.reshape(bp, seq, d), hs.reshape(bs, 1, d), stack("ssd_p"), stack("conv_p"), stack("pool_p"),
            stack("mk"), stack("mv"), new_ssd_s, jnp.transpose(new_conv_s, (0, 2, 1, 3)),
            jnp.transpose(new_pool_s, (0, 2, 1, 3)))
```
